```python
import math
import jax, jax.numpy as jnp
from jax import lax
import numpy as np


D_MODEL = 2048
BATCH = 4
SEQ = 4096
DEPTH = 4
DEC_BATCH = 8
DEC_SEQ = 4096
PAST_LEN = 128

HYENA_WIDTH = 1024
HYENA_ORDER = 2
FILTER_HIDDEN = 64
N_BANDS = 16
FILTER_FEAT = 1 + 2 * N_BANDS
FAST_DECAY_PCT = 0.3
SLOW_DECAY_PCT = 1.5
DECAY_TARGET = 1e-2
ATTN_GROUPS = ((128, 1), (512, 4), (2048, 16))
N_GROUPS = len(ATTN_GROUPS)
HEADS_PER_GROUP = 8
HEAD_DIM = 64
N_ATTN_HEADS = N_GROUPS * HEADS_PER_GROUP
ATTN_WIDTH = N_ATTN_HEADS * HEAD_DIM
ATTN_OUT = HEADS_PER_GROUP * HEAD_DIM
REL_BUCKETS = 32
REL_MAX_DIST = 1024
D_FF = 5632
NORM_EPS = 1e-6
N_MOD = 6
IN_COLS = 3 * HYENA_WIDTH + 3 * ATTN_WIDTH + 2 * D_MODEL
NEG_INF = -1e30

kernel_name = 'hybrid_hyena_dilated_attn_encoder'


def rms_norm(x, g):
    xf = x.astype(jnp.float32)
    y = xf * lax.rsqrt(jnp.mean(xf * xf, axis=-1, keepdims=True) + NORM_EPS)
    return (y * g.astype(jnp.float32)).astype(x.dtype)


def dwconv3(x, w, b):
    xp = jnp.pad(x, ((0, 0), (1, 1), (0, 0)))
    return xp[:, :-2] * w[0] + xp[:, 1:-1] * w[1] + xp[:, 2:] * w[2] + b


def hyena_filters(L, w1, b1, fr1, w2, b2, fr2, w3):
    f32 = jnp.float32
    t = jnp.linspace(0.0, 1.0, L, dtype=f32)[:, None]
    pos = jnp.arange(L, dtype=f32)[:, None]
    bands = jnp.linspace(1e-4, N_BANDS - 1, N_BANDS, dtype=f32)[None, :]
    ang = (2.0 * math.pi / L) * pos * bands
    feat = jnp.concatenate([t, jnp.cos(ang), -jnp.sin(ang)], axis=-1)
    h = jnp.sin(fr1.astype(f32) * (feat @ w1.astype(f32) + b1.astype(f32)))
    h = jnp.sin(fr2.astype(f32) * (h @ w2.astype(f32) + b2.astype(f32)))
    h = (h @ w3.astype(f32)).reshape(L, 2, HYENA_ORDER, HYENA_WIDTH)
    max_decay = math.log(DECAY_TARGET) / FAST_DECAY_PCT
    min_decay = math.log(DECAY_TARGET) / SLOW_DECAY_PCT
    deltas = jnp.abs(jnp.linspace(min_decay, max_decay, HYENA_WIDTH, dtype=f32))
    h = h * jnp.exp(-t * deltas)[:, None, None, :]
    k2 = jnp.concatenate([h[:, 0], jnp.zeros((1, HYENA_ORDER, HYENA_WIDTH), f32), h[:0:-1, 1]], axis=0)
    k2 = k2 / jnp.sum(jnp.abs(k2), axis=0, keepdims=True)
    return jnp.fft.rfft(k2, axis=0)


def hyena_mix(u, conv_w, conv_b, K, hy_bias):
    L = u.shape[1]
    f32 = jnp.float32
    uc = dwconv3(u, conv_w, conv_b).astype(f32)
    v, x1, x2 = jnp.split(uc, 3, axis=-1)
    bias = hy_bias.astype(f32)
    z = v
    for o, xg in enumerate((x1, x2)):
        y = jnp.fft.irfft(jnp.fft.rfft(z, n=2 * L, axis=1) * K[None, :, o, :], n=2 * L, axis=1)[:, :L]
        z = xg * (y + bias[o] * z)
    return z.astype(u.dtype)


def t5_bucket(rel):
    nb = REL_BUCKETS // 2
    ret = (rel > 0).astype(np.int32) * nb
    n = np.abs(rel)
    max_exact = nb // 2
    large = max_exact + (np.log(np.maximum(n, 1) / max_exact) / np.log(REL_MAX_DIST / max_exact)
                         * (nb - max_exact)).astype(np.int32)
    large = np.minimum(large, nb - 1)
    return ret + np.where(n < max_exact, n, large)


def dilated_group_attn(q, k, v, rel_bias_g, dil, n_side):
    B, L, H, E = q.shape
    blk = n_side
    chunk = dil * blk
    Lp = -(-L // chunk) * chunk
    M = Lp // dil
    nb = M // blk

    def to_classes(a):
        a = jnp.pad(a, ((0, 0), (0, Lp - L), (0, 0), (0, 0)))
        return a.reshape(B, M, dil, H, E).transpose(0, 2, 3, 1, 4)

    def windows(a):
        a = jnp.pad(a, ((0, 0), (0, 0), (0, 0), (blk, blk), (0, 0))).reshape(B, dil, H, nb + 2, blk, E)
        return jnp.concatenate([a[:, :, :, :-2], a[:, :, :, 1:-1], a[:, :, :, 2:]], axis=4)

    qb = to_classes(q).reshape(B, dil, H, nb, blk, E)
    kw = windows(to_classes(k))
    vw = windows(to_classes(v)).astype(jnp.float32)

    qq = np.arange(blk)[:, None]
    kk = np.arange(3 * blk)[None, :]
    j = kk - blk - qq
    band = np.abs(j) <= n_side
    pos = np.arange(Lp).reshape(M, dil).T
    valid = np.pad(pos < L, ((0, 0), (blk, blk))).reshape(dil, nb + 2, blk)
    kvalid = np.concatenate([valid[:, :-2], valid[:, 1:-1], valid[:, 2:]], axis=2)
    mask = band[None, None] & kvalid[:, :, None, :]
    bias = jnp.transpose(rel_bias_g[t5_bucket(j * dil)], (2, 0, 1)).astype(jnp.float32)

    s = jnp.einsum('bdhnqe,bdhnke->bdhnqk', qb, kw, preferred_element_type=jnp.float32) / math.sqrt(E)
    s = jnp.where(mask[None, :, None], s + bias[None, None, :, None], NEG_INF)
    m = jnp.max(s, axis=-1, keepdims=True)
    p = jnp.exp(s - m)
    den = jnp.sum(p, axis=-1, keepdims=True)
    o = jnp.einsum('bdhnqk,bdhnke->bdhnqe', p, vw) / den
    lse = (m + jnp.log(den))[..., 0]
    o = o.reshape(B, dil, H, M, E).transpose(0, 3, 1, 2, 4).reshape(B, Lp, H, E)[:, :L]
    lse = lse.reshape(B, dil, H, M).transpose(0, 3, 1, 2).reshape(B, Lp, H)[:, :L]
    return o, lse


def dilated_attention(qkv, rel_bias):
    B, L, _ = qkv.shape
    q, k, v = [a.reshape(B, L, N_GROUPS, HEADS_PER_GROUP, HEAD_DIM) for a in jnp.split(qkv, 3, axis=-1)]
    outs, lses = [], []
    for g, (window, dil) in enumerate(ATTN_GROUPS):
        n_side = (window // 2) // dil
        o, lse = dilated_group_attn(q[:, :, g], k[:, :, g], v[:, :, g],
                                    rel_bias[:, g * HEADS_PER_GROUP:(g + 1) * HEADS_PER_GROUP], dil, n_side)
        outs.append(o)
        lses.append(lse)
    wts = jax.nn.softmax(jnp.stack(lses, axis=2), axis=2)
    o = jnp.sum(wts[..., None] * jnp.stack(outs, axis=2), axis=2)
    return o.reshape(B, L, ATTN_OUT).astype(qkv.dtype)


def encoder_trunk(x, c, p):
    L = x.shape[1]
    cs = jax.nn.silu(c)
    for l in range(DEPTH):
        mod = cs @ p['ada_w'][l] + p['ada_b'][l]
        sh1, sc1, g1, sh2, sc2, g2 = jnp.split(mod[:, None, :], N_MOD, axis=-1)
        h = rms_norm(x, p['norm1_g'][l]) * (1 + sc1) + sh1
        u = h @ p['w_in'][l]
        u_hy, u_at, u_gate = jnp.split(u, [3 * HYENA_WIDTH, 3 * HYENA_WIDTH + 3 * ATTN_WIDTH], axis=-1)
        K = hyena_filters(L, p['filt_w1'][l], p['filt_b1'][l], p['filt_freq1'][l],
                          p['filt_w2'][l], p['filt_b2'][l], p['filt_freq2'][l], p['filt_w3'][l])
        y_hy = hyena_mix(u_hy, p['hy_conv_w'][l], p['hy_conv_b'][l], K, p['hy_bias'][l])
        y_at = dilated_attention(u_at, p['rel_bias'])
        g_hy, g_at = jnp.split(jax.nn.sigmoid(u_gate + p['b_gate'][l]), 2, axis=-1)
        merged = g_hy * (y_hy @ p['w_br_hy'][l]) + g_at * (y_at @ p['w_br_attn'][l])
        x = x + g1 * (merged @ p['w_out'][l])
        h = rms_norm(x, p['norm2_g'][l]) * (1 + sc2) + sh2
        a, gt = jnp.split(h @ p['ffn_up'][l], 2, axis=-1)
        gt = dwconv3(gt, p['ffn_conv_w'][l], p['ffn_conv_b'][l])
        x = x + g2 * ((jax.nn.silu(gt) * a) @ p['ffn_down'][l])
    return rms_norm(x, p['final_g'])


def setup_inputs(seed: int = 0) -> dict:
    key = jax.random.key(seed)
    ks = jax.random.split(key, 32)
    f32 = jnp.float32

    def nrm(k, shape, scale):
        return scale * jax.random.normal(k, shape, f32)

    D, HW = D_MODEL, HYENA_WIDTH
    return {
        'x_prompt': nrm(ks[0], (BATCH, SEQ, D), 1.0),
        'x_sample': nrm(ks[1], (DEC_BATCH, DEC_SEQ, D), 1.0),
        'c_prompt': nrm(ks[2], (BATCH, D), 1.0),
        'c_sample': nrm(ks[3], (DEC_BATCH, D), 1.0),
        'ada_w': nrm(ks[4], (DEPTH, D, N_MOD * D), 0.5 * D ** -0.5),
        'ada_b': nrm(ks[5], (DEPTH, N_MOD * D), 0.02),
        'norm1_g': 1.0 + nrm(ks[6], (DEPTH, D), 0.01),
        'w_in': nrm(ks[7], (DEPTH, D, IN_COLS), D ** -0.5),
        'b_gate': nrm(ks[8], (DEPTH, 2 * D), 0.01),
        'hy_conv_w': nrm(ks[9], (DEPTH, 3, 3 * HW), 3 ** -0.5),
        'hy_conv_b': nrm(ks[10], (DEPTH, 3 * HW), 0.01),
        'filt_w1': nrm(ks[11], (DEPTH, FILTER_FEAT, FILTER_HIDDEN), FILTER_FEAT ** -0.5),
        'filt_b1': nrm(ks[12], (DEPTH, FILTER_HIDDEN), 0.01),
        'filt_freq1': 1.0 + nrm(ks[13], (DEPTH, FILTER_HIDDEN), 0.01),
        'filt_w2': nrm(ks[14], (DEPTH, FILTER_HIDDEN, FILTER_HIDDEN), FILTER_HIDDEN ** -0.5),
        'filt_b2': nrm(ks[15], (DEPTH, FILTER_HIDDEN), 0.01),
        'filt_freq2': 1.0 + nrm(ks[16], (DEPTH, FILTER_HIDDEN), 0.01),
        'filt_w3': nrm(ks[17], (DEPTH, FILTER_HIDDEN, 2 * HYENA_ORDER * HW), FILTER_HIDDEN ** -0.5),
        'hy_bias': nrm(ks[18], (DEPTH, HYENA_ORDER, HW), 0.5),
        'rel_bias': nrm(ks[19], (REL_BUCKETS, N_ATTN_HEADS), 0.5),
        'w_br_hy': nrm(ks[20], (DEPTH, HW, D), HW ** -0.5),
        'w_br_attn': nrm(ks[21], (DEPTH, ATTN_OUT, D), ATTN_OUT ** -0.5),
        'w_out': nrm(ks[22], (DEPTH, D, D), D ** -0.5),
        'norm2_g': 1.0 + nrm(ks[23], (DEPTH, D), 0.01),
        'ffn_up': nrm(ks[24], (DEPTH, D, 2 * D_FF), D ** -0.5),
        'ffn_conv_w': nrm(ks[25], (DEPTH, 3, D_FF), 3 ** -0.5),
        'ffn_conv_b': nrm(ks[26], (DEPTH, D_FF), 0.01),
        'ffn_down': nrm(ks[27], (DEPTH, D_FF, D), D_FF ** -0.5),
        'final_g': 1.0 + nrm(ks[28], (D,), 0.01),
    }


def reference(x_prompt, x_sample, c_prompt, c_sample, ada_w, ada_b, norm1_g, w_in, b_gate,
              hy_conv_w, hy_conv_b, filt_w1, filt_b1, filt_freq1, filt_w2, filt_b2, filt_freq2,
              filt_w3, hy_bias, rel_bias, w_br_hy, w_br_attn, w_out, norm2_g, ffn_up,
              ffn_conv_w, ffn_conv_b, ffn_down, final_g):
    params = {
        'ada_w': ada_w, 'ada_b': ada_b, 'norm1_g': norm1_g, 'w_in': w_in, 'b_gate': b_gate,
        'hy_conv_w': hy_conv_w, 'hy_conv_b': hy_conv_b, 'filt_w1': filt_w1, 'filt_b1': filt_b1,
        'filt_freq1': filt_freq1, 'filt_w2': filt_w2, 'filt_b2': filt_b2, 'filt_freq2': filt_freq2,
        'filt_w3': filt_w3, 'hy_bias': hy_bias, 'rel_bias': rel_bias, 'w_br_hy': w_br_hy,
        'w_br_attn': w_br_attn, 'w_out': w_out, 'norm2_g': norm2_g, 'ffn_up': ffn_up,
        'ffn_conv_w': ffn_conv_w, 'ffn_conv_b': ffn_conv_b, 'ffn_down': ffn_down, 'final_g': final_g,
    }
    y_prompt = encoder_trunk(x_prompt, c_prompt, params)
    y_sample = encoder_trunk(x_sample, c_sample, params)
    return (y_prompt, y_sample)
```

```python
import functools
import math

import jax
import jax.numpy as jnp
import numpy as np
from jax import lax
from jax.experimental import pallas as pl
from jax.experimental.pallas import tpu as pltpu

F32 = jnp.float32
BF16 = jnp.bfloat16
HIGHEST = lax.Precision.HIGHEST

NORM_EPS = 1e-6
N_MOD = 6
HYENA_ORDER = 2
N_BANDS = 16
FAST_DECAY_PCT = 0.3
SLOW_DECAY_PCT = 1.5
DECAY_TARGET = 1e-2
ATTN_GROUPS = ((128, 1), (512, 4), (2048, 16))
HEADS_PER_GROUP = 8
HEAD_DIM = 64
REL_BUCKETS = 32
REL_MAX_DIST = 1024
NEG_INF = -1e30

LANES = 128
SUBLANES = 8
VMEM_LIMIT = 56 * 1024 * 1024

FFT_N2 = 128
ATTN_QB = 128


def _cparams(sem):
    return pltpu.CompilerParams(dimension_semantics=sem, vmem_limit_bytes=VMEM_LIMIT)


def _pick_tile(n, cap):
    best = None
    for t in range(LANES, min(n, cap) + 1, LANES):
        if n % t == 0:
            best = t
    assert best is not None, (n, cap)
    return best


def _ada_kernel(c_ref, w_ref, b_ref, o_ref):
    c = c_ref[...]
    cs = c * (1.0 / (1.0 + jnp.exp(-c)))
    o_ref[0] = jnp.dot(cs, w_ref[0], preferred_element_type=F32, precision=HIGHEST) + b_ref[0]


def ada_modulation(c_pad, ada_w, ada_b):
    depth, d, n = ada_w.shape
    nb = c_pad.shape[0]
    tn = _pick_tile(n, 1024)
    return pl.pallas_call(
        _ada_kernel,
        grid=(depth, n // tn),
        in_specs=[
            pl.BlockSpec((nb, d), lambda l, j: (0, 0)),
            pl.BlockSpec((1, d, tn), lambda l, j: (l, 0, j)),
            pl.BlockSpec((1, 1, tn), lambda l, j: (l, 0, j)),
        ],
        out_specs=pl.BlockSpec((1, nb, tn), lambda l, j: (l, 0, j)),
        out_shape=jax.ShapeDtypeStruct((depth, nb, n), F32),
        compiler_params=_cparams(("arbitrary", "arbitrary")),
        name="ada_modulation",
    )(c_pad, ada_w, ada_b.reshape(depth, 1, n))


def _normmod_rows(x, g, sc, sh):
    ms = jnp.mean(x * x, axis=-1, keepdims=True)
    y = x * lax.rsqrt(ms + NORM_EPS) * g
    return y * (1.0 + sc) + sh


def _normmod_matmul_kernel(x_ref, g_ref, sc_ref, sh_ref, w_ref, o_ref, h_ref):
    @pl.when(pl.program_id(1) == 0)
    def _():
        h_ref[...] = _normmod_rows(x_ref[...], g_ref[...], sc_ref[0], sh_ref[0]).astype(BF16)

    o_ref[...] = jnp.dot(h_ref[...], w_ref[...], preferred_element_type=F32).astype(o_ref.dtype)


def normmod_matmul(x, g, mod3, sc_idx, sh_idx, w, seq_len, tm=512, tn_cap=3072):
    t, d = x.shape
    n = w.shape[1]
    tn = _pick_tile(n, tn_cap)
    tiles_per_seq = seq_len // tm
    return pl.pallas_call(
        _normmod_matmul_kernel,
        grid=(t // tm, n // tn),
        in_specs=[
            pl.BlockSpec((tm, d), lambda i, j: (i, 0)),
            pl.BlockSpec((1, d), lambda i, j: (0, 0)),
            pl.BlockSpec((1, 1, d), lambda i, j: (i // tiles_per_seq, 0, sc_idx)),
            pl.BlockSpec((1, 1, d), lambda i, j: (i // tiles_per_seq, 0, sh_idx)),
            pl.BlockSpec((d, tn), lambda i, j: (0, j)),
        ],
        out_specs=pl.BlockSpec((tm, tn), lambda i, j: (i, j)),
        out_shape=jax.ShapeDtypeStruct((t, n), BF16),
        scratch_shapes=[pltpu.VMEM((tm, d), BF16)],
        compiler_params=_cparams(("arbitrary", "arbitrary")),
        name="normmod_matmul",
    )(x, g.reshape(1, d), mod3, mod3, w)


def _fft_dims(seq_len):
    n_fft = 2 * seq_len
    n1 = n_fft // FFT_N2
    assert n1 * FFT_N2 == n_fft and n1 % 2 == 0
    nk1 = n1 // 2 + 1
    pad = -(-nk1 // SUBLANES) * SUBLANES
    return n_fft, n1, nk1, pad


@functools.lru_cache(maxsize=None)
def _fft_constants(seq_len):
    n_fft, n1, nk1, pad = _fft_dims(seq_len)
    n2 = FFT_N2
    k1 = np.arange(nk1)[None, :, None]
    i1 = np.arange(n1)[None, None, :]
    i2 = np.arange(n2)[:, None, None]
    ph = 2.0 * np.pi * ((k1 * (i1 * n2 + i2)) % n_fft) / n_fft
    f1 = np.zeros((n2, 2 * pad, n1), np.float64)
    f1[:, :nk1] = np.cos(ph)
    f1[:, pad:pad + nk1] = -np.sin(ph)
    ck = np.full((nk1,), 2.0)
    ck[0] = 1.0
    ck[-1] = 1.0
    g1 = np.zeros((n2, n1 // 2, 2 * pad), np.float64)
    pht = np.transpose(ph[:, :, :n1 // 2], (0, 2, 1))
    g1[:, :, :nk1] = np.cos(pht) * ck / n_fft
    g1[:, :, pad:pad + nk1] = -np.sin(pht) * ck / n_fft
    a = np.arange(n2)
    ph2 = 2.0 * np.pi * ((a[:, None] * a[None, :]) % n2) / n2
    c2, s2 = np.cos(ph2), np.sin(ph2)
    f2 = np.block([[c2, s2], [-s2, c2]])
    g2 = np.block([[c2, -s2], [s2, c2]])
    return (f1.astype(np.float32), f2.astype(np.float32), g2.astype(np.float32), g1.astype(np.float32))


def _fft_stage1(src_ref, f1_ref, a_ref, n_rows, pad):
    def body(i2, carry):
        xs = src_ref[pl.ds(i2, n_rows, stride=FFT_N2), :]
        a = jnp.dot(f1_ref[i2], xs, preferred_element_type=F32, precision=HIGHEST)
        a_ref[pl.ds(pl.multiple_of(i2 * 2 * pad, SUBLANES), 2 * pad), :] = a
        return carry
    lax.fori_loop(0, FFT_N2, body, 0)


def _fft_stage2_load(a_ref, k1, pad):
    ar = a_ref[pl.ds(k1, FFT_N2, stride=2 * pad), :]
    ai = a_ref[pl.ds(pad + k1, FFT_N2, stride=2 * pad), :]
    return jnp.concatenate([ar, ai], axis=0)


def _filter_kernel(feat_ref, featr_ref, w1_ref, b1_ref, fr1_ref, w2_ref, b2_ref, fr2_ref,
                   w3f_ref, w3b_ref, delta_ref, f1_ref, f2_ref,
                   o_ref, hid_ref, k2_ref, a_ref, *, seq_len):
    n_fft, n1, nk1, pad = _fft_dims(seq_len)
    L = seq_len

    @pl.when((pl.program_id(1) == 0) & (pl.program_id(2) == 0))
    def _():
        for d, fref in enumerate((feat_ref, featr_ref)):
            h = jnp.dot(fref[...], w1_ref[0], preferred_element_type=F32, precision=HIGHEST) + b1_ref[0]
            h = jnp.sin(fr1_ref[0] * h)
            h = jnp.dot(h, w2_ref[0], preferred_element_type=F32, precision=HIGHEST) + b2_ref[0]
            hid_ref[d] = jnp.sin(fr2_ref[0] * h)

    delta = delta_ref[...]
    row = lax.broadcasted_iota(jnp.int32, (L, 1), 0)
    hf = jnp.dot(hid_ref[0], w3f_ref[0], preferred_element_type=F32, precision=HIGHEST)
    hf = hf * jnp.exp(-feat_ref[:, 0:1] * delta)
    k2_ref[pl.ds(0, L), :] = hf
    hb = jnp.dot(hid_ref[1], w3b_ref[0], preferred_element_type=F32, precision=HIGHEST)
    hb = jnp.where(row == 0, 0.0, hb * jnp.exp(-featr_ref[:, 0:1] * delta))
    k2_ref[pl.ds(L, L), :] = hb
    k2 = k2_ref[...]
    k2_ref[...] = k2 * (1.0 / jnp.sum(jnp.abs(k2), axis=0, keepdims=True))
    _fft_stage1(k2_ref, f1_ref, a_ref, n1, pad)

    def body(k1, carry):
        a = _fft_stage2_load(a_ref, k1, pad)
        o_ref[0, 0, k1] = jnp.dot(f2_ref[...], a, preferred_element_type=F32, precision=HIGHEST)
        return carry
    lax.fori_loop(0, nk1, body, 0)


def hyena_filter_spectra(seq_len, filt_w1, filt_b1, filt_freq1, filt_w2, filt_b2, filt_freq2, filt_w3, ct):
    depth, n_feat, hid = filt_w1.shape
    hw = filt_w3.shape[2] // (2 * HYENA_ORDER)
    n_fft, n1, nk1, pad = _fft_dims(seq_len)
    L = seq_len
    t = np.linspace(0.0, 1.0, L, dtype=np.float32).astype(np.float64)[:, None]
    pos = np.arange(L, dtype=np.float64)[:, None]
    bands = np.linspace(1e-4, N_BANDS - 1, N_BANDS, dtype=np.float32).astype(np.float64)[None, :]
    ang = (2.0 * math.pi / L) * pos * bands
    feat = np.concatenate([t, np.cos(ang), -np.sin(ang)], axis=-1)
    assert feat.shape[1] == n_feat
    feat_rev = np.concatenate([np.zeros((1, n_feat)), feat[:0:-1]], axis=0)
    n_feat_pad = -(-n_feat // LANES) * LANES
    feat = np.pad(feat, ((0, 0), (0, n_feat_pad - n_feat)))
    feat_rev = np.pad(feat_rev, ((0, 0), (0, n_feat_pad - n_feat)))
    filt_w1 = jnp.pad(filt_w1, ((0, 0), (0, n_feat_pad - n_feat), (0, 0)))
    n_feat = n_feat_pad
    max_decay = math.log(DECAY_TARGET) / FAST_DECAY_PCT
    min_decay = math.log(DECAY_TARGET) / SLOW_DECAY_PCT
    deltas = np.abs(np.linspace(min_decay, max_decay, hw, dtype=np.float32))[None, :]
    f1, f2, _, _ = _fft_constants(seq_len)
    nct = hw // ct

    def w3_spec(direction):
        return pl.BlockSpec((1, hid, ct), lambda l, c, o: (l, 0, (direction * HYENA_ORDER + o) * nct + c))

    def const(shape):
        return pl.BlockSpec(shape, lambda l, c, o: (0,) * len(shape), pipeline_mode=pl.Buffered(1))

    vec = lambda a: a.reshape(depth, 1, hid)
    vspec = pl.BlockSpec((1, 1, hid), lambda l, c, o: (l, 0, 0))
    return pl.pallas_call(
        functools.partial(_filter_kernel, seq_len=seq_len),
        grid=(depth, nct, HYENA_ORDER),
        in_specs=[
            const((L, n_feat)), const((L, n_feat)),
            pl.BlockSpec((1, n_feat, hid), lambda l, c, o: (l, 0, 0)), vspec, vspec,
            pl.BlockSpec((1, hid, hid), lambda l, c, o: (l, 0, 0)), vspec, vspec,
            w3_spec(0), w3_spec(1),
            pl.BlockSpec((1, ct), lambda l, c, o: (0, c)),
            const((FFT_N2, 2 * pad, n1)), const((2 * FFT_N2, 2 * FFT_N2)),
        ],
        out_specs=pl.BlockSpec((1, 1, nk1, 2 * FFT_N2, ct), lambda l, c, o: (l, o, 0, 0, c)),
        out_shape=jax.ShapeDtypeStruct((depth, HYENA_ORDER, nk1, 2 * FFT_N2, hw), F32),
        scratch_shapes=[
            pltpu.VMEM((2, L, hid), F32),
            pltpu.VMEM((n_fft, ct), F32),
            pltpu.VMEM((FFT_N2 * 2 * pad, ct), F32),
        ],
        compiler_params=_cparams(("arbitrary", "arbitrary", "arbitrary")),
        name="hyena_filter_spectra",
    )(jnp.asarray(feat, F32), jnp.asarray(feat_rev, F32), filt_w1, vec(filt_b1), vec(filt_freq1),
      filt_w2, vec(filt_b2), vec(filt_freq2), filt_w3, filt_w3,
      jnp.asarray(deltas, F32), jnp.asarray(f1), jnp.asarray(f2))


def _dwconv3_rows(x, w_ref, b_ref, prev_row=None, next_row=None):
    n = x.shape[0]
    row = lax.broadcasted_iota(jnp.int32, (n, 1), 0)
    xm = pltpu.roll(x, 1, axis=0)
    xp = pltpu.roll(x, n - 1, axis=0)
    xm = jnp.where(row == 0, 0.0 if prev_row is None else prev_row, xm)
    xp = jnp.where(row == n - 1, 0.0 if next_row is None else next_row, xp)
    return xm * w_ref[0:1, :] + x * w_ref[1:2, :] + xp * w_ref[2:3, :] + b_ref[...]


def _hyena_kernel(uv_ref, ux1_ref, ux2_ref, wv_ref, wx1_ref, wx2_ref, bv_ref, bx1_ref, bx2_ref,
                  hb_ref, kh_ref, f1_ref, f2_ref, g2_ref, g1_ref, o_ref,
                  z_ref, a_ref, b_ref, y_ref, *, seq_len):
    n_fft, n1, nk1, pad = _fft_dims(seq_len)
    L = seq_len
    half = n1 // 2
    ct = z_ref.shape[1]

    zero_rows = jnp.zeros((2 * FFT_N2, ct), F32)
    for k1 in range(nk1, pad):
        b_ref[pl.ds(k1 * 2 * FFT_N2, 2 * FFT_N2), :] = zero_rows

    z_ref[...] = _dwconv3_rows(uv_ref[0].astype(F32), wv_ref, bv_ref)
    for o, (ux_ref, wx_ref, bx_ref) in enumerate(((ux1_ref, wx1_ref, bx1_ref), (ux2_ref, wx2_ref, bx2_ref))):
        _fft_stage1(z_ref, f1_ref, a_ref, half, pad)

        def body2(k1, carry):
            a = _fft_stage2_load(a_ref, k1, pad)
            x = jnp.dot(f2_ref[...], a, preferred_element_type=F32, precision=HIGHEST)
            kk = kh_ref[0, o, k1]
            xr, xi = x[:FFT_N2], x[FFT_N2:]
            kr, ki = kk[:FFT_N2], kk[FFT_N2:]
            y = jnp.concatenate([xr * kr - xi * ki, xr * ki + xi * kr], axis=0)
            bk = jnp.dot(g2_ref[...], y, preferred_element_type=F32, precision=HIGHEST)
            b_ref[pl.ds(pl.multiple_of(k1 * 2 * FFT_N2, SUBLANES), 2 * FFT_N2), :] = bk
            return carry
        lax.fori_loop(0, nk1, body2, 0)

        def body3(i2, carry):
            br = b_ref[pl.ds(i2, pad, stride=2 * FFT_N2), :]
            bi = b_ref[pl.ds(FFT_N2 + i2, pad, stride=2 * FFT_N2), :]
            bb = jnp.concatenate([br, bi], axis=0)
            yv = jnp.dot(g1_ref[i2], bb, preferred_element_type=F32, precision=HIGHEST)
            y_ref[pl.ds(pl.multiple_of(i2 * half, SUBLANES), half), :] = yv
            return carry
        lax.fori_loop(0, FFT_N2, body3, 0)

        xg = _dwconv3_rows(ux_ref[0].astype(F32), wx_ref, bx_ref)
        bias = hb_ref[o:o + 1, :]
        for i1 in range(half):
            rows = pl.ds(i1 * FFT_N2, FFT_N2)
            y = y_ref[pl.ds(i1, FFT_N2, stride=half), :]
            z_ref[rows, :] = xg[i1 * FFT_N2:(i1 + 1) * FFT_N2] * (y + bias * z_ref[rows, :])
    o_ref[0] = z_ref[...].astype(o_ref.dtype)


def hyena_mix(u3, col0, conv_w, conv_b, khat_l, hy_bias, ct):
    nb, L, _ = u3.shape
    hw = hy_bias.shape[1]
    n_fft, n1, nk1, pad = _fft_dims(L)
    half = n1 // 2
    f1, f2, g2, g1 = _fft_constants(L)
    f1 = f1[:, :, :half]
    nct = hw // ct
    cb0 = col0 // ct
    assert col0 % ct == 0

    def u_spec(part):
        return pl.BlockSpec((1, L, ct), lambda c, b: (b, 0, cb0 + part * nct + c))

    def w_spec(part):
        return pl.BlockSpec((3, ct), lambda c, b: (0, part * nct + c))

    def b_spec(part):
        return pl.BlockSpec((1, ct), lambda c, b: (0, part * nct + c))

    const = lambda shape: pl.BlockSpec(shape, lambda c, b: (0,) * len(shape), pipeline_mode=pl.Buffered(1))
    return pl.pallas_call(
        functools.partial(_hyena_kernel, seq_len=L),
        grid=(nct, nb),
        in_specs=[
            u_spec(0), u_spec(1), u_spec(2), w_spec(0), w_spec(1), w_spec(2),
            b_spec(0), b_spec(1), b_spec(2),
            pl.BlockSpec((HYENA_ORDER, ct), lambda c, b: (0, c)),
            pl.BlockSpec((1, HYENA_ORDER, nk1, 2 * FFT_N2, ct), lambda c, b: (0, 0, 0, 0, c),
                         pipeline_mode=pl.Buffered(1)),
            const((FFT_N2, 2 * pad, half)), const((2 * FFT_N2, 2 * FFT_N2)),
            const((2 * FFT_N2, 2 * FFT_N2)), const((FFT_N2, half, 2 * pad)),
        ],
        out_specs=pl.BlockSpec((1, L, ct), lambda c, b: (b, 0, c)),
        out_shape=jax.ShapeDtypeStruct((nb, L, hw), BF16),
        scratch_shapes=[
            pltpu.VMEM((L, ct), F32),
            pltpu.VMEM((FFT_N2 * 2 * pad, ct), F32),
            pltpu.VMEM((pad * 2 * FFT_N2, ct), F32),
            pltpu.VMEM((FFT_N2 * half, ct), F32),
        ],
        compiler_params=_cparams(("arbitrary", "arbitrary")),
        name="hyena_mix",
    )(u3, u3, u3, conv_w, conv_w, conv_w, conv_b, conv_b, conv_b, hy_bias, khat_l,
      jnp.asarray(f1), jnp.asarray(f2), jnp.asarray(g2), jnp.asarray(g1))


def _t5_bucket(rel):
    nb = REL_BUCKETS // 2
    ret = (rel > 0).astype(np.int32) * nb
    n = np.abs(rel)
    max_exact = nb // 2
    large = max_exact + (np.log(np.maximum(n, 1) / max_exact) / np.log(REL_MAX_DIST / max_exact)
                         * (nb - max_exact)).astype(np.int32)
    large = np.minimum(large, nb - 1)
    return ret + np.where(n < max_exact, n, large)


def _bias_kernel(rb_ref, bkt_ref, o_ref):
    n_groups = bkt_ref.shape[0]
    for g in range(n_groups):
        bkt = bkt_ref[g]
        for h in range(HEADS_PER_GROUP):
            tile = jnp.zeros(bkt.shape, F32)
            for b in range(REL_BUCKETS):
                tile = jnp.where(bkt == b, rb_ref[b, g * HEADS_PER_GROUP + h], tile)
            o_ref[g, h] = tile


def attention_bias_tiles(rel_bias, n_side_list):
    n_groups = len(ATTN_GROUPS)
    qq = np.arange(ATTN_QB)[:, None]
    kk = np.arange(2 * ATTN_QB)[None, :]
    bkts = []
    for (window, dil), n_side in zip(ATTN_GROUPS, n_side_list):
        j = kk - n_side - qq
        bkts.append(_t5_bucket(j * dil))
    bkt = np.stack(bkts).astype(np.int32)
    return pl.pallas_call(
        _bias_kernel,
        in_specs=[pl.BlockSpec(memory_space=pltpu.SMEM), pl.BlockSpec(memory_space=pltpu.VMEM)],
        out_specs=pl.BlockSpec(memory_space=pltpu.VMEM),
        out_shape=jax.ShapeDtypeStruct((n_groups, HEADS_PER_GROUP, ATTN_QB, 2 * ATTN_QB), F32),
        name="attention_bias_tiles",
    )(rel_bias, jnp.asarray(bkt))


def _attn_kernel(q_ref, k_ref, v_ref, bias_ref, o_ref, l_ref, ks_ref, vs_ref, *, m_len, n_side, tq):
    qt = pl.program_id(2)
    width = HEADS_PER_GROUP * HEAD_DIM
    halo = n_side

    @pl.when(qt == 0)
    def _():
        zeros = jnp.zeros((halo, width), BF16)
        for ref, src in ((ks_ref, k_ref), (vs_ref, v_ref)):
            ref[pl.ds(0, halo), :] = zeros
            ref[pl.ds(halo + m_len, 2 * ATTN_QB - halo), :] = jnp.zeros((2 * ATTN_QB - halo, width), BF16)
            ref[pl.ds(halo, m_len), :] = src[0]

    qq = lax.broadcasted_iota(jnp.int32, (ATTN_QB, 2 * ATTN_QB), 0)
    kk = lax.broadcasted_iota(jnp.int32, (ATTN_QB, 2 * ATTN_QB), 1)
    rel = kk - halo - qq
    band = (rel >= -n_side) & (rel <= n_side)
    lane = lax.broadcasted_iota(jnp.int32, (ATTN_QB, 2 * HEAD_DIM), 1)
    low = lane < HEAD_DIM
    scale = 1.0 / math.sqrt(HEAD_DIM)

    def block(ib, carry):
        r0 = pl.multiple_of(ib * ATTN_QB, ATTN_QB)
        q0 = pl.multiple_of(qt * tq + r0, ATTN_QB)
        kpos = q0 + kk - halo
        mask = band & (kpos >= 0) & (kpos < m_len)
        for hp in range(HEADS_PER_GROUP // 2):
            cols = pl.ds(hp * 2 * HEAD_DIM, 2 * HEAD_DIM)
            qp = q_ref[0, pl.ds(r0, ATTN_QB), cols]
            kp = ks_ref[pl.ds(q0, 2 * ATTN_QB), cols]
            vp = vs_ref[pl.ds(q0, 2 * ATTN_QB), cols]
            outs, lses = [], []
            for hh in range(2):
                sel = low if hh == 0 else jnp.logical_not(low)
                qm = jnp.where(sel, qp, jnp.zeros_like(qp))
                s = lax.dot_general(qm, kp, (((1,), (1,)), ((), ())), preferred_element_type=F32) * scale
                s = jnp.where(mask, s + bias_ref[0, 2 * hp + hh], NEG_INF)
                mx = jnp.max(s, axis=-1, keepdims=True)
                p = jnp.exp(s - mx)
                den = jnp.sum(p, axis=-1, keepdims=True)
                pv = jnp.dot(p.astype(BF16), vp, preferred_element_type=F32)
                outs.append(pv / den)
                lses.append(mx + jnp.log(den))
            o_ref[0, pl.ds(r0, ATTN_QB), cols] = jnp.where(low, outs[0], outs[1])
            l_ref[0, pl.ds(r0, ATTN_QB), cols] = jnp.where(low, lses[0], lses[1])
        return carry
    lax.fori_loop(0, tq // ATTN_QB, block, 0)


def dilated_group_attention(u3, col0, g, dil, n_side, bias_tiles):
    nb, L, ncol = u3.shape
    width = HEADS_PER_GROUP * HEAD_DIM
    n_groups = len(ATTN_GROUPS)
    attn_w = n_groups * width
    assert L % (dil * ATTN_QB) == 0 and n_side <= ATTN_QB and n_side % 16 == 0
    assert ncol % width == 0 and col0 % width == 0
    m_len = L // dil
    tq = min(512, m_len)
    uv = u3.reshape(nb, m_len, dil * ncol)
    cpr = ncol // width
    cq = col0 // width + g
    ck = cq + attn_w // width
    cv = ck + attn_w // width
    kernel = functools.partial(_attn_kernel, m_len=m_len, n_side=n_side, tq=tq)
    o, lse = pl.pallas_call(
        kernel,
        grid=(nb, dil, m_len // tq),
        in_specs=[
            pl.BlockSpec((1, tq, width), lambda b, r, t: (b, t, r * cpr + cq)),
            pl.BlockSpec((1, m_len, width), lambda b, r, t: (b, 0, r * cpr + ck)),
            pl.BlockSpec((1, m_len, width), lambda b, r, t: (b, 0, r * cpr + cv)),
            pl.BlockSpec((1, HEADS_PER_GROUP, ATTN_QB, 2 * ATTN_QB), lambda b, r, t: (g, 0, 0, 0)),
        ],
        out_specs=[
            pl.BlockSpec((1, tq, width), lambda b, r, t: (b, t, r)),
            pl.BlockSpec((1, tq, width), lambda b, r, t: (b, t, r)),
        ],
        out_shape=[
            jax.ShapeDtypeStruct((nb, m_len, dil * width), F32),
            jax.ShapeDtypeStruct((nb, m_len, dil * width), F32),
        ],
        scratch_shapes=[
            pltpu.VMEM((m_len + 2 * ATTN_QB, width), BF16),
            pltpu.VMEM((m_len + 2 * ATTN_QB, width), BF16),
        ],
        compiler_params=_cparams(("arbitrary", "arbitrary", "arbitrary")),
        name=f"dilated_attention_g{g}",
    )(uv, uv, uv, bias_tiles)
    return o.reshape(nb * L, width), lse.reshape(nb * L, width)


def _merge_kernel(yh_ref, o0_ref, o1_ref, o2_ref, l0_ref, l1_ref, l2_ref, gh_ref, ga_ref,
                  bh_ref, ba_ref, wh_ref, wa_ref, out_ref, ya_ref):
    @pl.when(pl.program_id(1) == 0)
    def _():
        l0, l1, l2 = l0_ref[...], l1_ref[...], l2_ref[...]
        mx = jnp.maximum(jnp.maximum(l0, l1), l2)
        e0, e1, e2 = jnp.exp(l0 - mx), jnp.exp(l1 - mx), jnp.exp(l2 - mx)
        inv = 1.0 / (e0 + e1 + e2)
        ya = (e0 * inv) * o0_ref[...] + (e1 * inv) * o1_ref[...] + (e2 * inv) * o2_ref[...]
        ya_ref[...] = ya.astype(BF16)

    acc_h = jnp.dot(yh_ref[...], wh_ref[...], preferred_element_type=F32)
    acc_a = jnp.dot(ya_ref[...], wa_ref[...], preferred_element_type=F32)
    sig = lambda v: 1.0 / (1.0 + jnp.exp(-v))
    g_h = sig(gh_ref[...].astype(F32) + bh_ref[...])
    g_a = sig(ga_ref[...].astype(F32) + ba_ref[...])
    out_ref[...] = (g_h * acc_h + g_a * acc_a).astype(out_ref.dtype)


def branch_merge(y_hy, outs, lses, u2, gate_col0, b_gate, w_hy, w_at, tm=512):
    t, hw = y_hy.shape
    d = w_hy.shape[1]
    aw = w_at.shape[0]
    tn = _pick_tile(math.gcd(gate_col0, d), 512)
    gb = gate_col0 // tn
    nj = d // tn
    row = lambda w: pl.BlockSpec((tm, w), lambda i, j: (i, 0))
    return pl.pallas_call(
        _merge_kernel,
        grid=(t // tm, nj),
        in_specs=[
            row(hw), row(aw), row(aw), row(aw), row(aw), row(aw), row(aw),
            pl.BlockSpec((tm, tn), lambda i, j: (i, gb + j)),
            pl.BlockSpec((tm, tn), lambda i, j: (i, gb + nj + j)),
            pl.BlockSpec((1, tn), lambda i, j: (0, j)),
            pl.BlockSpec((1, tn), lambda i, j: (0, nj + j)),
            pl.BlockSpec((hw, tn), lambda i, j: (0, j)),
            pl.BlockSpec((aw, tn), lambda i, j: (0, j)),
        ],
        out_specs=pl.BlockSpec((tm, tn), lambda i, j: (i, j)),
        out_shape=jax.ShapeDtypeStruct((t, d), BF16),
        scratch_shapes=[pltpu.VMEM((tm, aw), BF16)],
        compiler_params=_cparams(("arbitrary", "arbitrary")),
        name="branch_merge",
    )(y_hy, *outs, *lses, u2, u2, b_gate, b_gate, w_hy, w_at)


def _out_proj_kernel(x_ref, m_ref, gate_ref, w_ref, o_ref):
    acc = jnp.dot(m_ref[...], w_ref[...], preferred_element_type=F32)
    o_ref[...] = x_ref[...] + gate_ref[0] * acc


def out_proj_residual(x, m, mod3, gate_idx, w, seq_len, tm=512):
    t, d = x.shape
    k = m.shape[1]
    tiles_per_seq = seq_len // tm
    return pl.pallas_call(
        _out_proj_kernel,
        grid=(t // tm,),
        in_specs=[
            pl.BlockSpec((tm, d), lambda i: (i, 0)),
            pl.BlockSpec((tm, k), lambda i: (i, 0)),
            pl.BlockSpec((1, 1, d), lambda i: (i // tiles_per_seq, 0, gate_idx)),
            pl.BlockSpec((k, d), lambda i: (0, 0)),
        ],
        out_specs=pl.BlockSpec((tm, d), lambda i: (i, 0)),
        out_shape=jax.ShapeDtypeStruct((t, d), F32),
        input_output_aliases={0: 0},
        compiler_params=_cparams(("arbitrary",)),
        name="out_proj_residual",
    )(x, m, mod3, w)


def _ffn_down_kernel(x_ref, a_ref, gt_ref, gprev_ref, gnext_ref, cw_ref, cb_ref, gate_ref, w_ref,
                     o_ref, acc_ref, *, tiles_per_seq, halo_rows):
    i = pl.program_id(0)
    kstep = pl.program_id(1)
    first = (i % tiles_per_seq) == 0
    last = (i % tiles_per_seq) == tiles_per_seq - 1
    prev_row = jnp.where(first, 0.0, gprev_ref[halo_rows - 1:halo_rows, :].astype(F32))
    next_row = jnp.where(last, 0.0, gnext_ref[0:1, :].astype(F32))
    gt = _dwconv3_rows(gt_ref[...].astype(F32), cw_ref, cb_ref, prev_row, next_row)
    act = (gt * (1.0 / (1.0 + jnp.exp(-gt))) * a_ref[...].astype(F32)).astype(BF16)
    part = jnp.dot(act, w_ref[...], preferred_element_type=F32)

    @pl.when(kstep == 0)
    def _():
        acc_ref[...] = part

    @pl.when(kstep > 0)
    def _():
        acc_ref[...] += part

    @pl.when(kstep == pl.num_programs(1) - 1)
    def _():
        o_ref[...] = x_ref[...] + gate_ref[0] * acc_ref[...]


def ffn_down_residual(x, ag, conv_w, conv_b, mod3, gate_idx, w, seq_len, tm=512):
    t, d = x.shape
    ff = w.shape[0]
    tk = _pick_tile(ff, 1536)
    nk = ff // tk
    halo_rows = 16
    tiles_per_seq = seq_len // tm
    rb = tm // halo_rows
    last_rb = t // halo_rows - 1
    kernel = functools.partial(_ffn_down_kernel, tiles_per_seq=tiles_per_seq, halo_rows=halo_rows)
    return pl.pallas_call(
        kernel,
        grid=(t // tm, nk),
        in_specs=[
            pl.BlockSpec((tm, d), lambda i, k: (i, 0)),
            pl.BlockSpec((tm, tk), lambda i, k: (i, k)),
            pl.BlockSpec((tm, tk), lambda i, k: (i, nk + k)),
            pl.BlockSpec((halo_rows, tk), lambda i, k: (jnp.maximum(i * rb - 1, 0), nk + k)),
            pl.BlockSpec((halo_rows, tk), lambda i, k: (jnp.minimum((i + 1) * rb, last_rb), nk + k)),
            pl.BlockSpec((3, tk), lambda i, k: (0, k)),
            pl.BlockSpec((1, tk), lambda i, k: (0, k)),
            pl.BlockSpec((1, 1, d), lambda i, k: (i // tiles_per_seq, 0, gate_idx)),
            pl.BlockSpec((tk, d), lambda i, k: (k, 0)),
        ],
        out_specs=pl.BlockSpec((tm, d), lambda i, k: (i, 0)),
        out_shape=jax.ShapeDtypeStruct((t, d), F32),
        scratch_shapes=[pltpu.VMEM((tm, d), F32)],
        input_output_aliases={0: 0},
        compiler_params=_cparams(("arbitrary", "arbitrary")),
        name="ffn_down_residual",
    )(x, ag, ag, ag, ag, conv_w, conv_b, mod3, w)


def _final_norm_kernel(x_ref, g_ref, o_ref):
    x = x_ref[...]
    ms = jnp.mean(x * x, axis=-1, keepdims=True)
    o_ref[...] = x * lax.rsqrt(ms + NORM_EPS) * g_ref[...]


def final_norm(x, g, row0, n_rows, tm=512):
    d = x.shape[1]
    off = row0 // tm
    return pl.pallas_call(
        _final_norm_kernel,
        grid=(n_rows // tm,),
        in_specs=[pl.BlockSpec((tm, d), lambda i: (off + i, 0)), pl.BlockSpec((1, d), lambda i: (0, 0))],
        out_specs=pl.BlockSpec((tm, d), lambda i: (i, 0)),
        out_shape=jax.ShapeDtypeStruct((n_rows, d), F32),
        compiler_params=_cparams(("arbitrary",)),
        name="final_norm",
    )(x, g.reshape(1, d))


def kernel(x_prompt, x_sample, c_prompt, c_sample, ada_w, ada_b, norm1_g, w_in, b_gate, hy_conv_w, hy_conv_b,
           filt_w1, filt_b1, filt_freq1, filt_w2, filt_b2, filt_freq2, filt_w3, hy_bias, rel_bias, w_br_hy,
           w_br_attn, w_out, norm2_g, ffn_up, ffn_conv_w, ffn_conv_b, ffn_down, final_g):
    bp, L, d = x_prompt.shape
    bs = x_sample.shape[0]
    assert x_sample.shape[1] == L
    nb = bp + bs
    depth = ada_w.shape[0]
    hw = hy_bias.shape[2]
    n_groups = len(ATTN_GROUPS)
    attn_w = n_groups * HEADS_PER_GROUP * HEAD_DIM
    hy_cols = 3 * hw
    gate_col0 = hy_cols + 3 * attn_w
    assert w_in.shape[2] == gate_col0 + 2 * d

    x = jnp.concatenate([x_prompt, x_sample], axis=0).reshape(nb * L, d)
    nb_pad = -(-nb // SUBLANES) * SUBLANES
    c_pad = jnp.zeros((nb_pad, d), F32).at[:nb].set(jnp.concatenate([c_prompt, c_sample], axis=0))
    mod = ada_modulation(c_pad, ada_w, ada_b)

    hy_ct = LANES
    khat = hyena_filter_spectra(L, filt_w1, filt_b1, filt_freq1, filt_w2, filt_b2, filt_freq2, filt_w3, hy_ct)
    n_sides = [(window // 2) // dil for window, dil in ATTN_GROUPS]
    bias_tiles = attention_bias_tiles(rel_bias, n_sides)

    for l in range(depth):
        mod3 = mod[l].reshape(nb_pad, 1, N_MOD * d)
        u = normmod_matmul(x, norm1_g[l], mod3, 1, 0, w_in[l].astype(BF16), L)
        u3 = u.reshape(nb, L, u.shape[1])
        y_hy = hyena_mix(u3, 0, hy_conv_w[l], hy_conv_b[l].reshape(1, hy_cols), khat[l:l + 1], hy_bias[l], hy_ct)
        outs, lses = [], []
        for g, (window, dil) in enumerate(ATTN_GROUPS):
            o_g, l_g = dilated_group_attention(u3, hy_cols, g, dil, n_sides[g], bias_tiles)
            outs.append(o_g)
            lses.append(l_g)
        merged = branch_merge(y_hy.reshape(nb * L, hw), outs, lses, u, gate_col0, b_gate[l].reshape(1, 2 * d),
                              w_br_hy[l].astype(BF16), w_br_attn[l].astype(BF16))
        x = out_proj_residual(x, merged, mod3, 2, w_out[l].astype(BF16), L)
        ag = normmod_matmul(x, norm2_g[l], mod3, 4, 3, ffn_up[l].astype(BF16), L)
        ff = ffn_down.shape[1]
        x = ffn_down_residual(x, ag, ffn_conv_w[l], ffn_conv_b[l].reshape(1, ff), mod3, 5,
                              ffn_down[l].astype(BF16), L)

    y_prompt = final_norm(x, final_g, 0, bp * L).reshape(bp, L, d)
    y_sample = final_norm(x, final_g, bp * L, bs * L).reshape(bs, L, d)
    return (y_prompt, y_sample)
```

```python
import functools
import math

import jax
import jax.numpy as jnp
import numpy as np
from jax import lax
from jax.experimental import pallas as pl
from jax.experimental.pallas import tpu as pltpu

F32 = jnp.float32
BF16 = jnp.bfloat16
HIGHEST = lax.Precision.HIGHEST

NORM_EPS = 1e-6
N_MOD = 6
HYENA_ORDER = 2
N_BANDS = 16
FAST_DECAY_PCT = 0.3
SLOW_DECAY_PCT = 1.5
DECAY_TARGET = 1e-2
ATTN_GROUPS = ((128, 1), (512, 4), (2048, 16))
HEADS_PER_GROUP = 8
HEAD_DIM = 64
REL_BUCKETS = 32
REL_MAX_DIST = 1024
NEG_INF = -1e30

LANES = 128
SUBLANES = 8
VMEM_LIMIT = 56 * 1024 * 1024

FFT_N2 = 128
HYENA_CT = 256
ATTN_QB = 128


def _cparams(sem):
    return pltpu.CompilerParams(dimension_semantics=sem, vmem_limit_bytes=VMEM_LIMIT)


def _pick_tile(n, cap):
    best = None
    for t in range(LANES, min(n, cap) + 1, LANES):
        if n % t == 0:
            best = t
    assert best is not None, (n, cap)
    return best


def _ada_kernel(c_ref, w_ref, b_ref, o_ref):
    c = c_ref[...]
    cs = c * (1.0 / (1.0 + jnp.exp(-c)))
    o_ref[0] = jnp.dot(cs, w_ref[0], preferred_element_type=F32, precision=HIGHEST) + b_ref[0]


def ada_modulation(c_pad, ada_w, ada_b):
    depth, d, n = ada_w.shape
    nb = c_pad.shape[0]
    tn = _pick_tile(n, 1024)
    return pl.pallas_call(
        _ada_kernel,
        grid=(depth, n // tn),
        in_specs=[
            pl.BlockSpec((nb, d), lambda l, j: (0, 0)),
            pl.BlockSpec((1, d, tn), lambda l, j: (l, 0, j)),
            pl.BlockSpec((1, 1, tn), lambda l, j: (l, 0, j)),
        ],
        out_specs=pl.BlockSpec((1, nb, tn), lambda l, j: (l, 0, j)),
        out_shape=jax.ShapeDtypeStruct((depth, nb, n), F32),
        compiler_params=_cparams(("arbitrary", "arbitrary")),
        name="ada_modulation",
    )(c_pad, ada_w, ada_b.reshape(depth, 1, n))


def _normmod_rows(x, g, sc, sh):
    ms = jnp.mean(x * x, axis=-1, keepdims=True)
    y = x * lax.rsqrt(ms + NORM_EPS) * g
    return y * (1.0 + sc) + sh


def _normmod_matmul_kernel(x_ref, g_ref, sc_ref, sh_ref, w_ref, o_ref, h_ref):
    @pl.when(pl.program_id(1) == 0)
    def _():
        h_ref[...] = _normmod_rows(x_ref[...], g_ref[...], sc_ref[0], sh_ref[0]).astype(BF16)

    o_ref[...] = jnp.dot(h_ref[...], w_ref[...], preferred_element_type=F32).astype(o_ref.dtype)


def normmod_matmul(x, g, mod3, sc_idx, sh_idx, w, seq_len, tm=512, tn_cap=3072):
    t, d = x.shape
    n = w.shape[1]
    tn = _pick_tile(n, tn_cap)
    tiles_per_seq = seq_len // tm
    return pl.pallas_call(
        _normmod_matmul_kernel,
        grid=(t // tm, n // tn),
        in_specs=[
            pl.BlockSpec((tm, d), lambda i, j: (i, 0)),
            pl.BlockSpec((1, d), lambda i, j: (0, 0)),
            pl.BlockSpec((1, 1, d), lambda i, j: (i // tiles_per_seq, 0, sc_idx)),
            pl.BlockSpec((1, 1, d), lambda i, j: (i // tiles_per_seq, 0, sh_idx)),
            pl.BlockSpec((d, tn), lambda i, j: (0, j)),
        ],
        out_specs=pl.BlockSpec((tm, tn), lambda i, j: (i, j)),
        out_shape=jax.ShapeDtypeStruct((t, n), BF16),
        scratch_shapes=[pltpu.VMEM((tm, d), BF16)],
        compiler_params=_cparams(("arbitrary", "arbitrary")),
        name="normmod_matmul",
    )(x, g.reshape(1, d), mod3, mod3, w)


def _fft_dims(seq_len):
    n_fft = 2 * seq_len
    n1 = n_fft // FFT_N2
    assert n1 * FFT_N2 == n_fft and n1 % 2 == 0
    nk1 = n1 // 2 + 1
    pad = -(-nk1 // SUBLANES) * SUBLANES
    return n_fft, n1, nk1, pad


@functools.lru_cache(maxsize=None)
def _fft_constants(seq_len):
    n_fft, n1, nk1, pad = _fft_dims(seq_len)
    n2 = FFT_N2
    k1 = np.arange(nk1)[None, :, None]
    i1 = np.arange(n1)[None, None, :]
    i2 = np.arange(n2)[:, None, None]
    ph = 2.0 * np.pi * ((k1 * (i1 * n2 + i2)) % n_fft) / n_fft
    f1 = np.zeros((n2, 2 * pad, n1), np.float64)
    f1[:, :nk1] = np.cos(ph)
    f1[:, pad:pad + nk1] = -np.sin(ph)
    ck = np.full((nk1,), 2.0)
    ck[0] = 1.0
    ck[-1] = 1.0
    g1 = np.zeros((n2, n1 // 2, 2 * pad), np.float64)
    pht = np.transpose(ph[:, :, :n1 // 2], (0, 2, 1))
    g1[:, :, :nk1] = np.cos(pht) * ck / n_fft
    g1[:, :, pad:pad + nk1] = -np.sin(pht) * ck / n_fft
    a = np.arange(n2)
    ph2 = 2.0 * np.pi * ((a[:, None] * a[None, :]) % n2) / n2
    c2, s2 = np.cos(ph2), np.sin(ph2)
    f2 = np.block([[c2, s2], [-s2, c2]])
    g2 = np.block([[c2, -s2], [s2, c2]])
    return (f1.astype(np.float32), f2.astype(np.float32), g2.astype(np.float32), g1.astype(np.float32))


def _fft_stage1(src_ref, f1_ref, a_ref, n_rows, pad):
    def body(i2, carry):
        xs = src_ref[pl.ds(i2, n_rows, stride=FFT_N2), :]
        a = jnp.dot(f1_ref[i2], xs, preferred_element_type=F32, precision=HIGHEST)
        a_ref[pl.ds(pl.multiple_of(i2 * 2 * pad, SUBLANES), 2 * pad), :] = a
        return carry
    lax.fori_loop(0, FFT_N2, body, 0)


def _fft_stage2_load(a_ref, k1, pad):
    ar = a_ref[pl.ds(k1, FFT_N2, stride=2 * pad), :]
    ai = a_ref[pl.ds(pad + k1, FFT_N2, stride=2 * pad), :]
    return jnp.concatenate([ar, ai], axis=0)


def _filter_kernel(feat_ref, featr_ref, w1_ref, b1_ref, fr1_ref, w2_ref, b2_ref, fr2_ref,
                   w3f_ref, w3b_ref, delta_ref, f1_ref, f2_ref,
                   o_ref, hid_ref, k2_ref, a_ref, *, seq_len):
    n_fft, n1, nk1, pad = _fft_dims(seq_len)
    L = seq_len

    @pl.when((pl.program_id(1) == 0) & (pl.program_id(2) == 0))
    def _():
        for d, fref in enumerate((feat_ref, featr_ref)):
            h = jnp.dot(fref[...], w1_ref[0], preferred_element_type=F32, precision=HIGHEST) + b1_ref[0]
            h = jnp.sin(fr1_ref[0] * h)
            h = jnp.dot(h, w2_ref[0], preferred_element_type=F32, precision=HIGHEST) + b2_ref[0]
            hid_ref[d] = jnp.sin(fr2_ref[0] * h)

    delta = delta_ref[...]
    row = lax.broadcasted_iota(jnp.int32, (L, 1), 0)
    hf = jnp.dot(hid_ref[0], w3f_ref[0], preferred_element_type=F32, precision=HIGHEST)
    hf = hf * jnp.exp(-feat_ref[:, 0:1] * delta)
    k2_ref[pl.ds(0, L), :] = hf
    hb = jnp.dot(hid_ref[1], w3b_ref[0], preferred_element_type=F32, precision=HIGHEST)
    hb = jnp.where(row == 0, 0.0, hb * jnp.exp(-featr_ref[:, 0:1] * delta))
    k2_ref[pl.ds(L, L), :] = hb
    k2 = k2_ref[...]
    k2_ref[...] = k2 * (1.0 / jnp.sum(jnp.abs(k2), axis=0, keepdims=True))
    _fft_stage1(k2_ref, f1_ref, a_ref, n1, pad)

    def body(k1, carry):
        a = _fft_stage2_load(a_ref, k1, pad)
        o_ref[0, 0, k1] = jnp.dot(f2_ref[...], a, preferred_element_type=F32, precision=HIGHEST)
        return carry
    lax.fori_loop(0, nk1, body, 0)


def hyena_filter_spectra(seq_len, filt_w1, filt_b1, filt_freq1, filt_w2, filt_b2, filt_freq2, filt_w3, ct):
    depth, n_feat, hid = filt_w1.shape
    hw = filt_w3.shape[2] // (2 * HYENA_ORDER)
    n_fft, n1, nk1, pad = _fft_dims(seq_len)
    L = seq_len
    t = np.linspace(0.0, 1.0, L, dtype=np.float32).astype(np.float64)[:, None]
    pos = np.arange(L, dtype=np.float64)[:, None]
    bands = np.linspace(1e-4, N_BANDS - 1, N_BANDS, dtype=np.float32).astype(np.float64)[None, :]
    ang = (2.0 * math.pi / L) * pos * bands
    feat = np.concatenate([t, np.cos(ang), -np.sin(ang)], axis=-1)
    assert feat.shape[1] == n_feat
    feat_rev = np.concatenate([np.zeros((1, n_feat)), feat[:0:-1]], axis=0)
    n_feat_pad = -(-n_feat // LANES) * LANES
    feat = np.pad(feat, ((0, 0), (0, n_feat_pad - n_feat)))
    feat_rev = np.pad(feat_rev, ((0, 0), (0, n_feat_pad - n_feat)))
    filt_w1 = jnp.pad(filt_w1, ((0, 0), (0, n_feat_pad - n_feat), (0, 0)))
    n_feat = n_feat_pad
    max_decay = math.log(DECAY_TARGET) / FAST_DECAY_PCT
    min_decay = math.log(DECAY_TARGET) / SLOW_DECAY_PCT
    deltas = np.abs(np.linspace(min_decay, max_decay, hw, dtype=np.float32))[None, :]
    f1, f2, _, _ = _fft_constants(seq_len)
    nct = hw // ct

    def w3_spec(direction):
        return pl.BlockSpec((1, hid, ct), lambda l, c, o: (l, 0, (direction * HYENA_ORDER + o) * nct + c))

    def const(shape):
        return pl.BlockSpec(shape, lambda l, c, o: (0,) * len(shape), pipeline_mode=pl.Buffered(1))

    vec = lambda a: a.reshape(depth, 1, hid)
    vspec = pl.BlockSpec((1, 1, hid), lambda l, c, o: (l, 0, 0))
    return pl.pallas_call(
        functools.partial(_filter_kernel, seq_len=seq_len),
        grid=(depth, nct, HYENA_ORDER),
        in_specs=[
            const((L, n_feat)), const((L, n_feat)),
            pl.BlockSpec((1, n_feat, hid), lambda l, c, o: (l, 0, 0)), vspec, vspec,
            pl.BlockSpec((1, hid, hid), lambda l, c, o: (l, 0, 0)), vspec, vspec,
            w3_spec(0), w3_spec(1),
            pl.BlockSpec((1, ct), lambda l, c, o: (0, c)),
            const((FFT_N2, 2 * pad, n1)), const((2 * FFT_N2, 2 * FFT_N2)),
        ],
        out_specs=pl.BlockSpec((1, 1, nk1, 2 * FFT_N2, ct), lambda l, c, o: (l, o, 0, 0, c)),
        out_shape=jax.ShapeDtypeStruct((depth, HYENA_ORDER, nk1, 2 * FFT_N2, hw), F32),
        scratch_shapes=[
            pltpu.VMEM((2, L, hid), F32),
            pltpu.VMEM((n_fft, ct), F32),
            pltpu.VMEM((FFT_N2 * 2 * pad, ct), F32),
        ],
        compiler_params=_cparams(("arbitrary", "arbitrary", "arbitrary")),
        name="hyena_filter_spectra",
    )(jnp.asarray(feat, F32), jnp.asarray(feat_rev, F32), filt_w1, vec(filt_b1), vec(filt_freq1),
      filt_w2, vec(filt_b2), vec(filt_freq2), filt_w3, filt_w3,
      jnp.asarray(deltas, F32), jnp.asarray(f1), jnp.asarray(f2))


def _dwconv3_rows(x, w_ref, b_ref, prev_row=None, next_row=None):
    n = x.shape[0]
    row = lax.broadcasted_iota(jnp.int32, (n, 1), 0)
    xm = pltpu.roll(x, 1, axis=0)
    xp = pltpu.roll(x, n - 1, axis=0)
    xm = jnp.where(row == 0, 0.0 if prev_row is None else prev_row, xm)
    xp = jnp.where(row == n - 1, 0.0 if next_row is None else next_row, xp)
    return xm * w_ref[0:1, :] + x * w_ref[1:2, :] + xp * w_ref[2:3, :] + b_ref[...]


def _ld_lanes(ref, start, n, stride):
    rows = pl.ds(start, n) if stride == 1 else pl.ds(start, n, stride=stride)
    return jnp.concatenate([ref[h, rows, :] for h in range(ref.shape[0])], axis=1)


def _st_lanes(ref, start, n, stride, val):
    rows = pl.ds(start, n) if stride == 1 else pl.ds(start, n, stride=stride)
    for h in range(ref.shape[0]):
        ref[h, rows, :] = val[:, h * LANES:(h + 1) * LANES]


def _dwconv3_to_slabs(src_ref, w_ref, b_ref, dst_ref, seq_len, chunk):
    halo = 16
    n_chunks = seq_len // chunk
    for c in range(n_chunks):
        r0 = c * chunk
        x = src_ref[0, r0:r0 + chunk, :].astype(F32)
        prev_row = src_ref[0, r0 - halo:r0, :].astype(F32)[halo - 1:halo] if c > 0 else None
        next_row = src_ref[0, r0 + chunk:r0 + chunk + halo, :].astype(F32)[0:1] if c < n_chunks - 1 else None
        y = _dwconv3_rows(x, w_ref, b_ref, prev_row, next_row)
        for h in range(dst_ref.shape[0]):
            dst_ref[h, r0:r0 + chunk, :] = y[:, h * LANES:(h + 1) * LANES]


def _hyena_conv_kernel(zin_ref, ux_ref, wz_ref, bz_ref, wx_ref, bx_ref, hb_ref, kh_ref,
                       f1_ref, f2_ref, g2_ref, g1_ref, o_ref, z_ref, xg_ref, a_ref, *, seq_len, first):
    n_fft, n1, nk1, pad = _fft_dims(seq_len)
    L = seq_len
    half = n1 // 2
    nslab = z_ref.shape[0]
    chunk = min(L, 512)

    if first:
        _dwconv3_to_slabs(zin_ref, wz_ref, bz_ref, z_ref, L, chunk)
    else:
        for h in range(nslab):
            z_ref[h] = zin_ref[0, :, h * LANES:(h + 1) * LANES]
    _dwconv3_to_slabs(ux_ref, wx_ref, bx_ref, xg_ref, L, chunk)

    def body1(i2, carry):
        xs = _ld_lanes(z_ref, i2, half, FFT_N2).astype(BF16)
        a = jnp.dot(f1_ref[i2], xs, preferred_element_type=F32)
        _st_lanes(a_ref, pl.multiple_of(i2 * 2 * pad, SUBLANES), 2 * pad, 1, a)
        return carry
    lax.fori_loop(0, FFT_N2, body1, 0, unroll=4)

    def body2(k1, carry):
        ar = _ld_lanes(a_ref, k1, FFT_N2, 2 * pad)
        ai = _ld_lanes(a_ref, pad + k1, FFT_N2, 2 * pad)
        x = jnp.dot(f2_ref[...], jnp.concatenate([ar, ai], axis=0).astype(BF16), preferred_element_type=F32)
        kk = kh_ref[0, 0, k1]
        xr, xi = x[:FFT_N2], x[FFT_N2:]
        kr, ki = kk[:FFT_N2], kk[FFT_N2:]
        y = jnp.concatenate([xr * kr - xi * ki, xr * ki + xi * kr], axis=0)
        bk = jnp.dot(g2_ref[...], y.astype(BF16), preferred_element_type=F32)
        _st_lanes(a_ref, k1, FFT_N2, 2 * pad, bk[:FFT_N2])
        _st_lanes(a_ref, pad + k1, FFT_N2, 2 * pad, bk[FFT_N2:])
        return carry
    lax.fori_loop(0, nk1, body2, 0, unroll=3 if nk1 % 3 == 0 else 1)

    bias = hb_ref[...]

    def body3(i2, carry):
        bb = _ld_lanes(a_ref, pl.multiple_of(i2 * 2 * pad, SUBLANES), 2 * pad, 1).astype(BF16)
        y = jnp.dot(g1_ref[i2], bb, preferred_element_type=F32)
        z_old = _ld_lanes(z_ref, i2, half, FFT_N2)
        xg = _ld_lanes(xg_ref, i2, half, FFT_N2)
        _st_lanes(z_ref, i2, half, FFT_N2, xg * (y + bias * z_old))
        return carry
    lax.fori_loop(0, FFT_N2, body3, 0, unroll=4)

    for h in range(nslab):
        o_ref[0, :, h * LANES:(h + 1) * LANES] = z_ref[h].astype(o_ref.dtype)


def hyena_step(zin, zin_col0, u3, x_col0, w_cols, conv_w, conv_b, khat, layer, order, hy_bias_row, first, ct):
    nb, L, _ = u3.shape
    hw = hy_bias_row.shape[1]
    n_fft, n1, nk1, pad = _fft_dims(L)
    half = n1 // 2
    f1, f2, g2, g1 = _fft_constants(L)
    f1 = f1[:, :, :half]
    nct = hw // ct
    nslab = ct // LANES
    assert zin_col0 % ct == 0 and x_col0 % ct == 0 and w_cols[0] % ct == 0 and w_cols[1] % ct == 0
    last = order == HYENA_ORDER - 1

    def col_spec(rows, col0):
        return pl.BlockSpec((rows, ct), lambda c, b: (0, col0 // ct + c))

    const = lambda shape: pl.BlockSpec(shape, lambda c, b: (0,) * len(shape), pipeline_mode=pl.Buffered(1))
    return pl.pallas_call(
        functools.partial(_hyena_conv_kernel, seq_len=L, first=first),
        grid=(nct, nb),
        in_specs=[
            pl.BlockSpec((1, L, ct), lambda c, b: (b, 0, zin_col0 // ct + c)),
            pl.BlockSpec((1, L, ct), lambda c, b: (b, 0, x_col0 // ct + c)),
            col_spec(3, w_cols[0]), col_spec(1, w_cols[0]), col_spec(3, w_cols[1]), col_spec(1, w_cols[1]),
            pl.BlockSpec((1, ct), lambda c, b: (0, c)),
            pl.BlockSpec((1, 1, nk1, 2 * FFT_N2, ct), lambda c, b: (layer, order, 0, 0, c),
                         pipeline_mode=pl.Buffered(1)),
            const((FFT_N2, 2 * pad, half)), const((2 * FFT_N2, 2 * FFT_N2)),
            const((2 * FFT_N2, 2 * FFT_N2)), const((FFT_N2, half, 2 * pad)),
        ],
        out_specs=pl.BlockSpec((1, L, ct), lambda c, b: (b, 0, c)),
        out_shape=jax.ShapeDtypeStruct((nb, L, hw), BF16 if last else F32),
        scratch_shapes=[
            pltpu.VMEM((nslab, L, LANES), F32),
            pltpu.VMEM((nslab, L, LANES), F32),
            pltpu.VMEM((nslab, FFT_N2 * 2 * pad, LANES), F32),
        ],
        compiler_params=_cparams(("arbitrary", "arbitrary")),
        name=f"hyena_step{order}",
    )(zin, u3, conv_w, conv_b, conv_w, conv_b, hy_bias_row, khat,
      jnp.asarray(f1, BF16), jnp.asarray(f2, BF16), jnp.asarray(g2, BF16), jnp.asarray(g1, BF16))


def hyena_mix(u3, col0, conv_w, conv_b, khat, layer, hy_bias, ct):
    hw = hy_bias.shape[1]
    z = hyena_step(u3, col0, u3, col0 + hw, (0, hw), conv_w, conv_b, khat, layer, 0, hy_bias[0:1], True, ct)
    return hyena_step(z, 0, u3, col0 + 2 * hw, (0, 2 * hw), conv_w, conv_b, khat, layer, 1, hy_bias[1:2], False, ct)


def _t5_bucket(rel):
    nb = REL_BUCKETS // 2
    ret = (rel > 0).astype(np.int32) * nb
    n = np.abs(rel)
    max_exact = nb // 2
    large = max_exact + (np.log(np.maximum(n, 1) / max_exact) / np.log(REL_MAX_DIST / max_exact)
                         * (nb - max_exact)).astype(np.int32)
    large = np.minimum(large, nb - 1)
    return ret + np.where(n < max_exact, n, large)


def _bias_kernel(rb_ref, bkt_ref, o_ref):
    n_groups = bkt_ref.shape[0]
    for g in range(n_groups):
        bkt = bkt_ref[g]
        for h in range(HEADS_PER_GROUP):
            tile = jnp.zeros(bkt.shape, F32)
            for b in range(REL_BUCKETS):
                tile = jnp.where(bkt == b, rb_ref[b, g * HEADS_PER_GROUP + h], tile)
            o_ref[g, h] = tile


def attention_bias_tiles(rel_bias, n_side_list):
    n_groups = len(ATTN_GROUPS)
    qq = np.arange(ATTN_QB)[:, None]
    kk = np.arange(2 * ATTN_QB)[None, :]
    bkts = []
    for (window, dil), n_side in zip(ATTN_GROUPS, n_side_list):
        j = kk - n_side - qq
        bkts.append(_t5_bucket(j * dil))
    bkt = np.stack(bkts).astype(np.int32)
    return pl.pallas_call(
        _bias_kernel,
        in_specs=[pl.BlockSpec(memory_space=pltpu.SMEM), pl.BlockSpec(memory_space=pltpu.VMEM)],
        out_specs=pl.BlockSpec(memory_space=pltpu.VMEM),
        out_shape=jax.ShapeDtypeStruct((n_groups, HEADS_PER_GROUP, ATTN_QB, 2 * ATTN_QB), F32),
        name="attention_bias_tiles",
    )(rel_bias, jnp.asarray(bkt))


def _attn_kernel(q_ref, k_ref, v_ref, bias_ref, o_ref, l_ref, ks_ref, vs_ref, *, m_len, n_side, tq):
    qt = pl.program_id(2)
    width = HEADS_PER_GROUP * HEAD_DIM
    halo = n_side

    @pl.when(qt == 0)
    def _():
        zeros = jnp.zeros((halo, width), BF16)
        for ref, src in ((ks_ref, k_ref), (vs_ref, v_ref)):
            ref[pl.ds(0, halo), :] = zeros
            ref[pl.ds(halo + m_len, 2 * ATTN_QB - halo), :] = jnp.zeros((2 * ATTN_QB - halo, width), BF16)
            ref[pl.ds(halo, m_len), :] = src[0]

    qq = lax.broadcasted_iota(jnp.int32, (ATTN_QB, 2 * ATTN_QB), 0)
    kk = lax.broadcasted_iota(jnp.int32, (ATTN_QB, 2 * ATTN_QB), 1)
    rel = kk - halo - qq
    band = (rel >= -n_side) & (rel <= n_side)
    lane = lax.broadcasted_iota(jnp.int32, (ATTN_QB, 2 * HEAD_DIM), 1)
    low = lane < HEAD_DIM
    scale = 1.0 / math.sqrt(HEAD_DIM)

    def block(ib, carry):
        r0 = pl.multiple_of(ib * ATTN_QB, ATTN_QB)
        q0 = pl.multiple_of(qt * tq + r0, ATTN_QB)
        kpos = q0 + kk - halo
        mask = band & (kpos >= 0) & (kpos < m_len)
        for hp in range(HEADS_PER_GROUP // 2):
            cols = pl.ds(hp * 2 * HEAD_DIM, 2 * HEAD_DIM)
            qp = q_ref[0, pl.ds(r0, ATTN_QB), cols]
            kp = ks_ref[pl.ds(q0, 2 * ATTN_QB), cols]
            vp = vs_ref[pl.ds(q0, 2 * ATTN_QB), cols]
            outs, lses = [], []
            for hh in range(2):
                sel = low if hh == 0 else jnp.logical_not(low)
                qm = jnp.where(sel, qp, jnp.zeros_like(qp))
                s = lax.dot_general(qm, kp, (((1,), (1,)), ((), ())), preferred_element_type=F32) * scale
                s = jnp.where(mask, s + bias_ref[0, 2 * hp + hh], NEG_INF)
                mx = jnp.max(s, axis=-1, keepdims=True)
                p = jnp.exp(s - mx)
                den = jnp.sum(p, axis=-1, keepdims=True)
                pv = jnp.dot(p.astype(BF16), vp, preferred_element_type=F32)
                outs.append(pv / den)
                lses.append(mx + jnp.log(den))
            o_ref[0, pl.ds(r0, ATTN_QB), cols] = jnp.where(low, outs[0], outs[1])
            l_ref[0, pl.ds(r0, ATTN_QB), cols] = jnp.where(low, lses[0], lses[1])
        return carry
    lax.fori_loop(0, tq // ATTN_QB, block, 0)


def dilated_group_attention(u3, col0, g, dil, n_side, bias_tiles):
    nb, L, ncol = u3.shape
    width = HEADS_PER_GROUP * HEAD_DIM
    n_groups = len(ATTN_GROUPS)
    attn_w = n_groups * width
    assert L % (dil * ATTN_QB) == 0 and n_side <= ATTN_QB and n_side % 16 == 0
    assert ncol % width == 0 and col0 % width == 0
    m_len = L // dil
    tq = min(512, m_len)
    uv = u3.reshape(nb, m_len, dil * ncol)
    cpr = ncol // width
    cq = col0 // width + g
    ck = cq + attn_w // width
    cv = ck + attn_w // width
    kernel = functools.partial(_attn_kernel, m_len=m_len, n_side=n_side, tq=tq)
    o, lse = pl.pallas_call(
        kernel,
        grid=(nb, dil, m_len // tq),
        in_specs=[
            pl.BlockSpec((1, tq, width), lambda b, r, t: (b, t, r * cpr + cq)),
            pl.BlockSpec((1, m_len, width), lambda b, r, t: (b, 0, r * cpr + ck)),
            pl.BlockSpec((1, m_len, width), lambda b, r, t: (b, 0, r * cpr + cv)),
            pl.BlockSpec((1, HEADS_PER_GROUP, ATTN_QB, 2 * ATTN_QB), lambda b, r, t: (g, 0, 0, 0)),
        ],
        out_specs=[
            pl.BlockSpec((1, tq, width), lambda b, r, t: (b, t, r)),
            pl.BlockSpec((1, tq, width), lambda b, r, t: (b, t, r)),
        ],
        out_shape=[
            jax.ShapeDtypeStruct((nb, m_len, dil * width), F32),
            jax.ShapeDtypeStruct((nb, m_len, dil * width), F32),
        ],
        scratch_shapes=[
            pltpu.VMEM((m_len + 2 * ATTN_QB, width), BF16),
            pltpu.VMEM((m_len + 2 * ATTN_QB, width), BF16),
        ],
        compiler_params=_cparams(("arbitrary", "arbitrary", "arbitrary")),
        name=f"dilated_attention_g{g}",
    )(uv, uv, uv, bias_tiles)
    return o.reshape(nb * L, width), lse.reshape(nb * L, width)


def _merge_kernel(yh_ref, o0_ref, o1_ref, o2_ref, l0_ref, l1_ref, l2_ref, gh_ref, ga_ref,
                  bh_ref, ba_ref, wh_ref, wa_ref, out_ref, ya_ref):
    @pl.when(pl.program_id(1) == 0)
    def _():
        l0, l1, l2 = l0_ref[...], l1_ref[...], l2_ref[...]
        mx = jnp.maximum(jnp.maximum(l0, l1), l2)
        e0, e1, e2 = jnp.exp(l0 - mx), jnp.exp(l1 - mx), jnp.exp(l2 - mx)
        inv = 1.0 / (e0 + e1 + e2)
        ya = (e0 * inv) * o0_ref[...] + (e1 * inv) * o1_ref[...] + (e2 * inv) * o2_ref[...]
        ya_ref[...] = ya.astype(BF16)

    acc_h = jnp.dot(yh_ref[...], wh_ref[...], preferred_element_type=F32)
    acc_a = jnp.dot(ya_ref[...], wa_ref[...], preferred_element_type=F32)
    sig = lambda v: 1.0 / (1.0 + jnp.exp(-v))
    g_h = sig(gh_ref[...].astype(F32) + bh_ref[...])
    g_a = sig(ga_ref[...].astype(F32) + ba_ref[...])
    out_ref[...] = (g_h * acc_h + g_a * acc_a).astype(out_ref.dtype)


def branch_merge(y_hy, outs, lses, u2, gate_col0, b_gate, w_hy, w_at, tm=512):
    t, hw = y_hy.shape
    d = w_hy.shape[1]
    aw = w_at.shape[0]
    tn = _pick_tile(math.gcd(gate_col0, d), 512)
    gb = gate_col0 // tn
    nj = d // tn
    row = lambda w: pl.BlockSpec((tm, w), lambda i, j: (i, 0))
    return pl.pallas_call(
        _merge_kernel,
        grid=(t // tm, nj),
        in_specs=[
            row(hw), row(aw), row(aw), row(aw), row(aw), row(aw), row(aw),
            pl.BlockSpec((tm, tn), lambda i, j: (i, gb + j)),
            pl.BlockSpec((tm, tn), lambda i, j: (i, gb + nj + j)),
            pl.BlockSpec((1, tn), lambda i, j: (0, j)),
            pl.BlockSpec((1, tn), lambda i, j: (0, nj + j)),
            pl.BlockSpec((hw, tn), lambda i, j: (0, j)),
            pl.BlockSpec((aw, tn), lambda i, j: (0, j)),
        ],
        out_specs=pl.BlockSpec((tm, tn), lambda i, j: (i, j)),
        out_shape=jax.ShapeDtypeStruct((t, d), BF16),
        scratch_shapes=[pltpu.VMEM((tm, aw), BF16)],
        compiler_params=_cparams(("arbitrary", "arbitrary")),
        name="branch_merge",
    )(y_hy, *outs, *lses, u2, u2, b_gate, b_gate, w_hy, w_at)


def _out_proj_kernel(x_ref, m_ref, gate_ref, w_ref, o_ref):
    acc = jnp.dot(m_ref[...], w_ref[...], preferred_element_type=F32)
    o_ref[...] = x_ref[...] + gate_ref[0] * acc


def out_proj_residual(x, m, mod3, gate_idx, w, seq_len, tm=512):
    t, d = x.shape
    k = m.shape[1]
    tiles_per_seq = seq_len // tm
    return pl.pallas_call(
        _out_proj_kernel,
        grid=(t // tm,),
        in_specs=[
            pl.BlockSpec((tm, d), lambda i: (i, 0)),
            pl.BlockSpec((tm, k), lambda i: (i, 0)),
            pl.BlockSpec((1, 1, d), lambda i: (i // tiles_per_seq, 0, gate_idx)),
            pl.BlockSpec((k, d), lambda i: (0, 0)),
        ],
        out_specs=pl.BlockSpec((tm, d), lambda i: (i, 0)),
        out_shape=jax.ShapeDtypeStruct((t, d), F32),
        input_output_aliases={0: 0},
        compiler_params=_cparams(("arbitrary",)),
        name="out_proj_residual",
    )(x, m, mod3, w)


def _ffn_down_kernel(x_ref, a_ref, gt_ref, gprev_ref, gnext_ref, cw_ref, cb_ref, gate_ref, w_ref,
                     o_ref, acc_ref, *, tiles_per_seq, halo_rows):
    i = pl.program_id(0)
    kstep = pl.program_id(1)
    first = (i % tiles_per_seq) == 0
    last = (i % tiles_per_seq) == tiles_per_seq - 1
    prev_row = jnp.where(first, 0.0, gprev_ref[halo_rows - 1:halo_rows, :].astype(F32))
    next_row = jnp.where(last, 0.0, gnext_ref[0:1, :].astype(F32))
    gt = _dwconv3_rows(gt_ref[...].astype(F32), cw_ref, cb_ref, prev_row, next_row)
    act = (gt * (1.0 / (1.0 + jnp.exp(-gt))) * a_ref[...].astype(F32)).astype(BF16)
    part = jnp.dot(act, w_ref[...], preferred_element_type=F32)

    @pl.when(kstep == 0)
    def _():
        acc_ref[...] = part

    @pl.when(kstep > 0)
    def _():
        acc_ref[...] += part

    @pl.when(kstep == pl.num_programs(1) - 1)
    def _():
        o_ref[...] = x_ref[...] + gate_ref[0] * acc_ref[...]


def ffn_down_residual(x, ag, conv_w, conv_b, mod3, gate_idx, w, seq_len, tm=512):
    t, d = x.shape
    ff = w.shape[0]
    tk = _pick_tile(ff, 1536)
    nk = ff // tk
    halo_rows = 16
    tiles_per_seq = seq_len // tm
    rb = tm // halo_rows
    last_rb = t // halo_rows - 1
    kernel = functools.partial(_ffn_down_kernel, tiles_per_seq=tiles_per_seq, halo_rows=halo_rows)
    return pl.pallas_call(
        kernel,
        grid=(t // tm, nk),
        in_specs=[
            pl.BlockSpec((tm, d), lambda i, k: (i, 0)),
            pl.BlockSpec((tm, tk), lambda i, k: (i, k)),
            pl.BlockSpec((tm, tk), lambda i, k: (i, nk + k)),
            pl.BlockSpec((halo_rows, tk), lambda i, k: (jnp.maximum(i * rb - 1, 0), nk + k)),
            pl.BlockSpec((halo_rows, tk), lambda i, k: (jnp.minimum((i + 1) * rb, last_rb), nk + k)),
            pl.BlockSpec((3, tk), lambda i, k: (0, k)),
            pl.BlockSpec((1, tk), lambda i, k: (0, k)),
            pl.BlockSpec((1, 1, d), lambda i, k: (i // tiles_per_seq, 0, gate_idx)),
            pl.BlockSpec((tk, d), lambda i, k: (k, 0)),
        ],
        out_specs=pl.BlockSpec((tm, d), lambda i, k: (i, 0)),
        out_shape=jax.ShapeDtypeStruct((t, d), F32),
        scratch_shapes=[pltpu.VMEM((tm, d), F32)],
        input_output_aliases={0: 0},
        compiler_params=_cparams(("arbitrary", "arbitrary")),
        name="ffn_down_residual",
    )(x, ag, ag, ag, ag, conv_w, conv_b, mod3, w)


def _final_norm_kernel(x_ref, g_ref, o_ref):
    x = x_ref[...]
    ms = jnp.mean(x * x, axis=-1, keepdims=True)
    o_ref[...] = x * lax.rsqrt(ms + NORM_EPS) * g_ref[...]


def final_norm(x, g, row0, n_rows, tm=512):
    d = x.shape[1]
    off = row0 // tm
    return pl.pallas_call(
        _final_norm_kernel,
        grid=(n_rows // tm,),
        in_specs=[pl.BlockSpec((tm, d), lambda i: (off + i, 0)), pl.BlockSpec((1, d), lambda i: (0, 0))],
        out_specs=pl.BlockSpec((tm, d), lambda i: (i, 0)),
        out_shape=jax.ShapeDtypeStruct((n_rows, d), F32),
        compiler_params=_cparams(("arbitrary",)),
        name="final_norm",
    )(x, g.reshape(1, d))


def kernel(x_prompt, x_sample, c_prompt, c_sample, ada_w, ada_b, norm1_g, w_in, b_gate, hy_conv_w, hy_conv_b,
           filt_w1, filt_b1, filt_freq1, filt_w2, filt_b2, filt_freq2, filt_w3, hy_bias, rel_bias, w_br_hy,
           w_br_attn, w_out, norm2_g, ffn_up, ffn_conv_w, ffn_conv_b, ffn_down, final_g):
    bp, L, d = x_prompt.shape
    bs = x_sample.shape[0]
    assert x_sample.shape[1] == L
    nb = bp + bs
    depth = ada_w.shape[0]
    hw = hy_bias.shape[2]
    n_groups = len(ATTN_GROUPS)
    attn_w = n_groups * HEADS_PER_GROUP * HEAD_DIM
    hy_cols = 3 * hw
    gate_col0 = hy_cols + 3 * attn_w
    assert w_in.shape[2] == gate_col0 + 2 * d

    x = jnp.concatenate([x_prompt, x_sample], axis=0).reshape(nb * L, d)
    nb_pad = -(-nb // SUBLANES) * SUBLANES
    c_pad = jnp.zeros((nb_pad, d), F32).at[:nb].set(jnp.concatenate([c_prompt, c_sample], axis=0))
    mod = ada_modulation(c_pad, ada_w, ada_b)

    hy_ct = min(HYENA_CT, hw)
    khat = hyena_filter_spectra(L, filt_w1, filt_b1, filt_freq1, filt_w2, filt_b2, filt_freq2, filt_w3, LANES)
    n_sides = [(window // 2) // dil for window, dil in ATTN_GROUPS]
    bias_tiles = attention_bias_tiles(rel_bias, n_sides)

    for l in range(depth):
        mod3 = mod[l].reshape(nb_pad, 1, N_MOD * d)
        u = normmod_matmul(x, norm1_g[l], mod3, 1, 0, w_in[l].astype(BF16), L)
        u3 = u.reshape(nb, L, u.shape[1])
        y_hy = hyena_mix(u3, 0, hy_conv_w[l], hy_conv_b[l].reshape(1, hy_cols), khat, l, hy_bias[l], hy_ct)
        outs, lses = [], []
        for g, (window, dil) in enumerate(ATTN_GROUPS):
            o_g, l_g = dilated_group_attention(u3, hy_cols, g, dil, n_sides[g], bias_tiles)
            outs.append(o_g)
            lses.append(l_g)
        merged = branch_merge(y_hy.reshape(nb * L, hw), outs, lses, u, gate_col0, b_gate[l].reshape(1, 2 * d),
                              w_br_hy[l].astype(BF16), w_br_attn[l].astype(BF16))
        x = out_proj_residual(x, merged, mod3, 2, w_out[l].astype(BF16), L)
        ag = normmod_matmul(x, norm2_g[l], mod3, 4, 3, ffn_up[l].astype(BF16), L)
        ff = ffn_down.shape[1]
        x = ffn_down_residual(x, ag, ffn_conv_w[l], ffn_conv_b[l].reshape(1, ff), mod3, 5,
                              ffn_down[l].astype(BF16), L)

    y_prompt = final_norm(x, final_g, 0, bp * L).reshape(bp, L, d)
    y_sample = final_norm(x, final_g, bp * L, bs * L).reshape(bs, L, d)
    return (y_prompt, y_sample)
```

```python
import functools
import math

import jax
import jax.numpy as jnp
import numpy as np
from jax import lax
from jax.experimental import pallas as pl
from jax.experimental.pallas import tpu as pltpu

F32 = jnp.float32
BF16 = jnp.bfloat16
HIGHEST = lax.Precision.HIGHEST

NORM_EPS = 1e-6
N_MOD = 6
HYENA_ORDER = 2
N_BANDS = 16
FAST_DECAY_PCT = 0.3
SLOW_DECAY_PCT = 1.5
DECAY_TARGET = 1e-2
ATTN_GROUPS = ((128, 1), (512, 4), (2048, 16))
HEADS_PER_GROUP = 8
HEAD_DIM = 64
REL_BUCKETS = 32
REL_MAX_DIST = 1024
NEG_INF = -1e30

LANES = 128
SUBLANES = 8
VMEM_LIMIT = 56 * 1024 * 1024

FFT_N2 = 128
HYENA_CT = 256
ATTN_QB = 128


def _cparams(sem):
    return pltpu.CompilerParams(dimension_semantics=sem, vmem_limit_bytes=VMEM_LIMIT)


def _pick_tile(n, cap):
    best = None
    for t in range(LANES, min(n, cap) + 1, LANES):
        if n % t == 0:
            best = t
    assert best is not None, (n, cap)
    return best


def _ada_kernel(c_ref, w_ref, b_ref, o_ref):
    c = c_ref[...]
    cs = c * (1.0 / (1.0 + jnp.exp(-c)))
    o_ref[0] = jnp.dot(cs, w_ref[0], preferred_element_type=F32, precision=HIGHEST) + b_ref[0]


def ada_modulation(c_pad, ada_w, ada_b):
    depth, d, n = ada_w.shape
    nb = c_pad.shape[0]
    tn = _pick_tile(n, 1024)
    return pl.pallas_call(
        _ada_kernel,
        grid=(depth, n // tn),
        in_specs=[
            pl.BlockSpec((nb, d), lambda l, j: (0, 0)),
            pl.BlockSpec((1, d, tn), lambda l, j: (l, 0, j)),
            pl.BlockSpec((1, 1, tn), lambda l, j: (l, 0, j)),
        ],
        out_specs=pl.BlockSpec((1, nb, tn), lambda l, j: (l, 0, j)),
        out_shape=jax.ShapeDtypeStruct((depth, nb, n), F32),
        compiler_params=_cparams(("arbitrary", "arbitrary")),
        name="ada_modulation",
    )(c_pad, ada_w, ada_b.reshape(depth, 1, n))


def _normmod_rows(x, g, sc, sh):
    ms = jnp.mean(x * x, axis=-1, keepdims=True)
    y = x * lax.rsqrt(ms + NORM_EPS) * g
    return y * (1.0 + sc) + sh


def _normmod_matmul_kernel(x_ref, g_ref, sc_ref, sh_ref, w_ref, o_ref, h_ref):
    @pl.when(pl.program_id(1) == 0)
    def _():
        h_ref[...] = _normmod_rows(x_ref[...], g_ref[...], sc_ref[0], sh_ref[0]).astype(BF16)

    o_ref[...] = jnp.dot(h_ref[...], w_ref[...], preferred_element_type=F32).astype(o_ref.dtype)


def normmod_matmul(x, g, mod3, sc_idx, sh_idx, w, seq_len, tm=512, tn_cap=3072):
    t, d = x.shape
    n = w.shape[1]
    tn = _pick_tile(n, tn_cap)
    tiles_per_seq = seq_len // tm
    return pl.pallas_call(
        _normmod_matmul_kernel,
        grid=(t // tm, n // tn),
        in_specs=[
            pl.BlockSpec((tm, d), lambda i, j: (i, 0)),
            pl.BlockSpec((1, d), lambda i, j: (0, 0)),
            pl.BlockSpec((1, 1, d), lambda i, j: (i // tiles_per_seq, 0, sc_idx)),
            pl.BlockSpec((1, 1, d), lambda i, j: (i // tiles_per_seq, 0, sh_idx)),
            pl.BlockSpec((d, tn), lambda i, j: (0, j)),
        ],
        out_specs=pl.BlockSpec((tm, tn), lambda i, j: (i, j)),
        out_shape=jax.ShapeDtypeStruct((t, n), BF16),
        scratch_shapes=[pltpu.VMEM((tm, d), BF16)],
        compiler_params=_cparams(("arbitrary", "arbitrary")),
        name="normmod_matmul",
    )(x, g.reshape(1, d), mod3, mod3, w)


def _in_proj_kernel(x_ref, g_ref, sc_ref, sh_ref, wm_ref, wg_ref, om_ref, *rest, n_main, dils, chunk):
    og_refs = rest[:len(dils)]
    h_ref, perm_ref = rest[len(dils):]
    j = pl.program_id(1)
    tm = h_ref.shape[0]

    @pl.when(j == 0)
    def _():
        h_ref[...] = _normmod_rows(x_ref[...], g_ref[...], sc_ref[0], sh_ref[0]).astype(BF16)

    @pl.when(j < n_main)
    def _():
        om_ref[...] = jnp.dot(h_ref[...], wm_ref[...], preferred_element_type=F32).astype(om_ref.dtype)

    for gi, (o_ref, d) in enumerate(zip(og_refs, dils)):
        @pl.when(j == n_main + gi)
        def _():
            rows = tm // d
            for c0 in range(0, wg_ref.shape[1], chunk):
                res = jnp.dot(h_ref[...], wg_ref[:, c0:c0 + chunk], preferred_element_type=F32)
                if d == 1:
                    o_ref[0, 0, :, c0:c0 + chunk] = res.astype(o_ref.dtype)
                    continue
                for s in range(chunk // LANES):
                    perm_ref[s] = res[:, s * LANES:(s + 1) * LANES]
                for r in range(d):
                    for s in range(chunk // LANES):
                        cols = slice(c0 + s * LANES, c0 + (s + 1) * LANES)
                        o_ref[0, r, :, cols] = perm_ref[s, pl.ds(r, rows, stride=d), :].astype(o_ref.dtype)


def in_projection(x, g, mod3, sc_idx, sh_idx, w_main, w_groups, dils, nb, seq_len, tm=512, tn_cap=1792):
    t, d_model = x.shape
    n_main = w_main.shape[1]
    gw = w_groups.shape[1] // len(dils)
    tn = _pick_tile(n_main, tn_cap)
    nj_main = n_main // tn
    tps = seq_len // tm
    chunk = gw // 3
    assert chunk % LANES == 0 and all(tm % (dd * 16) == 0 for dd in dils)
    kernel = functools.partial(_in_proj_kernel, n_main=nj_main, dils=tuple(dils), chunk=chunk)
    outs = pl.pallas_call(
        kernel,
        grid=(t // tm, nj_main + len(dils)),
        in_specs=[
            pl.BlockSpec((tm, d_model), lambda i, j: (i, 0)),
            pl.BlockSpec((1, d_model), lambda i, j: (0, 0)),
            pl.BlockSpec((1, 1, d_model), lambda i, j: (i // tps, 0, sc_idx)),
            pl.BlockSpec((1, 1, d_model), lambda i, j: (i // tps, 0, sh_idx)),
            pl.BlockSpec((d_model, tn), lambda i, j: (0, jnp.minimum(j, nj_main - 1))),
            pl.BlockSpec((d_model, gw), lambda i, j: (0, jnp.clip(j - nj_main, 0, len(dils) - 1))),
        ],
        out_specs=[pl.BlockSpec((tm, tn), lambda i, j: (i, jnp.minimum(j, nj_main - 1)))] + [
            pl.BlockSpec((1, dd, tm // dd, gw), lambda i, j: (i // tps, 0, i % tps, 0)) for dd in dils],
        out_shape=[jax.ShapeDtypeStruct((t, n_main), BF16)] + [
            jax.ShapeDtypeStruct((nb, dd, seq_len // dd, gw), BF16) for dd in dils],
        scratch_shapes=[pltpu.VMEM((tm, d_model), BF16), pltpu.VMEM((chunk // LANES, tm, LANES), F32)],
        compiler_params=_cparams(("arbitrary", "arbitrary")),
        name="in_projection",
    )(x, g.reshape(1, d_model), mod3, mod3, w_main, w_groups)
    return outs[0], list(outs[1:])


def _fft_dims(seq_len):
    n_fft = 2 * seq_len
    n1 = n_fft // FFT_N2
    assert n1 * FFT_N2 == n_fft and n1 % 2 == 0
    nk1 = n1 // 2 + 1
    pad = -(-nk1 // SUBLANES) * SUBLANES
    return n_fft, n1, nk1, pad


@functools.lru_cache(maxsize=None)
def _fft_constants(seq_len):
    n_fft, n1, nk1, pad = _fft_dims(seq_len)
    n2 = FFT_N2
    k1 = np.arange(nk1)[None, :, None]
    i1 = np.arange(n1)[None, None, :]
    i2 = np.arange(n2)[:, None, None]
    ph = 2.0 * np.pi * ((k1 * (i1 * n2 + i2)) % n_fft) / n_fft
    f1 = np.zeros((n2, 2 * pad, n1), np.float64)
    f1[:, :nk1] = np.cos(ph)
    f1[:, pad:pad + nk1] = -np.sin(ph)
    ck = np.full((nk1,), 2.0)
    ck[0] = 1.0
    ck[-1] = 1.0
    g1 = np.zeros((n2, n1 // 2, 2 * pad), np.float64)
    pht = np.transpose(ph[:, :, :n1 // 2], (0, 2, 1))
    g1[:, :, :nk1] = np.cos(pht) * ck / n_fft
    g1[:, :, pad:pad + nk1] = -np.sin(pht) * ck / n_fft
    a = np.arange(n2)
    ph2 = 2.0 * np.pi * ((a[:, None] * a[None, :]) % n2) / n2
    c2, s2 = np.cos(ph2), np.sin(ph2)
    f2 = np.block([[c2, s2], [-s2, c2]])
    g2 = np.block([[c2, -s2], [s2, c2]])
    return (f1.astype(np.float32), f2.astype(np.float32), g2.astype(np.float32), g1.astype(np.float32))


def _fft_stage1(src_ref, f1_ref, a_ref, n_rows, pad):
    def body(i2, carry):
        xs = src_ref[pl.ds(i2, n_rows, stride=FFT_N2), :]
        a = jnp.dot(f1_ref[i2], xs, preferred_element_type=F32, precision=HIGHEST)
        a_ref[pl.ds(pl.multiple_of(i2 * 2 * pad, SUBLANES), 2 * pad), :] = a
        return carry
    lax.fori_loop(0, FFT_N2, body, 0)


def _fft_stage2_load(a_ref, k1, pad):
    ar = a_ref[pl.ds(k1, FFT_N2, stride=2 * pad), :]
    ai = a_ref[pl.ds(pad + k1, FFT_N2, stride=2 * pad), :]
    return jnp.concatenate([ar, ai], axis=0)


def _filter_kernel(feat_ref, featr_ref, w1_ref, b1_ref, fr1_ref, w2_ref, b2_ref, fr2_ref,
                   w3f_ref, w3b_ref, delta_ref, f1_ref, f2_ref,
                   o_ref, hid_ref, k2_ref, a_ref, *, seq_len):
    n_fft, n1, nk1, pad = _fft_dims(seq_len)
    L = seq_len

    @pl.when((pl.program_id(1) == 0) & (pl.program_id(2) == 0))
    def _():
        for d, fref in enumerate((feat_ref, featr_ref)):
            h = jnp.dot(fref[...], w1_ref[0], preferred_element_type=F32, precision=HIGHEST) + b1_ref[0]
            h = jnp.sin(fr1_ref[0] * h)
            h = jnp.dot(h, w2_ref[0], preferred_element_type=F32, precision=HIGHEST) + b2_ref[0]
            hid_ref[d] = jnp.sin(fr2_ref[0] * h)

    delta = delta_ref[...]
    row = lax.broadcasted_iota(jnp.int32, (L, 1), 0)
    hf = jnp.dot(hid_ref[0], w3f_ref[0], preferred_element_type=F32, precision=HIGHEST)
    hf = hf * jnp.exp(-feat_ref[:, 0:1] * delta)
    k2_ref[pl.ds(0, L), :] = hf
    hb = jnp.dot(hid_ref[1], w3b_ref[0], preferred_element_type=F32, precision=HIGHEST)
    hb = jnp.where(row == 0, 0.0, hb * jnp.exp(-featr_ref[:, 0:1] * delta))
    k2_ref[pl.ds(L, L), :] = hb
    k2 = k2_ref[...]
    k2_ref[...] = k2 * (1.0 / jnp.sum(jnp.abs(k2), axis=0, keepdims=True))
    _fft_stage1(k2_ref, f1_ref, a_ref, n1, pad)

    def body(k1, carry):
        a = _fft_stage2_load(a_ref, k1, pad)
        o_ref[0, 0, k1] = jnp.dot(f2_ref[...], a, preferred_element_type=F32, precision=HIGHEST)
        return carry
    lax.fori_loop(0, nk1, body, 0)


def hyena_filter_spectra(seq_len, filt_w1, filt_b1, filt_freq1, filt_w2, filt_b2, filt_freq2, filt_w3, ct):
    depth, n_feat, hid = filt_w1.shape
    hw = filt_w3.shape[2] // (2 * HYENA_ORDER)
    n_fft, n1, nk1, pad = _fft_dims(seq_len)
    L = seq_len
    t = np.linspace(0.0, 1.0, L, dtype=np.float32).astype(np.float64)[:, None]
    pos = np.arange(L, dtype=np.float64)[:, None]
    bands = np.linspace(1e-4, N_BANDS - 1, N_BANDS, dtype=np.float32).astype(np.float64)[None, :]
    ang = (2.0 * math.pi / L) * pos * bands
    feat = np.concatenate([t, np.cos(ang), -np.sin(ang)], axis=-1)
    assert feat.shape[1] == n_feat
    feat_rev = np.concatenate([np.zeros((1, n_feat)), feat[:0:-1]], axis=0)
    n_feat_pad = -(-n_feat // LANES) * LANES
    feat = np.pad(feat, ((0, 0), (0, n_feat_pad - n_feat)))
    feat_rev = np.pad(feat_rev, ((0, 0), (0, n_feat_pad - n_feat)))
    filt_w1 = jnp.pad(filt_w1, ((0, 0), (0, n_feat_pad - n_feat), (0, 0)))
    n_feat = n_feat_pad
    max_decay = math.log(DECAY_TARGET) / FAST_DECAY_PCT
    min_decay = math.log(DECAY_TARGET) / SLOW_DECAY_PCT
    deltas = np.abs(np.linspace(min_decay, max_decay, hw, dtype=np.float32))[None, :]
    f1, f2, _, _ = _fft_constants(seq_len)
    nct = hw // ct

    def w3_spec(direction):
        return pl.BlockSpec((1, hid, ct), lambda l, c, o: (l, 0, (direction * HYENA_ORDER + o) * nct + c))

    def const(shape):
        return pl.BlockSpec(shape, lambda l, c, o: (0,) * len(shape), pipeline_mode=pl.Buffered(1))

    vec = lambda a: a.reshape(depth, 1, hid)
    vspec = pl.BlockSpec((1, 1, hid), lambda l, c, o: (l, 0, 0))
    return pl.pallas_call(
        functools.partial(_filter_kernel, seq_len=seq_len),
        grid=(depth, nct, HYENA_ORDER),
        in_specs=[
            const((L, n_feat)), const((L, n_feat)),
            pl.BlockSpec((1, n_feat, hid), lambda l, c, o: (l, 0, 0)), vspec, vspec,
            pl.BlockSpec((1, hid, hid), lambda l, c, o: (l, 0, 0)), vspec, vspec,
            w3_spec(0), w3_spec(1),
            pl.BlockSpec((1, ct), lambda l, c, o: (0, c)),
            const((FFT_N2, 2 * pad, n1)), const((2 * FFT_N2, 2 * FFT_N2)),
        ],
        out_specs=pl.BlockSpec((1, 1, nk1, 2 * FFT_N2, ct), lambda l, c, o: (l, o, 0, 0, c)),
        out_shape=jax.ShapeDtypeStruct((depth, HYENA_ORDER, nk1, 2 * FFT_N2, hw), F32),
        scratch_shapes=[
            pltpu.VMEM((2, L, hid), F32),
            pltpu.VMEM((n_fft, ct), F32),
            pltpu.VMEM((FFT_N2 * 2 * pad, ct), F32),
        ],
        compiler_params=_cparams(("arbitrary", "arbitrary", "arbitrary")),
        name="hyena_filter_spectra",
    )(jnp.asarray(feat, F32), jnp.asarray(feat_rev, F32), filt_w1, vec(filt_b1), vec(filt_freq1),
      filt_w2, vec(filt_b2), vec(filt_freq2), filt_w3, filt_w3,
      jnp.asarray(deltas, F32), jnp.asarray(f1), jnp.asarray(f2))


def _dwconv3_rows(x, w_ref, b_ref, prev_row=None, next_row=None):
    n = x.shape[0]
    row = lax.broadcasted_iota(jnp.int32, (n, 1), 0)
    xm = pltpu.roll(x, 1, axis=0)
    xp = pltpu.roll(x, n - 1, axis=0)
    xm = jnp.where(row == 0, 0.0 if prev_row is None else prev_row, xm)
    xp = jnp.where(row == n - 1, 0.0 if next_row is None else next_row, xp)
    return xm * w_ref[0:1, :] + x * w_ref[1:2, :] + xp * w_ref[2:3, :] + b_ref[...]


def _ld_lanes(ref, start, n, stride):
    rows = pl.ds(start, n) if stride == 1 else pl.ds(start, n, stride=stride)
    return jnp.concatenate([ref[h, rows, :] for h in range(ref.shape[0])], axis=1)


def _st_lanes(ref, start, n, stride, val):
    rows = pl.ds(start, n) if stride == 1 else pl.ds(start, n, stride=stride)
    for h in range(ref.shape[0]):
        ref[h, rows, :] = val[:, h * LANES:(h + 1) * LANES]


def _dwconv3_to_slabs(src_ref, w_ref, b_ref, dst_ref, seq_len, chunk):
    halo = 16
    n_chunks = seq_len // chunk
    for c in range(n_chunks):
        r0 = c * chunk
        x = src_ref[0, r0:r0 + chunk, :].astype(F32)
        prev_row = src_ref[0, r0 - halo:r0, :].astype(F32)[halo - 1:halo] if c > 0 else None
        next_row = src_ref[0, r0 + chunk:r0 + chunk + halo, :].astype(F32)[0:1] if c < n_chunks - 1 else None
        y = _dwconv3_rows(x, w_ref, b_ref, prev_row, next_row)
        for h in range(dst_ref.shape[0]):
            dst_ref[h, r0:r0 + chunk, :] = y[:, h * LANES:(h + 1) * LANES]


def _hyena_conv_kernel(zin_ref, ux_ref, wz_ref, bz_ref, wx_ref, bx_ref, hb_ref, kh_ref,
                       f1_ref, f2_ref, g2_ref, g1_ref, o_ref, z_ref, xg_ref, a_ref, *, seq_len, first):
    n_fft, n1, nk1, pad = _fft_dims(seq_len)
    L = seq_len
    half = n1 // 2
    nslab = z_ref.shape[0]
    chunk = min(L, 512)

    if first:
        _dwconv3_to_slabs(zin_ref, wz_ref, bz_ref, z_ref, L, chunk)
    else:
        for h in range(nslab):
            z_ref[h] = zin_ref[0, :, h * LANES:(h + 1) * LANES]
    _dwconv3_to_slabs(ux_ref, wx_ref, bx_ref, xg_ref, L, chunk)

    def body1(i2, carry):
        xs = _ld_lanes(z_ref, i2, half, FFT_N2).astype(BF16)
        a = jnp.dot(f1_ref[i2], xs, preferred_element_type=F32)
        _st_lanes(a_ref, pl.multiple_of(i2 * 2 * pad, SUBLANES), 2 * pad, 1, a)
        return carry
    lax.fori_loop(0, FFT_N2, body1, 0, unroll=4)

    def body2(k1, carry):
        ar = _ld_lanes(a_ref, k1, FFT_N2, 2 * pad)
        ai = _ld_lanes(a_ref, pad + k1, FFT_N2, 2 * pad)
        x = jnp.dot(f2_ref[...], jnp.concatenate([ar, ai], axis=0).astype(BF16), preferred_element_type=F32)
        kk = kh_ref[0, 0, k1]
        xr, xi = x[:FFT_N2], x[FFT_N2:]
        kr, ki = kk[:FFT_N2], kk[FFT_N2:]
        y = jnp.concatenate([xr * kr - xi * ki, xr * ki + xi * kr], axis=0)
        bk = jnp.dot(g2_ref[...], y.astype(BF16), preferred_element_type=F32)
        _st_lanes(a_ref, k1, FFT_N2, 2 * pad, bk[:FFT_N2])
        _st_lanes(a_ref, pad + k1, FFT_N2, 2 * pad, bk[FFT_N2:])
        return carry
    lax.fori_loop(0, nk1, body2, 0, unroll=3 if nk1 % 3 == 0 else 1)

    bias = hb_ref[...]

    def body3(i2, carry):
        bb = _ld_lanes(a_ref, pl.multiple_of(i2 * 2 * pad, SUBLANES), 2 * pad, 1).astype(BF16)
        y = jnp.dot(g1_ref[i2], bb, preferred_element_type=F32)
        z_old = _ld_lanes(z_ref, i2, half, FFT_N2)
        xg = _ld_lanes(xg_ref, i2, half, FFT_N2)
        _st_lanes(z_ref, i2, half, FFT_N2, xg * (y + bias * z_old))
        return carry
    lax.fori_loop(0, FFT_N2, body3, 0, unroll=4)

    for h in range(nslab):
        o_ref[0, :, h * LANES:(h + 1) * LANES] = z_ref[h].astype(o_ref.dtype)


def hyena_step(zin, zin_col0, u3, x_col0, w_cols, conv_w, conv_b, khat, layer, order, hy_bias_row, first, ct):
    nb, L, _ = u3.shape
    hw = hy_bias_row.shape[1]
    n_fft, n1, nk1, pad = _fft_dims(L)
    half = n1 // 2
    f1, f2, g2, g1 = _fft_constants(L)
    f1 = f1[:, :, :half]
    nct = hw // ct
    nslab = ct // LANES
    assert zin_col0 % ct == 0 and x_col0 % ct == 0 and w_cols[0] % ct == 0 and w_cols[1] % ct == 0
    last = order == HYENA_ORDER - 1

    def col_spec(rows, col0):
        return pl.BlockSpec((rows, ct), lambda c, b: (0, col0 // ct + c))

    const = lambda shape: pl.BlockSpec(shape, lambda c, b: (0,) * len(shape), pipeline_mode=pl.Buffered(1))
    return pl.pallas_call(
        functools.partial(_hyena_conv_kernel, seq_len=L, first=first),
        grid=(nct, nb),
        in_specs=[
            pl.BlockSpec((1, L, ct), lambda c, b: (b, 0, zin_col0 // ct + c)),
            pl.BlockSpec((1, L, ct), lambda c, b: (b, 0, x_col0 // ct + c)),
            col_spec(3, w_cols[0]), col_spec(1, w_cols[0]), col_spec(3, w_cols[1]), col_spec(1, w_cols[1]),
            pl.BlockSpec((1, ct), lambda c, b: (0, c)),
            pl.BlockSpec((1, 1, nk1, 2 * FFT_N2, ct), lambda c, b: (layer, order, 0, 0, c),
                         pipeline_mode=pl.Buffered(1)),
            const((FFT_N2, 2 * pad, half)), const((2 * FFT_N2, 2 * FFT_N2)),
            const((2 * FFT_N2, 2 * FFT_N2)), const((FFT_N2, half, 2 * pad)),
        ],
        out_specs=pl.BlockSpec((1, L, ct), lambda c, b: (b, 0, c)),
        out_shape=jax.ShapeDtypeStruct((nb, L, hw), BF16 if last else F32),
        scratch_shapes=[
            pltpu.VMEM((nslab, L, LANES), F32),
            pltpu.VMEM((nslab, L, LANES), F32),
            pltpu.VMEM((nslab, FFT_N2 * 2 * pad, LANES), F32),
        ],
        compiler_params=_cparams(("arbitrary", "arbitrary")),
        name=f"hyena_step{order}",
    )(zin, u3, conv_w, conv_b, conv_w, conv_b, hy_bias_row, khat,
      jnp.asarray(f1, BF16), jnp.asarray(f2, BF16), jnp.asarray(g2, BF16), jnp.asarray(g1, BF16))


def hyena_mix(u3, col0, conv_w, conv_b, khat, layer, hy_bias, ct):
    hw = hy_bias.shape[1]
    z = hyena_step(u3, col0, u3, col0 + hw, (0, hw), conv_w, conv_b, khat, layer, 0, hy_bias[0:1], True, ct)
    return hyena_step(z, 0, u3, col0 + 2 * hw, (0, 2 * hw), conv_w, conv_b, khat, layer, 1, hy_bias[1:2], False, ct)


def _t5_bucket(rel):
    nb = REL_BUCKETS // 2
    ret = (rel > 0).astype(np.int32) * nb
    n = np.abs(rel)
    max_exact = nb // 2
    large = max_exact + (np.log(np.maximum(n, 1) / max_exact) / np.log(REL_MAX_DIST / max_exact)
                         * (nb - max_exact)).astype(np.int32)
    large = np.minimum(large, nb - 1)
    return ret + np.where(n < max_exact, n, large)


def _bias_kernel(rb_ref, bkt_ref, o_ref):
    n_groups = bkt_ref.shape[0]
    for g in range(n_groups):
        bkt = bkt_ref[g]
        for h in range(HEADS_PER_GROUP):
            tile = jnp.zeros(bkt.shape, F32)
            for b in range(REL_BUCKETS):
                tile = jnp.where(bkt == b, rb_ref[b, g * HEADS_PER_GROUP + h], tile)
            o_ref[g, h] = tile


def attention_bias_tiles(rel_bias, n_side_list):
    n_groups = len(ATTN_GROUPS)
    qq = np.arange(ATTN_QB)[:, None]
    kk = np.arange(2 * ATTN_QB)[None, :]
    bkts = []
    for (window, dil), n_side in zip(ATTN_GROUPS, n_side_list):
        j = kk - n_side - qq
        bkts.append(_t5_bucket(j * dil))
    bkt = np.stack(bkts).astype(np.int32)
    return pl.pallas_call(
        _bias_kernel,
        in_specs=[pl.BlockSpec(memory_space=pltpu.SMEM), pl.BlockSpec(memory_space=pltpu.VMEM)],
        out_specs=pl.BlockSpec(memory_space=pltpu.VMEM),
        out_shape=jax.ShapeDtypeStruct((n_groups, HEADS_PER_GROUP, ATTN_QB, 2 * ATTN_QB), F32),
        name="attention_bias_tiles",
    )(rel_bias, jnp.asarray(bkt))


def _attn_kernel(q_ref, k_ref, v_ref, bias_ref, o_ref, l_ref, ks_ref, vs_ref, *, m_len, n_side, tq):
    qt = pl.program_id(2)
    width = HEADS_PER_GROUP * HEAD_DIM
    halo = n_side

    @pl.when(qt == 0)
    def _():
        zeros = jnp.zeros((halo, width), BF16)
        for ref, src in ((ks_ref, k_ref), (vs_ref, v_ref)):
            ref[pl.ds(0, halo), :] = zeros
            ref[pl.ds(halo + m_len, 2 * ATTN_QB - halo), :] = jnp.zeros((2 * ATTN_QB - halo, width), BF16)
            ref[pl.ds(halo, m_len), :] = src[...]

    qq = lax.broadcasted_iota(jnp.int32, (ATTN_QB, 2 * ATTN_QB), 0)
    kk = lax.broadcasted_iota(jnp.int32, (ATTN_QB, 2 * ATTN_QB), 1)
    rel = kk - halo - qq
    band = (rel >= -n_side) & (rel <= n_side)
    lane = lax.broadcasted_iota(jnp.int32, (ATTN_QB, 2 * HEAD_DIM), 1)
    low = lane < HEAD_DIM
    scale = 1.0 / math.sqrt(HEAD_DIM)

    def block(ib, carry):
        r0 = pl.multiple_of(ib * ATTN_QB, ATTN_QB)
        q0 = pl.multiple_of(qt * tq + r0, ATTN_QB)
        kpos = q0 + kk - halo
        mask = band & (kpos >= 0) & (kpos < m_len)
        for hp in range(HEADS_PER_GROUP // 2):
            cols = pl.ds(hp * 2 * HEAD_DIM, 2 * HEAD_DIM)
            qp = q_ref[pl.ds(r0, ATTN_QB), cols]
            kp = ks_ref[pl.ds(q0, 2 * ATTN_QB), cols]
            vp = vs_ref[pl.ds(q0, 2 * ATTN_QB), cols]
            outs, lses = [], []
            for hh in range(2):
                sel = low if hh == 0 else jnp.logical_not(low)
                qm = jnp.where(sel, qp, jnp.zeros_like(qp))
                s = lax.dot_general(qm, kp, (((1,), (1,)), ((), ())), preferred_element_type=F32) * scale
                s = jnp.where(mask, s + bias_ref[2 * hp + hh], NEG_INF)
                mx = jnp.max(s, axis=-1, keepdims=True)
                p = jnp.exp(s - mx)
                den = jnp.sum(p, axis=-1, keepdims=True)
                pv = jnp.dot(p.astype(BF16), vp, preferred_element_type=F32)
                outs.append(pv / den)
                lses.append(mx + jnp.log(den))
            o_ref[pl.ds(r0, ATTN_QB), cols] = jnp.where(low, outs[0], outs[1])
            l_ref[pl.ds(r0, ATTN_QB), cols] = jnp.where(low, lses[0], lses[1])
        return carry
    lax.fori_loop(0, tq // ATTN_QB, block, 0)


def dilated_group_attention(qkv, g, dil, n_side, bias_tiles):
    nb, d, m_len, _ = qkv.shape
    width = HEADS_PER_GROUP * HEAD_DIM
    assert d == dil and m_len % ATTN_QB == 0 and n_side <= ATTN_QB and n_side % 16 == 0
    tq = min(512, m_len)
    kernel = functools.partial(_attn_kernel, m_len=m_len, n_side=n_side, tq=tq)
    return pl.pallas_call(
        kernel,
        grid=(nb, dil, m_len // tq),
        in_specs=[
            pl.BlockSpec((None, None, tq, width), lambda b, r, t: (b, r, t, 0)),
            pl.BlockSpec((None, None, m_len, width), lambda b, r, t: (b, r, 0, 1)),
            pl.BlockSpec((None, None, m_len, width), lambda b, r, t: (b, r, 0, 2)),
            pl.BlockSpec((None, HEADS_PER_GROUP, ATTN_QB, 2 * ATTN_QB), lambda b, r, t: (g, 0, 0, 0)),
        ],
        out_specs=[
            pl.BlockSpec((None, None, tq, width), lambda b, r, t: (b, r, t, 0)),
            pl.BlockSpec((None, None, tq, width), lambda b, r, t: (b, r, t, 0)),
        ],
        out_shape=[
            jax.ShapeDtypeStruct((nb, dil, m_len, width), F32),
            jax.ShapeDtypeStruct((nb, dil, m_len, width), F32),
        ],
        scratch_shapes=[
            pltpu.VMEM((m_len + 2 * ATTN_QB, width), BF16),
            pltpu.VMEM((m_len + 2 * ATTN_QB, width), BF16),
        ],
        compiler_params=_cparams(("arbitrary", "arbitrary", "arbitrary")),
        name=f"dilated_attention_g{g}",
    )(qkv, qkv, qkv, bias_tiles)


def _merge_kernel(yh_ref, *rest, dils):
    ng = len(dils)
    o_refs, l_refs = rest[:ng], rest[ng:2 * ng]
    gh_ref, ga_ref, bh_ref, ba_ref, wh_ref, wa_ref, out_ref, ya_ref, tok_ref = rest[2 * ng:]
    tm, aw = ya_ref.shape
    nslab = aw // LANES

    @pl.when(pl.program_id(1) == 0)
    def _():
        for gi, d in enumerate(dils):
            if d == 1:
                continue
            rows = tm // d
            for r in range(d):
                for s in range(nslab):
                    cols = slice(s * LANES, (s + 1) * LANES)
                    tok_ref[2 * gi, s, pl.ds(r, rows, stride=d), :] = o_refs[gi][r, :, cols]
                    tok_ref[2 * gi + 1, s, pl.ds(r, rows, stride=d), :] = l_refs[gi][r, :, cols]
        for s in range(nslab):
            cols = slice(s * LANES, (s + 1) * LANES)
            os_, ls_ = [], []
            for gi, d in enumerate(dils):
                if d == 1:
                    os_.append(o_refs[gi][0, :, cols])
                    ls_.append(l_refs[gi][0, :, cols])
                else:
                    os_.append(tok_ref[2 * gi, s])
                    ls_.append(tok_ref[2 * gi + 1, s])
            mx = functools.reduce(jnp.maximum, ls_)
            es = [jnp.exp(l - mx) for l in ls_]
            inv = 1.0 / functools.reduce(lambda a, b: a + b, es)
            ya = functools.reduce(lambda a, b: a + b, [(e * inv) * o for e, o in zip(es, os_)])
            ya_ref[:, cols] = ya.astype(BF16)

    acc_h = jnp.dot(yh_ref[...], wh_ref[...], preferred_element_type=F32)
    acc_a = jnp.dot(ya_ref[...], wa_ref[...], preferred_element_type=F32)
    sig = lambda v: 1.0 / (1.0 + jnp.exp(-v))
    g_h = sig(gh_ref[...].astype(F32) + bh_ref[...])
    g_a = sig(ga_ref[...].astype(F32) + ba_ref[...])
    out_ref[...] = (g_h * acc_h + g_a * acc_a).astype(out_ref.dtype)


def branch_merge(y_hy, outs, lses, dils, u2, gate_col0, b_gate, w_hy, w_at, seq_len, tm=512):
    t, hw = y_hy.shape
    d = w_hy.shape[1]
    aw = w_at.shape[0]
    tn = _pick_tile(math.gcd(gate_col0, d), 512)
    gb = gate_col0 // tn
    nj = d // tn
    tps = seq_len // tm
    assert all(tm % (dd * SUBLANES) == 0 for dd in dils)
    grp = [pl.BlockSpec((None, dd, tm // dd, aw), lambda i, j: (i // tps, 0, i % tps, 0)) for dd in dils]
    return pl.pallas_call(
        functools.partial(_merge_kernel, dils=tuple(dils)),
        grid=(t // tm, nj),
        in_specs=[pl.BlockSpec((tm, hw), lambda i, j: (i, 0))] + grp + grp + [
            pl.BlockSpec((tm, tn), lambda i, j: (i, gb + j)),
            pl.BlockSpec((tm, tn), lambda i, j: (i, gb + nj + j)),
            pl.BlockSpec((1, tn), lambda i, j: (0, j)),
            pl.BlockSpec((1, tn), lambda i, j: (0, nj + j)),
            pl.BlockSpec((hw, tn), lambda i, j: (0, j)),
            pl.BlockSpec((aw, tn), lambda i, j: (0, j)),
        ],
        out_specs=pl.BlockSpec((tm, tn), lambda i, j: (i, j)),
        out_shape=jax.ShapeDtypeStruct((t, d), BF16),
        scratch_shapes=[pltpu.VMEM((tm, aw), BF16), pltpu.VMEM((2 * len(dils), aw // LANES, tm, LANES), F32)],
        compiler_params=_cparams(("arbitrary", "arbitrary")),
        name="branch_merge",
    )(y_hy, *outs, *lses, u2, u2, b_gate, b_gate, w_hy, w_at)


def _out_proj_kernel(x_ref, m_ref, gate_ref, w_ref, o_ref):
    acc = jnp.dot(m_ref[...], w_ref[...], preferred_element_type=F32)
    o_ref[...] = x_ref[...] + gate_ref[0] * acc


def out_proj_residual(x, m, mod3, gate_idx, w, seq_len, tm=512):
    t, d = x.shape
    k = m.shape[1]
    tiles_per_seq = seq_len // tm
    return pl.pallas_call(
        _out_proj_kernel,
        grid=(t // tm,),
        in_specs=[
            pl.BlockSpec((tm, d), lambda i: (i, 0)),
            pl.BlockSpec((tm, k), lambda i: (i, 0)),
            pl.BlockSpec((1, 1, d), lambda i: (i // tiles_per_seq, 0, gate_idx)),
            pl.BlockSpec((k, d), lambda i: (0, 0)),
        ],
        out_specs=pl.BlockSpec((tm, d), lambda i: (i, 0)),
        out_shape=jax.ShapeDtypeStruct((t, d), F32),
        input_output_aliases={0: 0},
        compiler_params=_cparams(("arbitrary",)),
        name="out_proj_residual",
    )(x, m, mod3, w)


def _ffn_down_kernel(x_ref, a_ref, gt_ref, gprev_ref, gnext_ref, cw_ref, cb_ref, gate_ref, w_ref,
                     o_ref, acc_ref, *, tiles_per_seq, halo_rows):
    i = pl.program_id(0)
    kstep = pl.program_id(1)
    first = (i % tiles_per_seq) == 0
    last = (i % tiles_per_seq) == tiles_per_seq - 1
    prev_row = jnp.where(first, 0.0, gprev_ref[halo_rows - 1:halo_rows, :].astype(F32))
    next_row = jnp.where(last, 0.0, gnext_ref[0:1, :].astype(F32))
    gt = _dwconv3_rows(gt_ref[...].astype(F32), cw_ref, cb_ref, prev_row, next_row)
    act = (gt * (1.0 / (1.0 + jnp.exp(-gt))) * a_ref[...].astype(F32)).astype(BF16)
    part = jnp.dot(act, w_ref[...], preferred_element_type=F32)

    @pl.when(kstep == 0)
    def _():
        acc_ref[...] = part

    @pl.when(kstep > 0)
    def _():
        acc_ref[...] += part

    @pl.when(kstep == pl.num_programs(1) - 1)
    def _():
        o_ref[...] = x_ref[...] + gate_ref[0] * acc_ref[...]


def ffn_down_residual(x, ag, conv_w, conv_b, mod3, gate_idx, w, seq_len, tm=512):
    t, d = x.shape
    ff = w.shape[0]
    tk = _pick_tile(ff, 1536)
    nk = ff // tk
    halo_rows = 16
    tiles_per_seq = seq_len // tm
    rb = tm // halo_rows
    last_rb = t // halo_rows - 1
    kernel = functools.partial(_ffn_down_kernel, tiles_per_seq=tiles_per_seq, halo_rows=halo_rows)
    return pl.pallas_call(
        kernel,
        grid=(t // tm, nk),
        in_specs=[
            pl.BlockSpec((tm, d), lambda i, k: (i, 0)),
            pl.BlockSpec((tm, tk), lambda i, k: (i, k)),
            pl.BlockSpec((tm, tk), lambda i, k: (i, nk + k)),
            pl.BlockSpec((halo_rows, tk), lambda i, k: (jnp.maximum(i * rb - 1, 0), nk + k)),
            pl.BlockSpec((halo_rows, tk), lambda i, k: (jnp.minimum((i + 1) * rb, last_rb), nk + k)),
            pl.BlockSpec((3, tk), lambda i, k: (0, k)),
            pl.BlockSpec((1, tk), lambda i, k: (0, k)),
            pl.BlockSpec((1, 1, d), lambda i, k: (i // tiles_per_seq, 0, gate_idx)),
            pl.BlockSpec((tk, d), lambda i, k: (k, 0)),
        ],
        out_specs=pl.BlockSpec((tm, d), lambda i, k: (i, 0)),
        out_shape=jax.ShapeDtypeStruct((t, d), F32),
        scratch_shapes=[pltpu.VMEM((tm, d), F32)],
        input_output_aliases={0: 0},
        compiler_params=_cparams(("arbitrary", "arbitrary")),
        name="ffn_down_residual",
    )(x, ag, ag, ag, ag, conv_w, conv_b, mod3, w)


def _final_norm_kernel(x_ref, g_ref, o_ref):
    x = x_ref[...]
    ms = jnp.mean(x * x, axis=-1, keepdims=True)
    o_ref[...] = x * lax.rsqrt(ms + NORM_EPS) * g_ref[...]


def final_norm(x, g, row0, n_rows, tm=512):
    d = x.shape[1]
    off = row0 // tm
    return pl.pallas_call(
        _final_norm_kernel,
        grid=(n_rows // tm,),
        in_specs=[pl.BlockSpec((tm, d), lambda i: (off + i, 0)), pl.BlockSpec((1, d), lambda i: (0, 0))],
        out_specs=pl.BlockSpec((tm, d), lambda i: (i, 0)),
        out_shape=jax.ShapeDtypeStruct((n_rows, d), F32),
        compiler_params=_cparams(("arbitrary",)),
        name="final_norm",
    )(x, g.reshape(1, d))


def kernel(x_prompt, x_sample, c_prompt, c_sample, ada_w, ada_b, norm1_g, w_in, b_gate, hy_conv_w, hy_conv_b,
           filt_w1, filt_b1, filt_freq1, filt_w2, filt_b2, filt_freq2, filt_w3, hy_bias, rel_bias, w_br_hy,
           w_br_attn, w_out, norm2_g, ffn_up, ffn_conv_w, ffn_conv_b, ffn_down, final_g):
    bp, L, d = x_prompt.shape
    bs = x_sample.shape[0]
    assert x_sample.shape[1] == L
    nb = bp + bs
    depth = ada_w.shape[0]
    hw = hy_bias.shape[2]
    n_groups = len(ATTN_GROUPS)
    attn_w = n_groups * HEADS_PER_GROUP * HEAD_DIM
    hy_cols = 3 * hw
    gate_col0 = hy_cols + 3 * attn_w
    assert w_in.shape[2] == gate_col0 + 2 * d

    x = jnp.concatenate([x_prompt, x_sample], axis=0).reshape(nb * L, d)
    nb_pad = -(-nb // SUBLANES) * SUBLANES
    c_pad = jnp.zeros((nb_pad, d), F32).at[:nb].set(jnp.concatenate([c_prompt, c_sample], axis=0))
    mod = ada_modulation(c_pad, ada_w, ada_b)

    hy_ct = min(HYENA_CT, hw)
    khat = hyena_filter_spectra(L, filt_w1, filt_b1, filt_freq1, filt_w2, filt_b2, filt_freq2, filt_w3, LANES)
    n_sides = [(window // 2) // dil for window, dil in ATTN_GROUPS]
    bias_tiles = attention_bias_tiles(rel_bias, n_sides)

    dils = [dil for _, dil in ATTN_GROUPS]
    gw = HEADS_PER_GROUP * HEAD_DIM

    def group_cols(w, g):
        return [w[:, hy_cols + part * attn_w + g * gw: hy_cols + part * attn_w + (g + 1) * gw] for part in range(3)]

    for l in range(depth):
        mod3 = mod[l].reshape(nb_pad, 1, N_MOD * d)
        w_l = w_in[l]
        w_main = jnp.concatenate([w_l[:, :hy_cols], w_l[:, gate_col0:]], axis=1).astype(BF16)
        w_groups = jnp.concatenate([c for g in range(n_groups) for c in group_cols(w_l, g)], axis=1).astype(BF16)
        u, qkv = in_projection(x, norm1_g[l], mod3, 1, 0, w_main, w_groups, dils, nb, L)
        u3 = u.reshape(nb, L, u.shape[1])
        y_hy = hyena_mix(u3, 0, hy_conv_w[l], hy_conv_b[l].reshape(1, hy_cols), khat, l, hy_bias[l], hy_ct)
        outs, lses = [], []
        for g, dil in enumerate(dils):
            o_g, l_g = dilated_group_attention(qkv[g], g, dil, n_sides[g], bias_tiles)
            outs.append(o_g)
            lses.append(l_g)
        merged = branch_merge(y_hy.reshape(nb * L, hw), outs, lses, dils, u, hy_cols, b_gate[l].reshape(1, 2 * d),
                              w_br_hy[l].astype(BF16), w_br_attn[l].astype(BF16), L)
        x = out_proj_residual(x, merged, mod3, 2, w_out[l].astype(BF16), L)
        ag = normmod_matmul(x, norm2_g[l], mod3, 4, 3, ffn_up[l].astype(BF16), L)
        ff = ffn_down.shape[1]
        x = ffn_down_residual(x, ag, ffn_conv_w[l], ffn_conv_b[l].reshape(1, ff), mod3, 5,
                              ffn_down[l].astype(BF16), L)

    y_prompt = final_norm(x, final_g, 0, bp * L).reshape(bp, L, d)
    y_sample = final_norm(x, final_g, bp * L, bs * L).reshape(bs, L, d)
    return (y_prompt, y_sample)
```

```python
import functools
import math

import jax
import jax.numpy as jnp
import numpy as np
from jax import lax
from jax.experimental import pallas as pl
from jax.experimental.pallas import tpu as pltpu

F32 = jnp.float32
BF16 = jnp.bfloat16
HIGHEST = lax.Precision.HIGHEST

NORM_EPS = 1e-6
N_MOD = 6
HYENA_ORDER = 2
N_BANDS = 16
FAST_DECAY_PCT = 0.3
SLOW_DECAY_PCT = 1.5
DECAY_TARGET = 1e-2
ATTN_GROUPS = ((128, 1), (512, 4), (2048, 16))
HEADS_PER_GROUP = 8
HEAD_DIM = 64
REL_BUCKETS = 32
REL_MAX_DIST = 1024
NEG_INF = -1e30

LANES = 128
SUBLANES = 8
VMEM_LIMIT = 56 * 1024 * 1024

FFT_N2 = 128
FFT_GROUP = 4
HYENA_CT = 256
ATTN_QB = 128


def _cparams(sem):
    return pltpu.CompilerParams(dimension_semantics=sem, vmem_limit_bytes=VMEM_LIMIT)


def _pick_tile(n, cap):
    best = None
    for t in range(LANES, min(n, cap) + 1, LANES):
        if n % t == 0:
            best = t
    assert best is not None, (n, cap)
    return best


def _ada_kernel(c_ref, w_ref, b_ref, o_ref):
    c = c_ref[...]
    cs = c * (1.0 / (1.0 + jnp.exp(-c)))
    o_ref[0] = jnp.dot(cs, w_ref[0], preferred_element_type=F32, precision=HIGHEST) + b_ref[0]


def ada_modulation(c_pad, ada_w, ada_b):
    depth, d, n = ada_w.shape
    nb = c_pad.shape[0]
    tn = _pick_tile(n, 1024)
    return pl.pallas_call(
        _ada_kernel,
        grid=(depth, n // tn),
        in_specs=[
            pl.BlockSpec((nb, d), lambda l, j: (0, 0)),
            pl.BlockSpec((1, d, tn), lambda l, j: (l, 0, j)),
            pl.BlockSpec((1, 1, tn), lambda l, j: (l, 0, j)),
        ],
        out_specs=pl.BlockSpec((1, nb, tn), lambda l, j: (l, 0, j)),
        out_shape=jax.ShapeDtypeStruct((depth, nb, n), F32),
        compiler_params=_cparams(("arbitrary", "arbitrary")),
        name="ada_modulation",
    )(c_pad, ada_w, ada_b.reshape(depth, 1, n))


def _normmod_rows(x, g, sc, sh):
    ms = jnp.mean(x * x, axis=-1, keepdims=True)
    y = x * lax.rsqrt(ms + NORM_EPS) * g
    return y * (1.0 + sc) + sh


def _normmod_matmul_kernel(x_ref, g_ref, sc_ref, sh_ref, w_ref, o_ref, h_ref):
    @pl.when(pl.program_id(1) == 0)
    def _():
        h_ref[...] = _normmod_rows(x_ref[...], g_ref[...], sc_ref[0], sh_ref[0]).astype(BF16)

    o_ref[...] = jnp.dot(h_ref[...], w_ref[...], preferred_element_type=F32).astype(o_ref.dtype)


def normmod_matmul(x, g, mod3, sc_idx, sh_idx, w, seq_len, tm=512, tn_cap=3072):
    t, d = x.shape
    n = w.shape[1]
    tn = _pick_tile(n, tn_cap)
    tiles_per_seq = seq_len // tm
    return pl.pallas_call(
        _normmod_matmul_kernel,
        grid=(t // tm, n // tn),
        in_specs=[
            pl.BlockSpec((tm, d), lambda i, j: (i, 0)),
            pl.BlockSpec((1, d), lambda i, j: (0, 0)),
            pl.BlockSpec((1, 1, d), lambda i, j: (i // tiles_per_seq, 0, sc_idx)),
            pl.BlockSpec((1, 1, d), lambda i, j: (i // tiles_per_seq, 0, sh_idx)),
            pl.BlockSpec((d, tn), lambda i, j: (0, j)),
        ],
        out_specs=pl.BlockSpec((tm, tn), lambda i, j: (i, j)),
        out_shape=jax.ShapeDtypeStruct((t, n), BF16),
        scratch_shapes=[pltpu.VMEM((tm, d), BF16)],
        compiler_params=_cparams(("arbitrary", "arbitrary")),
        name="normmod_matmul",
    )(x, g.reshape(1, d), mod3, mod3, w)


def _in_proj_kernel(x_ref, g_ref, sc_ref, sh_ref, wm_ref, wg_ref, om_ref, *rest, n_main, dils, chunk):
    og_refs = rest[:len(dils)]
    h_ref, perm_ref = rest[len(dils):]
    j = pl.program_id(1)
    tm = h_ref.shape[0]

    @pl.when(j == 0)
    def _():
        h_ref[...] = _normmod_rows(x_ref[...], g_ref[...], sc_ref[0], sh_ref[0]).astype(BF16)

    @pl.when(j < n_main)
    def _():
        om_ref[...] = jnp.dot(h_ref[...], wm_ref[...], preferred_element_type=F32).astype(om_ref.dtype)

    for gi, (o_ref, d) in enumerate(zip(og_refs, dils)):
        @pl.when(j == n_main + gi)
        def _():
            rows = tm // d
            for c0 in range(0, wg_ref.shape[1], chunk):
                res = jnp.dot(h_ref[...], wg_ref[:, c0:c0 + chunk], preferred_element_type=F32)
                if d == 1:
                    o_ref[0, 0, :, c0:c0 + chunk] = res.astype(o_ref.dtype)
                    continue
                for s in range(chunk // LANES):
                    perm_ref[s] = res[:, s * LANES:(s + 1) * LANES]
                for r in range(d):
                    for s in range(chunk // LANES):
                        cols = slice(c0 + s * LANES, c0 + (s + 1) * LANES)
                        o_ref[0, r, :, cols] = perm_ref[s, pl.ds(r, rows, stride=d), :].astype(o_ref.dtype)


def in_projection(x, g, mod3, sc_idx, sh_idx, w_main, w_groups, dils, nb, seq_len, tm=512, tn_cap=1792):
    t, d_model = x.shape
    n_main = w_main.shape[1]
    gw = w_groups.shape[1] // len(dils)
    tn = _pick_tile(n_main, tn_cap)
    nj_main = n_main // tn
    tps = seq_len // tm
    chunk = gw // 3
    assert chunk % LANES == 0 and all(tm % (dd * 16) == 0 for dd in dils)
    kernel = functools.partial(_in_proj_kernel, n_main=nj_main, dils=tuple(dils), chunk=chunk)
    outs = pl.pallas_call(
        kernel,
        grid=(t // tm, nj_main + len(dils)),
        in_specs=[
            pl.BlockSpec((tm, d_model), lambda i, j: (i, 0)),
            pl.BlockSpec((1, d_model), lambda i, j: (0, 0)),
            pl.BlockSpec((1, 1, d_model), lambda i, j: (i // tps, 0, sc_idx)),
            pl.BlockSpec((1, 1, d_model), lambda i, j: (i // tps, 0, sh_idx)),
            pl.BlockSpec((d_model, tn), lambda i, j: (0, jnp.minimum(j, nj_main - 1))),
            pl.BlockSpec((d_model, gw), lambda i, j: (0, jnp.clip(j - nj_main, 0, len(dils) - 1))),
        ],
        out_specs=[pl.BlockSpec((tm, tn), lambda i, j: (i, jnp.minimum(j, nj_main - 1)))] + [
            pl.BlockSpec((1, dd, tm // dd, gw), lambda i, j: (i // tps, 0, i % tps, 0)) for dd in dils],
        out_shape=[jax.ShapeDtypeStruct((t, n_main), BF16)] + [
            jax.ShapeDtypeStruct((nb, dd, seq_len // dd, gw), BF16) for dd in dils],
        scratch_shapes=[pltpu.VMEM((tm, d_model), BF16), pltpu.VMEM((chunk // LANES, tm, LANES), F32)],
        compiler_params=_cparams(("arbitrary", "arbitrary")),
        name="in_projection",
    )(x, g.reshape(1, d_model), mod3, mod3, w_main, w_groups)
    return outs[0], list(outs[1:])


def _fft_dims(seq_len):
    n_fft = 2 * seq_len
    n1 = n_fft // FFT_N2
    assert n1 * FFT_N2 == n_fft and n1 % 2 == 0
    nk1 = n1 // 2 + 1
    pad = -(-nk1 // SUBLANES) * SUBLANES
    return n_fft, n1, nk1, pad


@functools.lru_cache(maxsize=None)
def _fft_constants(seq_len):
    n_fft, n1, nk1, pad = _fft_dims(seq_len)
    n2 = FFT_N2
    k1 = np.arange(nk1)[None, :, None]
    i1 = np.arange(n1)[None, None, :]
    i2 = np.arange(n2)[:, None, None]
    ph = 2.0 * np.pi * ((k1 * (i1 * n2 + i2)) % n_fft) / n_fft
    f1 = np.zeros((n2, 2 * pad, n1), np.float64)
    f1[:, :nk1] = np.cos(ph)
    f1[:, pad:pad + nk1] = -np.sin(ph)
    ck = np.full((nk1,), 2.0)
    ck[0] = 1.0
    ck[-1] = 1.0
    g1 = np.zeros((n2, n1 // 2, 2 * pad), np.float64)
    pht = np.transpose(ph[:, :, :n1 // 2], (0, 2, 1))
    g1[:, :, :nk1] = np.cos(pht) * ck / n_fft
    g1[:, :, pad:pad + nk1] = -np.sin(pht) * ck / n_fft
    a = np.arange(n2)
    ph2 = 2.0 * np.pi * ((a[:, None] * a[None, :]) % n2) / n2
    c2, s2 = np.cos(ph2), np.sin(ph2)
    f2 = np.block([[c2, s2], [-s2, c2]])
    g2 = np.block([[c2, -s2], [s2, c2]])
    return (f1.astype(np.float32), f2.astype(np.float32), g2.astype(np.float32), g1.astype(np.float32))


def _block_diag_groups(mats, group):
    n, r, c = mats.shape
    out = np.zeros((n // group, group * r, group * c), mats.dtype)
    for j in range(group):
        out[:, j * r:(j + 1) * r, j * c:(j + 1) * c] = mats[j::group]
    return out


def _ld_lanes(ref, start, n, stride):
    rows = pl.ds(start, n) if stride == 1 else pl.ds(start, n, stride=stride)
    return jnp.concatenate([ref[h, rows, :] for h in range(ref.shape[0])], axis=1)


def _st_lanes(ref, start, n, stride, val):
    rows = pl.ds(start, n) if stride == 1 else pl.ds(start, n, stride=stride)
    for h in range(ref.shape[0]):
        ref[h, rows, :] = val[:, h * LANES:(h + 1) * LANES]


def _fft_stage1(src_ref, f1b_ref, a_ref, n_rows, pad):
    def body(g, carry):
        xs = jnp.concatenate([_ld_lanes(src_ref, g * FFT_GROUP + j, n_rows, FFT_N2) for j in range(FFT_GROUP)],
                             axis=0).astype(BF16)
        a = jnp.dot(f1b_ref[g], xs, preferred_element_type=F32)
        _st_lanes(a_ref, pl.multiple_of(g * FFT_GROUP * 2 * pad, SUBLANES), FFT_GROUP * 2 * pad, 1, a)
        return carry
    lax.fori_loop(0, FFT_N2 // FFT_GROUP, body, 0, unroll=2)


def _fft_stage2_rhs(a_ref, k1, pad):
    ar = _ld_lanes(a_ref, k1, FFT_N2, 2 * pad)
    ai = _ld_lanes(a_ref, pad + k1, FFT_N2, 2 * pad)
    return jnp.concatenate([ar, ai], axis=0).astype(BF16)


def _filter_kernel(feat_ref, featr_ref, w1_ref, b1_ref, fr1_ref, w2_ref, b2_ref, fr2_ref,
                   w3f_ref, w3b_ref, delta_ref, f1_ref, f2_ref,
                   o_ref, hid_ref, k2_ref, a_ref, *, seq_len):
    n_fft, n1, nk1, pad = _fft_dims(seq_len)
    L = seq_len

    @pl.when((pl.program_id(1) == 0) & (pl.program_id(2) == 0))
    def _():
        for d, fref in enumerate((feat_ref, featr_ref)):
            h = jnp.dot(fref[...], w1_ref[0], preferred_element_type=F32, precision=HIGHEST) + b1_ref[0]
            h = jnp.sin(fr1_ref[0] * h)
            h = jnp.dot(h, w2_ref[0], preferred_element_type=F32, precision=HIGHEST) + b2_ref[0]
            hid_ref[d] = jnp.sin(fr2_ref[0] * h)

    delta = delta_ref[...]
    chunk = min(L, 512)
    row = lax.broadcasted_iota(jnp.int32, (chunk, 1), 0)
    l1 = jnp.zeros(delta.shape, F32)
    for d, (w3_ref, fref) in enumerate(((w3f_ref, feat_ref), (w3b_ref, featr_ref))):
        for c in range(L // chunk):
            rows = pl.ds(c * chunk, chunk)
            h = jnp.dot(hid_ref[d, rows, :], w3_ref[0], preferred_element_type=F32, precision=HIGHEST)
            h = h * jnp.exp(-fref[rows, 0:1] * delta)
            if d == 1 and c == 0:
                h = jnp.where(row == 0, 0.0, h)
            l1 = l1 + jnp.sum(jnp.abs(h), axis=0, keepdims=True)
            _st_lanes(k2_ref, d * L + c * chunk, chunk, 1, h)
    inv = 1.0 / l1
    _fft_stage1(k2_ref, f1_ref, a_ref, n1, pad)

    def body(k1, carry):
        spec = jnp.dot(f2_ref[...], _fft_stage2_rhs(a_ref, k1, pad), preferred_element_type=F32) * inv
        o_ref[0, 0, k1] = spec.astype(o_ref.dtype)
        return carry
    lax.fori_loop(0, nk1, body, 0, unroll=3 if nk1 % 3 == 0 else 1)


def hyena_filter_spectra(seq_len, filt_w1, filt_b1, filt_freq1, filt_w2, filt_b2, filt_freq2, filt_w3, ct):
    depth, n_feat, hid = filt_w1.shape
    hw = filt_w3.shape[2] // (2 * HYENA_ORDER)
    n_fft, n1, nk1, pad = _fft_dims(seq_len)
    L = seq_len
    t = np.linspace(0.0, 1.0, L, dtype=np.float32).astype(np.float64)[:, None]
    pos = np.arange(L, dtype=np.float64)[:, None]
    bands = np.linspace(1e-4, N_BANDS - 1, N_BANDS, dtype=np.float32).astype(np.float64)[None, :]
    ang = (2.0 * math.pi / L) * pos * bands
    feat = np.concatenate([t, np.cos(ang), -np.sin(ang)], axis=-1)
    assert feat.shape[1] == n_feat
    feat_rev = np.concatenate([np.zeros((1, n_feat)), feat[:0:-1]], axis=0)
    n_feat_pad = -(-n_feat // LANES) * LANES
    feat = np.pad(feat, ((0, 0), (0, n_feat_pad - n_feat)))
    feat_rev = np.pad(feat_rev, ((0, 0), (0, n_feat_pad - n_feat)))
    filt_w1 = jnp.pad(filt_w1, ((0, 0), (0, n_feat_pad - n_feat), (0, 0)))
    n_feat = n_feat_pad
    max_decay = math.log(DECAY_TARGET) / FAST_DECAY_PCT
    min_decay = math.log(DECAY_TARGET) / SLOW_DECAY_PCT
    deltas = np.abs(np.linspace(min_decay, max_decay, hw, dtype=np.float32))[None, :]
    f1, f2, _, _ = _fft_constants(seq_len)
    f1 = _block_diag_groups(f1, FFT_GROUP)
    nct = hw // ct
    nslab = ct // LANES

    def w3_spec(direction):
        return pl.BlockSpec((1, hid, ct), lambda l, c, o: (l, 0, (direction * HYENA_ORDER + o) * nct + c))

    def const(shape):
        return pl.BlockSpec(shape, lambda l, c, o: (0,) * len(shape), pipeline_mode=pl.Buffered(1))

    vec = lambda a: a.reshape(depth, 1, hid)
    vspec = pl.BlockSpec((1, 1, hid), lambda l, c, o: (l, 0, 0))
    return pl.pallas_call(
        functools.partial(_filter_kernel, seq_len=seq_len),
        grid=(depth, nct, HYENA_ORDER),
        in_specs=[
            const((L, n_feat)), const((L, n_feat)),
            pl.BlockSpec((1, n_feat, hid), lambda l, c, o: (l, 0, 0)), vspec, vspec,
            pl.BlockSpec((1, hid, hid), lambda l, c, o: (l, 0, 0)), vspec, vspec,
            w3_spec(0), w3_spec(1),
            pl.BlockSpec((1, ct), lambda l, c, o: (0, c)),
            const(f1.shape), const((2 * FFT_N2, 2 * FFT_N2)),
        ],
        out_specs=pl.BlockSpec((1, 1, nk1, 2 * FFT_N2, ct), lambda l, c, o: (l, o, 0, 0, c)),
        out_shape=jax.ShapeDtypeStruct((depth, HYENA_ORDER, nk1, 2 * FFT_N2, hw), BF16),
        scratch_shapes=[
            pltpu.VMEM((2, L, hid), F32),
            pltpu.VMEM((nslab, n_fft, LANES), F32),
            pltpu.VMEM((nslab, FFT_N2 * 2 * pad, LANES), F32),
        ],
        compiler_params=_cparams(("arbitrary", "arbitrary", "arbitrary")),
        name="hyena_filter_spectra",
    )(jnp.asarray(feat, F32), jnp.asarray(feat_rev, F32), filt_w1, vec(filt_b1), vec(filt_freq1),
      filt_w2, vec(filt_b2), vec(filt_freq2), filt_w3, filt_w3,
      jnp.asarray(deltas, F32), jnp.asarray(f1, BF16), jnp.asarray(f2, BF16))


def _dwconv3_rows(x, w_ref, b_ref, prev_row=None, next_row=None):
    n = x.shape[0]
    row = lax.broadcasted_iota(jnp.int32, (n, 1), 0)
    xm = pltpu.roll(x, 1, axis=0)
    xp = pltpu.roll(x, n - 1, axis=0)
    xm = jnp.where(row == 0, 0.0 if prev_row is None else prev_row, xm)
    xp = jnp.where(row == n - 1, 0.0 if next_row is None else next_row, xp)
    return xm * w_ref[0:1, :] + x * w_ref[1:2, :] + xp * w_ref[2:3, :] + b_ref[...]


def _dwconv3_to_slabs(src_ref, w_ref, b_ref, dst_ref, seq_len, chunk):
    halo = 16
    n_chunks = seq_len // chunk
    for c in range(n_chunks):
        r0 = c * chunk
        x = src_ref[0, r0:r0 + chunk, :].astype(F32)
        prev_row = src_ref[0, r0 - halo:r0, :].astype(F32)[halo - 1:halo] if c > 0 else None
        next_row = src_ref[0, r0 + chunk:r0 + chunk + halo, :].astype(F32)[0:1] if c < n_chunks - 1 else None
        y = _dwconv3_rows(x, w_ref, b_ref, prev_row, next_row)
        for h in range(dst_ref.shape[0]):
            dst_ref[h, r0:r0 + chunk, :] = y[:, h * LANES:(h + 1) * LANES]


def _hyena_conv_kernel(zin_ref, ux_ref, wz_ref, bz_ref, wx_ref, bx_ref, hb_ref, kh_ref,
                       f1_ref, f2_ref, g2_ref, g1_ref, o_ref, z_ref, xg_ref, a_ref, *, seq_len, first):
    n_fft, n1, nk1, pad = _fft_dims(seq_len)
    L = seq_len
    half = n1 // 2
    nslab = z_ref.shape[0]
    chunk = min(L, 512)

    if first:
        _dwconv3_to_slabs(zin_ref, wz_ref, bz_ref, z_ref, L, chunk)
    else:
        for h in range(nslab):
            z_ref[h] = zin_ref[0, :, h * LANES:(h + 1) * LANES]
    _dwconv3_to_slabs(ux_ref, wx_ref, bx_ref, xg_ref, L, chunk)

    _fft_stage1(z_ref, f1_ref, a_ref, half, pad)

    def body2(k1, carry):
        x = jnp.dot(f2_ref[...], _fft_stage2_rhs(a_ref, k1, pad), preferred_element_type=F32)
        kk = kh_ref[0, 0, k1].astype(F32)
        xr, xi = x[:FFT_N2], x[FFT_N2:]
        kr, ki = kk[:FFT_N2], kk[FFT_N2:]
        y = jnp.concatenate([xr * kr - xi * ki, xr * ki + xi * kr], axis=0)
        bk = jnp.dot(g2_ref[...], y.astype(BF16), preferred_element_type=F32)
        _st_lanes(a_ref, k1, FFT_N2, 2 * pad, bk[:FFT_N2])
        _st_lanes(a_ref, pad + k1, FFT_N2, 2 * pad, bk[FFT_N2:])
        return carry
    lax.fori_loop(0, nk1, body2, 0, unroll=3 if nk1 % 3 == 0 else 1)

    bias = hb_ref[...]

    def body3(g, carry):
        rows = FFT_GROUP * 2 * pad
        bb = _ld_lanes(a_ref, pl.multiple_of(g * rows, SUBLANES), rows, 1).astype(BF16)
        y = jnp.dot(g1_ref[g], bb, preferred_element_type=F32)
        for j in range(FFT_GROUP):
            i2 = g * FFT_GROUP + j
            z_old = _ld_lanes(z_ref, i2, half, FFT_N2)
            xg = _ld_lanes(xg_ref, i2, half, FFT_N2)
            _st_lanes(z_ref, i2, half, FFT_N2, xg * (y[j * half:(j + 1) * half] + bias * z_old))
        return carry
    lax.fori_loop(0, FFT_N2 // FFT_GROUP, body3, 0, unroll=2)

    for h in range(nslab):
        o_ref[0, :, h * LANES:(h + 1) * LANES] = z_ref[h].astype(o_ref.dtype)


def hyena_step(zin, zin_col0, u3, x_col0, w_cols, conv_w, conv_b, khat, layer, order, hy_bias_row, first, ct):
    nb, L, _ = u3.shape
    hw = hy_bias_row.shape[1]
    n_fft, n1, nk1, pad = _fft_dims(L)
    half = n1 // 2
    f1, f2, g2, g1 = _fft_constants(L)
    f1 = _block_diag_groups(f1[:, :, :half], FFT_GROUP)
    g1 = _block_diag_groups(g1, FFT_GROUP)
    nct = hw // ct
    nslab = ct // LANES
    assert zin_col0 % ct == 0 and x_col0 % ct == 0 and w_cols[0] % ct == 0 and w_cols[1] % ct == 0
    last = order == HYENA_ORDER - 1

    def col_spec(rows, col0):
        return pl.BlockSpec((rows, ct), lambda c, b: (0, col0 // ct + c))

    const = lambda shape: pl.BlockSpec(shape, lambda c, b: (0,) * len(shape), pipeline_mode=pl.Buffered(1))
    return pl.pallas_call(
        functools.partial(_hyena_conv_kernel, seq_len=L, first=first),
        grid=(nct, nb),
        in_specs=[
            pl.BlockSpec((1, L, ct), lambda c, b: (b, 0, zin_col0 // ct + c)),
            pl.BlockSpec((1, L, ct), lambda c, b: (b, 0, x_col0 // ct + c)),
            col_spec(3, w_cols[0]), col_spec(1, w_cols[0]), col_spec(3, w_cols[1]), col_spec(1, w_cols[1]),
            pl.BlockSpec((1, ct), lambda c, b: (0, c)),
            pl.BlockSpec((1, 1, nk1, 2 * FFT_N2, ct), lambda c, b: (layer, order, 0, 0, c),
                         pipeline_mode=pl.Buffered(1)),
            const(f1.shape), const((2 * FFT_N2, 2 * FFT_N2)),
            const((2 * FFT_N2, 2 * FFT_N2)), const(g1.shape),
        ],
        out_specs=pl.BlockSpec((1, L, ct), lambda c, b: (b, 0, c)),
        out_shape=jax.ShapeDtypeStruct((nb, L, hw), BF16 if last else F32),
        scratch_shapes=[
            pltpu.VMEM((nslab, L, LANES), F32),
            pltpu.VMEM((nslab, L, LANES), F32),
            pltpu.VMEM((nslab, FFT_N2 * 2 * pad, LANES), F32),
        ],
        compiler_params=_cparams(("arbitrary", "arbitrary")),
        name=f"hyena_step{order}",
    )(zin, u3, conv_w, conv_b, conv_w, conv_b, hy_bias_row, khat,
      jnp.asarray(f1, BF16), jnp.asarray(f2, BF16), jnp.asarray(g2, BF16), jnp.asarray(g1, BF16))


def hyena_mix(u3, col0, conv_w, conv_b, khat, layer, hy_bias, ct):
    hw = hy_bias.shape[1]
    z = hyena_step(u3, col0, u3, col0 + hw, (0, hw), conv_w, conv_b, khat, layer, 0, hy_bias[0:1], True, ct)
    return hyena_step(z, 0, u3, col0 + 2 * hw, (0, 2 * hw), conv_w, conv_b, khat, layer, 1, hy_bias[1:2], False, ct)


def _t5_bucket(rel):
    nb = REL_BUCKETS // 2
    ret = (rel > 0).astype(np.int32) * nb
    n = np.abs(rel)
    max_exact = nb // 2
    large = max_exact + (np.log(np.maximum(n, 1) / max_exact) / np.log(REL_MAX_DIST / max_exact)
                         * (nb - max_exact)).astype(np.int32)
    large = np.minimum(large, nb - 1)
    return ret + np.where(n < max_exact, n, large)


def _bias_kernel(rb_ref, bkt_ref, o_ref):
    n_groups = bkt_ref.shape[0]
    for g in range(n_groups):
        bkt = bkt_ref[g]
        for h in range(HEADS_PER_GROUP):
            tile = jnp.zeros(bkt.shape, F32)
            for b in range(REL_BUCKETS):
                tile = jnp.where(bkt == b, rb_ref[b, g * HEADS_PER_GROUP + h], tile)
            o_ref[g, h] = tile


def attention_bias_tiles(rel_bias, n_side_list):
    n_groups = len(ATTN_GROUPS)
    qq = np.arange(ATTN_QB)[:, None]
    kk = np.arange(2 * ATTN_QB)[None, :]
    bkts = []
    for (window, dil), n_side in zip(ATTN_GROUPS, n_side_list):
        j = kk - n_side - qq
        bkts.append(_t5_bucket(j * dil))
    bkt = np.stack(bkts).astype(np.int32)
    return pl.pallas_call(
        _bias_kernel,
        in_specs=[pl.BlockSpec(memory_space=pltpu.SMEM), pl.BlockSpec(memory_space=pltpu.VMEM)],
        out_specs=pl.BlockSpec(memory_space=pltpu.VMEM),
        out_shape=jax.ShapeDtypeStruct((n_groups, HEADS_PER_GROUP, ATTN_QB, 2 * ATTN_QB), F32),
        name="attention_bias_tiles",
    )(rel_bias, jnp.asarray(bkt))


def _attn_kernel(q_ref, k_ref, v_ref, bias_ref, o_ref, l_ref, ks_ref, vs_ref, *, m_len, n_side, tq):
    qt = pl.program_id(2)
    width = HEADS_PER_GROUP * HEAD_DIM
    halo = n_side

    @pl.when(qt == 0)
    def _():
        zeros = jnp.zeros((halo, width), BF16)
        for ref, src in ((ks_ref, k_ref), (vs_ref, v_ref)):
            ref[pl.ds(0, halo), :] = zeros
            ref[pl.ds(halo + m_len, 2 * ATTN_QB - halo), :] = jnp.zeros((2 * ATTN_QB - halo, width), BF16)
            ref[pl.ds(halo, m_len), :] = src[...]

    qq = lax.broadcasted_iota(jnp.int32, (ATTN_QB, 2 * ATTN_QB), 0)
    kk = lax.broadcasted_iota(jnp.int32, (ATTN_QB, 2 * ATTN_QB), 1)
    rel = kk - halo - qq
    band = (rel >= -n_side) & (rel <= n_side)
    lane = lax.broadcasted_iota(jnp.int32, (ATTN_QB, 2 * HEAD_DIM), 1)
    low = lane < HEAD_DIM
    scale = 1.0 / math.sqrt(HEAD_DIM)

    def block(ib, carry):
        r0 = pl.multiple_of(ib * ATTN_QB, ATTN_QB)
        q0 = pl.multiple_of(qt * tq + r0, ATTN_QB)
        kpos = q0 + kk - halo
        mask = band & (kpos >= 0) & (kpos < m_len)
        for hp in range(HEADS_PER_GROUP // 2):
            cols = pl.ds(hp * 2 * HEAD_DIM, 2 * HEAD_DIM)
            qp = q_ref[pl.ds(r0, ATTN_QB), cols]
            kp = ks_ref[pl.ds(q0, 2 * ATTN_QB), cols]
            vp = vs_ref[pl.ds(q0, 2 * ATTN_QB), cols]
            outs, lses = [], []
            for hh in range(2):
                sel = low if hh == 0 else jnp.logical_not(low)
                qm = jnp.where(sel, qp, jnp.zeros_like(qp))
                s = lax.dot_general(qm, kp, (((1,), (1,)), ((), ())), preferred_element_type=F32) * scale
                s = jnp.where(mask, s + bias_ref[2 * hp + hh], NEG_INF)
                mx = jnp.max(s, axis=-1, keepdims=True)
                p = jnp.exp(s - mx)
                den = jnp.sum(p, axis=-1, keepdims=True)
                pv = jnp.dot(p.astype(BF16), vp, preferred_element_type=F32)
                outs.append(pv / den)
                lses.append(mx + jnp.log(den))
            o_ref[pl.ds(r0, ATTN_QB), cols] = jnp.where(low, outs[0], outs[1])
            l_ref[pl.ds(r0, ATTN_QB), cols] = jnp.where(low, lses[0], lses[1])
        return carry
    lax.fori_loop(0, tq // ATTN_QB, block, 0)


def dilated_group_attention(qkv, g, dil, n_side, bias_tiles):
    nb, d, m_len, _ = qkv.shape
    width = HEADS_PER_GROUP * HEAD_DIM
    assert d == dil and m_len % ATTN_QB == 0 and n_side <= ATTN_QB and n_side % 16 == 0
    tq = min(512, m_len)
    kernel = functools.partial(_attn_kernel, m_len=m_len, n_side=n_side, tq=tq)
    return pl.pallas_call(
        kernel,
        grid=(nb, dil, m_len // tq),
        in_specs=[
            pl.BlockSpec((None, None, tq, width), lambda b, r, t: (b, r, t, 0)),
            pl.BlockSpec((None, None, m_len, width), lambda b, r, t: (b, r, 0, 1)),
            pl.BlockSpec((None, None, m_len, width), lambda b, r, t: (b, r, 0, 2)),
            pl.BlockSpec((None, HEADS_PER_GROUP, ATTN_QB, 2 * ATTN_QB), lambda b, r, t: (g, 0, 0, 0)),
        ],
        out_specs=[
            pl.BlockSpec((None, None, tq, width), lambda b, r, t: (b, r, t, 0)),
            pl.BlockSpec((None, None, tq, width), lambda b, r, t: (b, r, t, 0)),
        ],
        out_shape=[
            jax.ShapeDtypeStruct((nb, dil, m_len, width), F32),
            jax.ShapeDtypeStruct((nb, dil, m_len, width), F32),
        ],
        scratch_shapes=[
            pltpu.VMEM((m_len + 2 * ATTN_QB, width), BF16),
            pltpu.VMEM((m_len + 2 * ATTN_QB, width), BF16),
        ],
        compiler_params=_cparams(("arbitrary", "arbitrary", "arbitrary")),
        name=f"dilated_attention_g{g}",
    )(qkv, qkv, qkv, bias_tiles)


def _merge_kernel(yh_ref, *rest, dils):
    ng = len(dils)
    o_refs, l_refs = rest[:ng], rest[ng:2 * ng]
    gh_ref, ga_ref, bh_ref, ba_ref, wh_ref, wa_ref, out_ref, ya_ref, tok_ref = rest[2 * ng:]
    tm, aw = ya_ref.shape
    nslab = aw // LANES

    @pl.when(pl.program_id(1) == 0)
    def _():
        for gi, d in enumerate(dils):
            if d == 1:
                continue
            rows = tm // d
            for r in range(d):
                for s in range(nslab):
                    cols = slice(s * LANES, (s + 1) * LANES)
                    tok_ref[2 * gi, s, pl.ds(r, rows, stride=d), :] = o_refs[gi][r, :, cols]
                    tok_ref[2 * gi + 1, s, pl.ds(r, rows, stride=d), :] = l_refs[gi][r, :, cols]
        for s in range(nslab):
            cols = slice(s * LANES, (s + 1) * LANES)
            os_, ls_ = [], []
            for gi, d in enumerate(dils):
                if d == 1:
                    os_.append(o_refs[gi][0, :, cols])
                    ls_.append(l_refs[gi][0, :, cols])
                else:
                    os_.append(tok_ref[2 * gi, s])
                    ls_.append(tok_ref[2 * gi + 1, s])
            mx = functools.reduce(jnp.maximum, ls_)
            es = [jnp.exp(l - mx) for l in ls_]
            inv = 1.0 / functools.reduce(lambda a, b: a + b, es)
            ya = functools.reduce(lambda a, b: a + b, [(e * inv) * o for e, o in zip(es, os_)])
            ya_ref[:, cols] = ya.astype(BF16)

    acc_h = jnp.dot(yh_ref[...], wh_ref[...], preferred_element_type=F32)
    acc_a = jnp.dot(ya_ref[...], wa_ref[...], preferred_element_type=F32)
    sig = lambda v: 1.0 / (1.0 + jnp.exp(-v))
    g_h = sig(gh_ref[...].astype(F32) + bh_ref[...])
    g_a = sig(ga_ref[...].astype(F32) + ba_ref[...])
    out_ref[...] = (g_h * acc_h + g_a * acc_a).astype(out_ref.dtype)


def branch_merge(y_hy, outs, lses, dils, u2, gate_col0, b_gate, w_hy, w_at, seq_len, tm=512):
    t, hw = y_hy.shape
    d = w_hy.shape[1]
    aw = w_at.shape[0]
    tn = _pick_tile(math.gcd(gate_col0, d), 1024)
    gb = gate_col0 // tn
    nj = d // tn
    tps = seq_len // tm
    assert all(tm % (dd * SUBLANES) == 0 for dd in dils)
    grp = [pl.BlockSpec((None, dd, tm // dd, aw), lambda i, j: (i // tps, 0, i % tps, 0)) for dd in dils]
    return pl.pallas_call(
        functools.partial(_merge_kernel, dils=tuple(dils)),
        grid=(t // tm, nj),
        in_specs=[pl.BlockSpec((tm, hw), lambda i, j: (i, 0))] + grp + grp + [
            pl.BlockSpec((tm, tn), lambda i, j: (i, gb + j)),
            pl.BlockSpec((tm, tn), lambda i, j: (i, gb + nj + j)),
            pl.BlockSpec((1, tn), lambda i, j: (0, j)),
            pl.BlockSpec((1, tn), lambda i, j: (0, nj + j)),
            pl.BlockSpec((hw, tn), lambda i, j: (0, j)),
            pl.BlockSpec((aw, tn), lambda i, j: (0, j)),
        ],
        out_specs=pl.BlockSpec((tm, tn), lambda i, j: (i, j)),
        out_shape=jax.ShapeDtypeStruct((t, d), BF16),
        scratch_shapes=[pltpu.VMEM((tm, aw), BF16), pltpu.VMEM((2 * len(dils), aw // LANES, tm, LANES), F32)],
        compiler_params=_cparams(("arbitrary", "arbitrary")),
        name="branch_merge",
    )(y_hy, *outs, *lses, u2, u2, b_gate, b_gate, w_hy, w_at)


def _out_proj_kernel(x_ref, m_ref, gate_ref, w_ref, o_ref):
    acc = jnp.dot(m_ref[...], w_ref[...], preferred_element_type=F32)
    o_ref[...] = x_ref[...] + gate_ref[0] * acc


def out_proj_residual(x, m, mod3, gate_idx, w, seq_len, tm=512):
    t, d = x.shape
    k = m.shape[1]
    tiles_per_seq = seq_len // tm
    return pl.pallas_call(
        _out_proj_kernel,
        grid=(t // tm,),
        in_specs=[
            pl.BlockSpec((tm, d), lambda i: (i, 0)),
            pl.BlockSpec((tm, k), lambda i: (i, 0)),
            pl.BlockSpec((1, 1, d), lambda i: (i // tiles_per_seq, 0, gate_idx)),
            pl.BlockSpec((k, d), lambda i: (0, 0)),
        ],
        out_specs=pl.BlockSpec((tm, d), lambda i: (i, 0)),
        out_shape=jax.ShapeDtypeStruct((t, d), F32),
        input_output_aliases={0: 0},
        compiler_params=_cparams(("arbitrary",)),
        name="out_proj_residual",
    )(x, m, mod3, w)


def _ffn_down_kernel(x_ref, a_ref, gt_ref, gprev_ref, gnext_ref, cw_ref, cb_ref, gate_ref, w_ref,
                     o_ref, acc_ref, act_ref, *, tiles_per_seq, halo_rows, nk, n_steps):
    s = pl.program_id(0)

    @pl.when(s == 0)
    def _():
        act_ref[1] = jnp.zeros(act_ref.shape[1:], act_ref.dtype)
        acc_ref[...] = jnp.zeros(acc_ref.shape, acc_ref.dtype)

    i_c = jnp.minimum(s, n_steps - 1) // nk
    first = (i_c % tiles_per_seq) == 0
    last = (i_c % tiles_per_seq) == tiles_per_seq - 1
    prev_row = jnp.where(first, 0.0, gprev_ref[halo_rows - 1:halo_rows, :].astype(F32))
    next_row = jnp.where(last, 0.0, gnext_ref[0:1, :].astype(F32))
    gt = _dwconv3_rows(gt_ref[...].astype(F32), cw_ref, cb_ref, prev_row, next_row)
    act = (gt * (1.0 / (1.0 + jnp.exp(-gt))) * a_ref[...].astype(F32)).astype(BF16)

    k_p = jnp.maximum(s - 1, 0) % nk
    part = jnp.dot(act_ref[(s + 1) % 2], w_ref[...], preferred_element_type=F32)
    acc_ref[...] = jnp.where(k_p == 0, part, acc_ref[...] + part)
    act_ref[s % 2] = act

    @pl.when((s > 0) & (k_p == nk - 1))
    def _():
        o_ref[...] = x_ref[...] + gate_ref[0] * acc_ref[...]


def ffn_down_residual(x, ag, conv_w, conv_b, mod3, gate_idx, w, seq_len, tm=512):
    t, d = x.shape
    ff = w.shape[0]
    tk = _pick_tile(ff, 1536)
    nk = ff // tk
    halo_rows = 16
    tiles_per_seq = seq_len // tm
    rb = tm // halo_rows
    last_rb = t // halo_rows - 1
    n_steps = (t // tm) * nk
    kernel = functools.partial(_ffn_down_kernel, tiles_per_seq=tiles_per_seq, halo_rows=halo_rows,
                               nk=nk, n_steps=n_steps)
    cur_i = lambda s: jnp.minimum(s, n_steps - 1) // nk
    cur_k = lambda s: jnp.minimum(s, n_steps - 1) % nk
    prv_i = lambda s: jnp.maximum(s - 1, 0) // nk
    prv_k = lambda s: jnp.maximum(s - 1, 0) % nk
    return pl.pallas_call(
        kernel,
        grid=(n_steps + 1,),
        in_specs=[
            pl.BlockSpec((tm, d), lambda s: (prv_i(s), 0)),
            pl.BlockSpec((tm, tk), lambda s: (cur_i(s), cur_k(s))),
            pl.BlockSpec((tm, tk), lambda s: (cur_i(s), nk + cur_k(s))),
            pl.BlockSpec((halo_rows, tk), lambda s: (jnp.maximum(cur_i(s) * rb - 1, 0), nk + cur_k(s))),
            pl.BlockSpec((halo_rows, tk), lambda s: (jnp.minimum((cur_i(s) + 1) * rb, last_rb), nk + cur_k(s))),
            pl.BlockSpec((3, tk), lambda s: (0, cur_k(s))),
            pl.BlockSpec((1, tk), lambda s: (0, cur_k(s))),
            pl.BlockSpec((1, 1, d), lambda s: (prv_i(s) // tiles_per_seq, 0, gate_idx)),
            pl.BlockSpec((tk, d), lambda s: (prv_k(s), 0)),
        ],
        out_specs=pl.BlockSpec((tm, d), lambda s: (prv_i(s), 0)),
        out_shape=jax.ShapeDtypeStruct((t, d), F32),
        scratch_shapes=[pltpu.VMEM((tm, d), F32), pltpu.VMEM((2, tm, tk), BF16)],
        input_output_aliases={0: 0},
        compiler_params=_cparams(("arbitrary",)),
        name="ffn_down_residual",
    )(x, ag, ag, ag, ag, conv_w, conv_b, mod3, w)


def _final_norm_kernel(x_ref, g_ref, o_ref):
    x = x_ref[...]
    ms = jnp.mean(x * x, axis=-1, keepdims=True)
    o_ref[...] = x * lax.rsqrt(ms + NORM_EPS) * g_ref[...]


def final_norm(x, g, row0, n_rows, tm=512):
    d = x.shape[1]
    off = row0 // tm
    return pl.pallas_call(
        _final_norm_kernel,
        grid=(n_rows // tm,),
        in_specs=[pl.BlockSpec((tm, d), lambda i: (off + i, 0)), pl.BlockSpec((1, d), lambda i: (0, 0))],
        out_specs=pl.BlockSpec((tm, d), lambda i: (i, 0)),
        out_shape=jax.ShapeDtypeStruct((n_rows, d), F32),
        compiler_params=_cparams(("arbitrary",)),
        name="final_norm",
    )(x, g.reshape(1, d))


def kernel(x_prompt, x_sample, c_prompt, c_sample, ada_w, ada_b, norm1_g, w_in, b_gate, hy_conv_w, hy_conv_b,
           filt_w1, filt_b1, filt_freq1, filt_w2, filt_b2, filt_freq2, filt_w3, hy_bias, rel_bias, w_br_hy,
           w_br_attn, w_out, norm2_g, ffn_up, ffn_conv_w, ffn_conv_b, ffn_down, final_g):
    bp, L, d = x_prompt.shape
    bs = x_sample.shape[0]
    assert x_sample.shape[1] == L
    nb = bp + bs
    depth = ada_w.shape[0]
    hw = hy_bias.shape[2]
    n_groups = len(ATTN_GROUPS)
    attn_w = n_groups * HEADS_PER_GROUP * HEAD_DIM
    hy_cols = 3 * hw
    gate_col0 = hy_cols + 3 * attn_w
    assert w_in.shape[2] == gate_col0 + 2 * d

    x = jnp.concatenate([x_prompt, x_sample], axis=0).reshape(nb * L, d)
    nb_pad = -(-nb // SUBLANES) * SUBLANES
    c_pad = jnp.zeros((nb_pad, d), F32).at[:nb].set(jnp.concatenate([c_prompt, c_sample], axis=0))
    mod = ada_modulation(c_pad, ada_w, ada_b)

    hy_ct = min(HYENA_CT, hw)
    khat = hyena_filter_spectra(L, filt_w1, filt_b1, filt_freq1, filt_w2, filt_b2, filt_freq2, filt_w3, hy_ct)
    n_sides = [(window // 2) // dil for window, dil in ATTN_GROUPS]
    bias_tiles = attention_bias_tiles(rel_bias, n_sides)

    dils = [dil for _, dil in ATTN_GROUPS]
    gw = HEADS_PER_GROUP * HEAD_DIM

    def group_cols(w, g):
        return [w[:, hy_cols + part * attn_w + g * gw: hy_cols + part * attn_w + (g + 1) * gw] for part in range(3)]

    for l in range(depth):
        mod3 = mod[l].reshape(nb_pad, 1, N_MOD * d)
        w_l = w_in[l]
        w_main = jnp.concatenate([w_l[:, :hy_cols], w_l[:, gate_col0:]], axis=1).astype(BF16)
        w_groups = jnp.concatenate([c for g in range(n_groups) for c in group_cols(w_l, g)], axis=1).astype(BF16)
        u, qkv = in_projection(x, norm1_g[l], mod3, 1, 0, w_main, w_groups, dils, nb, L)
        u3 = u.reshape(nb, L, u.shape[1])
        y_hy = hyena_mix(u3, 0, hy_conv_w[l], hy_conv_b[l].reshape(1, hy_cols), khat, l, hy_bias[l], hy_ct)
        outs, lses = [], []
        for g, dil in enumerate(dils):
            o_g, l_g = dilated_group_attention(qkv[g], g, dil, n_sides[g], bias_tiles)
            outs.append(o_g)
            lses.append(l_g)
        merged = branch_merge(y_hy.reshape(nb * L, hw), outs, lses, dils, u, hy_cols, b_gate[l].reshape(1, 2 * d),
                              w_br_hy[l].astype(BF16), w_br_attn[l].astype(BF16), L)
        x = out_proj_residual(x, merged, mod3, 2, w_out[l].astype(BF16), L)
        ag = normmod_matmul(x, norm2_g[l], mod3, 4, 3, ffn_up[l].astype(BF16), L)
        ff = ffn_down.shape[1]
        x = ffn_down_residual(x, ag, ffn_conv_w[l], ffn_conv_b[l].reshape(1, ff), mod3, 5,
                              ffn_down[l].astype(BF16), L)

    y_prompt = final_norm(x, final_g, 0, bp * L).reshape(bp, L, d)
    y_sample = final_norm(x, final_g, bp * L, bs * L).reshape(bs, L, d)
    return (y_prompt, y_sample)
```

```python
import functools
import math

import jax
import jax.numpy as jnp
import numpy as np
from jax import lax
from jax.experimental import pallas as pl
from jax.experimental.pallas import tpu as pltpu

F32 = jnp.float32
BF16 = jnp.bfloat16
HIGHEST = lax.Precision.HIGHEST

NORM_EPS = 1e-6
N_MOD = 6
HYENA_ORDER = 2
N_BANDS = 16
FAST_DECAY_PCT = 0.3
SLOW_DECAY_PCT = 1.5
DECAY_TARGET = 1e-2
ATTN_GROUPS = ((128, 1), (512, 4), (2048, 16))
HEADS_PER_GROUP = 8
HEAD_DIM = 64
REL_BUCKETS = 32
REL_MAX_DIST = 1024
NEG_INF = -1e30

LANES = 128
SUBLANES = 8
VMEM_LIMIT = 56 * 1024 * 1024

FFT_N2 = 128
FFT_GROUP = SUBLANES
HYENA_CT = 256
ATTN_QB = 128


def _cparams(sem):
    return pltpu.CompilerParams(dimension_semantics=sem, vmem_limit_bytes=VMEM_LIMIT)


def _pick_tile(n, cap):
    best = None
    for t in range(LANES, min(n, cap) + 1, LANES):
        if n % t == 0:
            best = t
    assert best is not None, (n, cap)
    return best


def _ada_kernel(c_ref, w_ref, b_ref, o_ref):
    c = c_ref[...]
    cs = c * (1.0 / (1.0 + jnp.exp(-c)))
    o_ref[0] = jnp.dot(cs, w_ref[0], preferred_element_type=F32, precision=HIGHEST) + b_ref[0]


def ada_modulation(c_pad, ada_w, ada_b):
    depth, d, n = ada_w.shape
    nb = c_pad.shape[0]
    tn = _pick_tile(n, 1024)
    return pl.pallas_call(
        _ada_kernel,
        grid=(depth, n // tn),
        in_specs=[
            pl.BlockSpec((nb, d), lambda l, j: (0, 0)),
            pl.BlockSpec((1, d, tn), lambda l, j: (l, 0, j)),
            pl.BlockSpec((1, 1, tn), lambda l, j: (l, 0, j)),
        ],
        out_specs=pl.BlockSpec((1, nb, tn), lambda l, j: (l, 0, j)),
        out_shape=jax.ShapeDtypeStruct((depth, nb, n), F32),
        compiler_params=_cparams(("arbitrary", "arbitrary")),
        name="ada_modulation",
    )(c_pad, ada_w, ada_b.reshape(depth, 1, n))


def _normmod_rows(x, g, sc, sh):
    ms = jnp.mean(x * x, axis=-1, keepdims=True)
    y = x * lax.rsqrt(ms + NORM_EPS) * g
    return y * (1.0 + sc) + sh


FFN_HALO = 16


def _ffn_up_act_kernel(x_ref, xp_ref, xn_ref, g_ref, sc_ref, sh_ref, w_ref, cw_ref, cb_ref,
                       o_ref, h_ref, *, tiles_per_seq):
    i = pl.program_id(0)
    tm = x_ref.shape[0]

    @pl.when(pl.program_id(1) == 0)
    def _():
        g, sc, sh = g_ref[...], sc_ref[0], sh_ref[0]
        h_ref[0:FFN_HALO] = _normmod_rows(xp_ref[...], g, sc, sh).astype(BF16)
        h_ref[FFN_HALO:FFN_HALO + tm] = _normmod_rows(x_ref[...], g, sc, sh).astype(BF16)
        h_ref[FFN_HALO + tm:] = _normmod_rows(xn_ref[...], g, sc, sh).astype(BF16)

    first = (i % tiles_per_seq) == 0
    last = (i % tiles_per_seq) == tiles_per_seq - 1
    row = lax.broadcasted_iota(jnp.int32, (tm + 2 * FFN_HALO, 1), 0)
    outside = ((row == FFN_HALO - 1) & first) | ((row == FFN_HALO + tm) & last)
    for c in range(o_ref.shape[1] // LANES):
        res = jnp.dot(h_ref[...], w_ref[:, 2 * c * LANES:2 * (c + 1) * LANES], preferred_element_type=F32)
        a = res[FFN_HALO:FFN_HALO + tm, :LANES]
        gt = jnp.where(outside, 0.0, res[:, LANES:])
        cols = slice(c * LANES, (c + 1) * LANES)
        conv = _dwconv3_rows(gt, cw_ref.at[:, cols], cb_ref.at[:, cols])[FFN_HALO:FFN_HALO + tm]
        o_ref[:, cols] = (conv * (1.0 / (1.0 + jnp.exp(-conv))) * a).astype(o_ref.dtype)


def interleave_ffn_up(w_up):
    d, two_ff = w_up.shape
    ff = two_ff // 2
    return w_up.reshape(d, 2, ff // LANES, LANES).transpose(0, 2, 1, 3).reshape(d, two_ff)


def ffn_up_act(x, g, mod3, sc_idx, sh_idx, w_il, conv_w, conv_b, seq_len, tm=512, tn_cap=1536):
    t, d = x.shape
    ff = w_il.shape[1] // 2
    tn = _pick_tile(ff, tn_cap)
    nj = ff // tn
    tps = seq_len // tm
    rb = tm // FFN_HALO
    last_rb = t // FFN_HALO - 1
    return pl.pallas_call(
        functools.partial(_ffn_up_act_kernel, tiles_per_seq=tps),
        grid=(t // tm, nj),
        in_specs=[
            pl.BlockSpec((tm, d), lambda i, j: (i, 0)),
            pl.BlockSpec((FFN_HALO, d), lambda i, j: (jnp.maximum(i * rb - 1, 0), 0)),
            pl.BlockSpec((FFN_HALO, d), lambda i, j: (jnp.minimum((i + 1) * rb, last_rb), 0)),
            pl.BlockSpec((1, d), lambda i, j: (0, 0)),
            pl.BlockSpec((1, 1, d), lambda i, j: (i // tps, 0, sc_idx)),
            pl.BlockSpec((1, 1, d), lambda i, j: (i // tps, 0, sh_idx)),
            pl.BlockSpec((d, 2 * tn), lambda i, j: (0, j)),
            pl.BlockSpec((3, tn), lambda i, j: (0, j)),
            pl.BlockSpec((1, tn), lambda i, j: (0, j)),
        ],
        out_specs=pl.BlockSpec((tm, tn), lambda i, j: (i, j)),
        out_shape=jax.ShapeDtypeStruct((t, ff), BF16),
        scratch_shapes=[pltpu.VMEM((tm + 2 * FFN_HALO, d), BF16)],
        compiler_params=_cparams(("arbitrary", "arbitrary")),
        name="ffn_up_act",
    )(x, x, x, g.reshape(1, d), mod3, mod3, w_il, conv_w, conv_b)


def _in_proj_kernel(x_ref, g_ref, sc_ref, sh_ref, wm_ref, wg_ref, om_ref, *rest, n_main, dils, chunk):
    og_refs = rest[:len(dils)]
    h_ref, perm_ref = rest[len(dils):]
    j = pl.program_id(1)
    tm = h_ref.shape[0]

    @pl.when(j == 0)
    def _():
        h_ref[...] = _normmod_rows(x_ref[...], g_ref[...], sc_ref[0], sh_ref[0]).astype(BF16)

    @pl.when(j < n_main)
    def _():
        om_ref[...] = jnp.dot(h_ref[...], wm_ref[...], preferred_element_type=F32).astype(om_ref.dtype)

    for gi, (o_ref, d) in enumerate(zip(og_refs, dils)):
        @pl.when(j == n_main + gi)
        def _():
            rows = tm // d
            for c0 in range(0, wg_ref.shape[1], chunk):
                res = jnp.dot(h_ref[...], wg_ref[:, c0:c0 + chunk], preferred_element_type=F32)
                if d == 1:
                    o_ref[0, 0, :, c0:c0 + chunk] = res.astype(o_ref.dtype)
                    continue
                for s in range(chunk // LANES):
                    perm_ref[s] = res[:, s * LANES:(s + 1) * LANES]
                for r in range(d):
                    for s in range(chunk // LANES):
                        cols = slice(c0 + s * LANES, c0 + (s + 1) * LANES)
                        o_ref[0, r, :, cols] = perm_ref[s, pl.ds(r, rows, stride=d), :].astype(o_ref.dtype)


def in_projection(x, g, mod3, sc_idx, sh_idx, w_main, w_groups, dils, nb, seq_len, tm=512, tn_cap=1792):
    t, d_model = x.shape
    n_main = w_main.shape[1]
    gw = w_groups.shape[1] // len(dils)
    tn = _pick_tile(n_main, tn_cap)
    nj_main = n_main // tn
    tps = seq_len // tm
    chunk = gw // 3
    assert chunk % LANES == 0 and all(tm % (dd * 16) == 0 for dd in dils)
    kernel = functools.partial(_in_proj_kernel, n_main=nj_main, dils=tuple(dils), chunk=chunk)
    outs = pl.pallas_call(
        kernel,
        grid=(t // tm, nj_main + len(dils)),
        in_specs=[
            pl.BlockSpec((tm, d_model), lambda i, j: (i, 0)),
            pl.BlockSpec((1, d_model), lambda i, j: (0, 0)),
            pl.BlockSpec((1, 1, d_model), lambda i, j: (i // tps, 0, sc_idx)),
            pl.BlockSpec((1, 1, d_model), lambda i, j: (i // tps, 0, sh_idx)),
            pl.BlockSpec((d_model, tn), lambda i, j: (0, jnp.minimum(j, nj_main - 1))),
            pl.BlockSpec((d_model, gw), lambda i, j: (0, jnp.clip(j - nj_main, 0, len(dils) - 1))),
        ],
        out_specs=[pl.BlockSpec((tm, tn), lambda i, j: (i, jnp.minimum(j, nj_main - 1)))] + [
            pl.BlockSpec((1, dd, tm // dd, gw), lambda i, j: (i // tps, 0, i % tps, 0)) for dd in dils],
        out_shape=[jax.ShapeDtypeStruct((t, n_main), BF16)] + [
            jax.ShapeDtypeStruct((nb, dd, seq_len // dd, gw), BF16) for dd in dils],
        scratch_shapes=[pltpu.VMEM((tm, d_model), BF16), pltpu.VMEM((chunk // LANES, tm, LANES), F32)],
        compiler_params=_cparams(("arbitrary", "arbitrary")),
        name="in_projection",
    )(x, g.reshape(1, d_model), mod3, mod3, w_main, w_groups)
    return outs[0], list(outs[1:])


def _fft_dims(seq_len):
    n_fft = 2 * seq_len
    n1 = n_fft // FFT_N2
    assert n1 * FFT_N2 == n_fft and n1 % 2 == 0
    nk1 = n1 // 2 + 1
    pad = -(-nk1 // SUBLANES) * SUBLANES
    return n_fft, n1, nk1, pad


@functools.lru_cache(maxsize=None)
def _fft_constants(seq_len):
    n_fft, n1, nk1, pad = _fft_dims(seq_len)
    n2 = FFT_N2
    k1 = np.arange(nk1)[None, :, None]
    i1 = np.arange(n1)[None, None, :]
    i2 = np.arange(n2)[:, None, None]
    ph = 2.0 * np.pi * ((k1 * (i1 * n2 + i2)) % n_fft) / n_fft
    f1 = np.zeros((n2, 2 * pad, n1), np.float64)
    f1[:, :nk1] = np.cos(ph)
    f1[:, pad:pad + nk1] = -np.sin(ph)
    ck = np.full((nk1,), 2.0)
    ck[0] = 1.0
    ck[-1] = 1.0
    g1 = np.zeros((n2, n1 // 2, 2 * pad), np.float64)
    pht = np.transpose(ph[:, :, :n1 // 2], (0, 2, 1))
    g1[:, :, :nk1] = np.cos(pht) * ck / n_fft
    g1[:, :, pad:pad + nk1] = -np.sin(pht) * ck / n_fft
    a = np.arange(n2)
    ph2 = 2.0 * np.pi * ((a[:, None] * a[None, :]) % n2) / n2
    c2, s2 = np.cos(ph2), np.sin(ph2)
    f2 = np.block([[c2, s2], [-s2, c2]])
    g2 = np.block([[c2, -s2], [s2, c2]])
    return (f1.astype(np.float32), f2.astype(np.float32), g2.astype(np.float32), g1.astype(np.float32))


def _interleaved_block_diag(mats):
    n, r, c = mats.shape
    out = np.zeros((n // FFT_GROUP, r * FFT_GROUP, c * FFT_GROUP), mats.dtype)
    for j in range(FFT_GROUP):
        out[:, j::FFT_GROUP, j::FFT_GROUP] = mats[j::FFT_GROUP]
    return out


def _group_rows(g):
    return pl.ds(pl.multiple_of(g * FFT_GROUP, FFT_GROUP), FFT_GROUP)


def _ld_group(ref, g):
    n_outer = ref.shape[1]
    return jnp.concatenate([ref[h, :, _group_rows(g), :].reshape(n_outer * FFT_GROUP, LANES)
                            for h in range(ref.shape[0])], axis=1)


def _st_group(ref, g, val):
    n_outer = ref.shape[1]
    for h in range(ref.shape[0]):
        ref[h, :, _group_rows(g), :] = val[:, h * LANES:(h + 1) * LANES].reshape(n_outer, FFT_GROUP, LANES)


def _ld_outer(ref, idx):
    return jnp.concatenate([ref[h, idx] for h in range(ref.shape[0])], axis=1)


def _st_outer(ref, idx, val):
    for h in range(ref.shape[0]):
        ref[h, idx] = val[:, h * LANES:(h + 1) * LANES]


def _st_rows(ref, row0, val):
    n = val.shape[0]
    for h in range(ref.shape[0]):
        ref[h, row0 // FFT_N2:(row0 + n) // FFT_N2] = val[:, h * LANES:(h + 1) * LANES].reshape(
            n // FFT_N2, FFT_N2, LANES)


def _fft_stage1(src_ref, f1b_ref, a_ref):
    def body(g, carry):
        a = jnp.dot(f1b_ref[g], _ld_group(src_ref, g).astype(BF16), preferred_element_type=F32)
        _st_group(a_ref, g, a)
        return carry
    lax.fori_loop(0, FFT_N2 // FFT_GROUP, body, 0, unroll=2)


def _fft_stage2_rhs(a_ref, k1, pad):
    return jnp.concatenate([_ld_outer(a_ref, k1), _ld_outer(a_ref, pad + k1)], axis=0).astype(BF16)


def _filter_kernel(feat_ref, featr_ref, w1_ref, b1_ref, fr1_ref, w2_ref, b2_ref, fr2_ref,
                   w3f_ref, w3b_ref, delta_ref, f1_ref, f2_ref,
                   o_ref, hid_ref, k2_ref, a_ref, *, seq_len):
    n_fft, n1, nk1, pad = _fft_dims(seq_len)
    L = seq_len

    @pl.when((pl.program_id(1) == 0) & (pl.program_id(2) == 0))
    def _():
        for d, fref in enumerate((feat_ref, featr_ref)):
            h = jnp.dot(fref[...], w1_ref[0], preferred_element_type=F32, precision=HIGHEST) + b1_ref[0]
            h = jnp.sin(fr1_ref[0] * h)
            h = jnp.dot(h, w2_ref[0], preferred_element_type=F32, precision=HIGHEST) + b2_ref[0]
            hid_ref[d] = jnp.sin(fr2_ref[0] * h)

    delta = delta_ref[...]
    chunk = min(L, 512)
    row = lax.broadcasted_iota(jnp.int32, (chunk, 1), 0)
    l1 = jnp.zeros(delta.shape, F32)
    for d, (w3_ref, fref) in enumerate(((w3f_ref, feat_ref), (w3b_ref, featr_ref))):
        for c in range(L // chunk):
            rows = pl.ds(c * chunk, chunk)
            h = jnp.dot(hid_ref[d, rows, :], w3_ref[0], preferred_element_type=F32, precision=HIGHEST)
            h = h * jnp.exp(-fref[rows, 0:1] * delta)
            if d == 1 and c == 0:
                h = jnp.where(row == 0, 0.0, h)
            l1 = l1 + jnp.sum(jnp.abs(h), axis=0, keepdims=True)
            _st_rows(k2_ref, d * L + c * chunk, h)
    inv = 1.0 / l1
    _fft_stage1(k2_ref, f1_ref, a_ref)

    def body(k1, carry):
        spec = jnp.dot(f2_ref[...], _fft_stage2_rhs(a_ref, k1, pad), preferred_element_type=F32) * inv
        o_ref[0, 0, k1] = spec.astype(o_ref.dtype)
        return carry
    lax.fori_loop(0, nk1, body, 0, unroll=3 if nk1 % 3 == 0 else 1)


def hyena_filter_spectra(seq_len, filt_w1, filt_b1, filt_freq1, filt_w2, filt_b2, filt_freq2, filt_w3, ct):
    depth, n_feat, hid = filt_w1.shape
    hw = filt_w3.shape[2] // (2 * HYENA_ORDER)
    n_fft, n1, nk1, pad = _fft_dims(seq_len)
    L = seq_len
    t = np.linspace(0.0, 1.0, L, dtype=np.float32).astype(np.float64)[:, None]
    pos = np.arange(L, dtype=np.float64)[:, None]
    bands = np.linspace(1e-4, N_BANDS - 1, N_BANDS, dtype=np.float32).astype(np.float64)[None, :]
    ang = (2.0 * math.pi / L) * pos * bands
    feat = np.concatenate([t, np.cos(ang), -np.sin(ang)], axis=-1)
    assert feat.shape[1] == n_feat
    feat_rev = np.concatenate([np.zeros((1, n_feat)), feat[:0:-1]], axis=0)
    n_feat_pad = -(-n_feat // LANES) * LANES
    feat = np.pad(feat, ((0, 0), (0, n_feat_pad - n_feat)))
    feat_rev = np.pad(feat_rev, ((0, 0), (0, n_feat_pad - n_feat)))
    filt_w1 = jnp.pad(filt_w1, ((0, 0), (0, n_feat_pad - n_feat), (0, 0)))
    n_feat = n_feat_pad
    max_decay = math.log(DECAY_TARGET) / FAST_DECAY_PCT
    min_decay = math.log(DECAY_TARGET) / SLOW_DECAY_PCT
    deltas = np.abs(np.linspace(min_decay, max_decay, hw, dtype=np.float32))[None, :]
    f1, f2, _, _ = _fft_constants(seq_len)
    f1 = _interleaved_block_diag(f1)
    nct = hw // ct
    nslab = ct // LANES
    assert L % FFT_N2 == 0 and (L < 512 or L % 512 == 0)

    def w3_spec(direction):
        return pl.BlockSpec((1, hid, ct), lambda l, c, o: (l, 0, (direction * HYENA_ORDER + o) * nct + c))

    def const(shape):
        return pl.BlockSpec(shape, lambda l, c, o: (0,) * len(shape), pipeline_mode=pl.Buffered(1))

    vec = lambda a: a.reshape(depth, 1, hid)
    vspec = pl.BlockSpec((1, 1, hid), lambda l, c, o: (l, 0, 0))
    return pl.pallas_call(
        functools.partial(_filter_kernel, seq_len=seq_len),
        grid=(depth, nct, HYENA_ORDER),
        in_specs=[
            const((L, n_feat)), const((L, n_feat)),
            pl.BlockSpec((1, n_feat, hid), lambda l, c, o: (l, 0, 0)), vspec, vspec,
            pl.BlockSpec((1, hid, hid), lambda l, c, o: (l, 0, 0)), vspec, vspec,
            w3_spec(0), w3_spec(1),
            pl.BlockSpec((1, ct), lambda l, c, o: (0, c)),
            const(f1.shape), const((2 * FFT_N2, 2 * FFT_N2)),
        ],
        out_specs=pl.BlockSpec((1, 1, nk1, 2 * FFT_N2, ct), lambda l, c, o: (l, o, 0, 0, c)),
        out_shape=jax.ShapeDtypeStruct((depth, HYENA_ORDER, nk1, 2 * FFT_N2, hw), BF16),
        scratch_shapes=[
            pltpu.VMEM((2, L, hid), F32),
            pltpu.VMEM((nslab, n1, FFT_N2, LANES), F32),
            pltpu.VMEM((nslab, 2 * pad, FFT_N2, LANES), F32),
        ],
        compiler_params=_cparams(("arbitrary", "arbitrary", "arbitrary")),
        name="hyena_filter_spectra",
    )(jnp.asarray(feat, F32), jnp.asarray(feat_rev, F32), filt_w1, vec(filt_b1), vec(filt_freq1),
      filt_w2, vec(filt_b2), vec(filt_freq2), filt_w3, filt_w3,
      jnp.asarray(deltas, F32), jnp.asarray(f1, BF16), jnp.asarray(f2, BF16))


def _dwconv3_rows(x, w_ref, b_ref, prev_row=None, next_row=None):
    n = x.shape[0]
    row = lax.broadcasted_iota(jnp.int32, (n, 1), 0)
    xm = pltpu.roll(x, 1, axis=0)
    xp = pltpu.roll(x, n - 1, axis=0)
    xm = jnp.where(row == 0, 0.0 if prev_row is None else prev_row, xm)
    xp = jnp.where(row == n - 1, 0.0 if next_row is None else next_row, xp)
    return xm * w_ref[0:1, :] + x * w_ref[1:2, :] + xp * w_ref[2:3, :] + b_ref[...]


def _dwconv3_to_slabs(src_ref, w_ref, b_ref, dst_ref, seq_len, chunk):
    halo = 16
    n_chunks = seq_len // chunk
    for c in range(n_chunks):
        r0 = c * chunk
        x = src_ref[0, r0:r0 + chunk, :].astype(F32)
        prev_row = src_ref[0, r0 - halo:r0, :].astype(F32)[halo - 1:halo] if c > 0 else None
        next_row = src_ref[0, r0 + chunk:r0 + chunk + halo, :].astype(F32)[0:1] if c < n_chunks - 1 else None
        _st_rows(dst_ref, r0, _dwconv3_rows(x, w_ref, b_ref, prev_row, next_row))


def _hyena_conv_kernel(zin_ref, ux_ref, wz_ref, bz_ref, wx_ref, bx_ref, hb_ref, kh_ref,
                       f1_ref, f2_ref, g2_ref, g1_ref, o_ref, z_ref, xg_ref, a_ref, *, seq_len, first):
    n_fft, n1, nk1, pad = _fft_dims(seq_len)
    L = seq_len
    half = n1 // 2
    nslab = z_ref.shape[0]
    chunk = min(L, 512)

    if first:
        _dwconv3_to_slabs(zin_ref, wz_ref, bz_ref, z_ref, L, chunk)
    else:
        for c in range(L // chunk):
            _st_rows(z_ref, c * chunk, zin_ref[0, c * chunk:(c + 1) * chunk, :])
    _dwconv3_to_slabs(ux_ref, wx_ref, bx_ref, xg_ref, L, chunk)

    _fft_stage1(z_ref, f1_ref, a_ref)

    def body2(k1, carry):
        x = jnp.dot(f2_ref[...], _fft_stage2_rhs(a_ref, k1, pad), preferred_element_type=F32)
        kk = kh_ref[0, 0, k1].astype(F32)
        xr, xi = x[:FFT_N2], x[FFT_N2:]
        kr, ki = kk[:FFT_N2], kk[FFT_N2:]
        y = jnp.concatenate([xr * kr - xi * ki, xr * ki + xi * kr], axis=0)
        bk = jnp.dot(g2_ref[...], y.astype(BF16), preferred_element_type=F32)
        _st_outer(a_ref, k1, bk[:FFT_N2])
        _st_outer(a_ref, pad + k1, bk[FFT_N2:])
        return carry
    lax.fori_loop(0, nk1, body2, 0, unroll=3 if nk1 % 3 == 0 else 1)

    bias = hb_ref[...]

    def body3(g, carry):
        y = jnp.dot(g1_ref[g], _ld_group(a_ref, g).astype(BF16), preferred_element_type=F32)
        _st_group(z_ref, g, _ld_group(xg_ref, g) * (y + bias * _ld_group(z_ref, g)))
        return carry
    lax.fori_loop(0, FFT_N2 // FFT_GROUP, body3, 0, unroll=2)

    for c in range(L // chunk):
        blk = jnp.concatenate([z_ref[h, c * chunk // FFT_N2:(c + 1) * chunk // FFT_N2].reshape(chunk, LANES)
                               for h in range(nslab)], axis=1)
        o_ref[0, c * chunk:(c + 1) * chunk, :] = blk.astype(o_ref.dtype)


def hyena_step(zin, zin_col0, u3, x_col0, w_cols, conv_w, conv_b, khat, layer, order, hy_bias_row, first, ct):
    nb, L, _ = u3.shape
    hw = hy_bias_row.shape[1]
    n_fft, n1, nk1, pad = _fft_dims(L)
    half = n1 // 2
    f1, f2, g2, g1 = _fft_constants(L)
    f1 = _interleaved_block_diag(f1[:, :, :half])
    g1 = _interleaved_block_diag(g1)
    nct = hw // ct
    nslab = ct // LANES
    assert zin_col0 % ct == 0 and x_col0 % ct == 0 and w_cols[0] % ct == 0 and w_cols[1] % ct == 0
    last = order == HYENA_ORDER - 1

    def col_spec(rows, col0):
        return pl.BlockSpec((rows, ct), lambda c, b: (0, col0 // ct + c))

    const = lambda shape: pl.BlockSpec(shape, lambda c, b: (0,) * len(shape), pipeline_mode=pl.Buffered(1))
    return pl.pallas_call(
        functools.partial(_hyena_conv_kernel, seq_len=L, first=first),
        grid=(nct, nb),
        in_specs=[
            pl.BlockSpec((1, L, ct), lambda c, b: (b, 0, zin_col0 // ct + c)),
            pl.BlockSpec((1, L, ct), lambda c, b: (b, 0, x_col0 // ct + c)),
            col_spec(3, w_cols[0]), col_spec(1, w_cols[0]), col_spec(3, w_cols[1]), col_spec(1, w_cols[1]),
            pl.BlockSpec((1, ct), lambda c, b: (0, c)),
            pl.BlockSpec((1, 1, nk1, 2 * FFT_N2, ct), lambda c, b: (layer, order, 0, 0, c),
                         pipeline_mode=pl.Buffered(1)),
            const(f1.shape), const((2 * FFT_N2, 2 * FFT_N2)),
            const((2 * FFT_N2, 2 * FFT_N2)), const(g1.shape),
        ],
        out_specs=pl.BlockSpec((1, L, ct), lambda c, b: (b, 0, c)),
        out_shape=jax.ShapeDtypeStruct((nb, L, hw), BF16 if last else F32),
        scratch_shapes=[
            pltpu.VMEM((nslab, half, FFT_N2, LANES), F32),
            pltpu.VMEM((nslab, half, FFT_N2, LANES), F32),
            pltpu.VMEM((nslab, 2 * pad, FFT_N2, LANES), F32),
        ],
        compiler_params=_cparams(("arbitrary", "arbitrary")),
        name=f"hyena_step{order}",
    )(zin, u3, conv_w, conv_b, conv_w, conv_b, hy_bias_row, khat,
      jnp.asarray(f1, BF16), jnp.asarray(f2, BF16), jnp.asarray(g2, BF16), jnp.asarray(g1, BF16))


def hyena_mix(u3, col0, conv_w, conv_b, khat, layer, hy_bias, ct):
    hw = hy_bias.shape[1]
    z = hyena_step(u3, col0, u3, col0 + hw, (0, hw), conv_w, conv_b, khat, layer, 0, hy_bias[0:1], True, ct)
    return hyena_step(z, 0, u3, col0 + 2 * hw, (0, 2 * hw), conv_w, conv_b, khat, layer, 1, hy_bias[1:2], False, ct)


def _t5_bucket(rel):
    nb = REL_BUCKETS // 2
    ret = (rel > 0).astype(np.int32) * nb
    n = np.abs(rel)
    max_exact = nb // 2
    large = max_exact + (np.log(np.maximum(n, 1) / max_exact) / np.log(REL_MAX_DIST / max_exact)
                         * (nb - max_exact)).astype(np.int32)
    large = np.minimum(large, nb - 1)
    return ret + np.where(n < max_exact, n, large)


def _bias_kernel(rb_ref, bkt_ref, o_ref):
    n_groups = bkt_ref.shape[0]
    for g in range(n_groups):
        bkt = bkt_ref[g]
        for h in range(HEADS_PER_GROUP):
            tile = jnp.zeros(bkt.shape, F32)
            for b in range(REL_BUCKETS):
                tile = jnp.where(bkt == b, rb_ref[b, g * HEADS_PER_GROUP + h], tile)
            o_ref[g, h] = tile


def attention_bias_tiles(rel_bias, n_side_list):
    n_groups = len(ATTN_GROUPS)
    qq = np.arange(ATTN_QB)[:, None]
    kk = np.arange(2 * ATTN_QB)[None, :]
    bkts = []
    for (window, dil), n_side in zip(ATTN_GROUPS, n_side_list):
        j = kk - n_side - qq
        bkts.append(_t5_bucket(j * dil))
    bkt = np.stack(bkts).astype(np.int32)
    return pl.pallas_call(
        _bias_kernel,
        in_specs=[pl.BlockSpec(memory_space=pltpu.SMEM), pl.BlockSpec(memory_space=pltpu.VMEM)],
        out_specs=pl.BlockSpec(memory_space=pltpu.VMEM),
        out_shape=jax.ShapeDtypeStruct((n_groups, HEADS_PER_GROUP, ATTN_QB, 2 * ATTN_QB), F32),
        name="attention_bias_tiles",
    )(rel_bias, jnp.asarray(bkt))


def _attn_kernel(q_ref, k_ref, v_ref, bias_ref, o_ref, l_ref, ks_ref, vs_ref, *, m_len, n_side, tq):
    qt = pl.program_id(2)
    width = HEADS_PER_GROUP * HEAD_DIM
    halo = n_side

    @pl.when(qt == 0)
    def _():
        zeros = jnp.zeros((halo, width), BF16)
        for ref, src in ((ks_ref, k_ref), (vs_ref, v_ref)):
            ref[pl.ds(0, halo), :] = zeros
            ref[pl.ds(halo + m_len, 2 * ATTN_QB - halo), :] = jnp.zeros((2 * ATTN_QB - halo, width), BF16)
            ref[pl.ds(halo, m_len), :] = src[...]

    qq = lax.broadcasted_iota(jnp.int32, (ATTN_QB, 2 * ATTN_QB), 0)
    kk = lax.broadcasted_iota(jnp.int32, (ATTN_QB, 2 * ATTN_QB), 1)
    rel = kk - halo - qq
    band = (rel >= -n_side) & (rel <= n_side)
    lane = lax.broadcasted_iota(jnp.int32, (ATTN_QB, 2 * HEAD_DIM), 1)
    low = lane < HEAD_DIM
    scale = 1.0 / math.sqrt(HEAD_DIM)

    def block(ib, carry):
        r0 = pl.multiple_of(ib * ATTN_QB, ATTN_QB)
        q0 = pl.multiple_of(qt * tq + r0, ATTN_QB)
        kpos = q0 + kk - halo
        mask = band & (kpos >= 0) & (kpos < m_len)
        for hp in range(HEADS_PER_GROUP // 2):
            cols = pl.ds(hp * 2 * HEAD_DIM, 2 * HEAD_DIM)
            qp = q_ref[pl.ds(r0, ATTN_QB), cols]
            kp = ks_ref[pl.ds(q0, 2 * ATTN_QB), cols]
            vp = vs_ref[pl.ds(q0, 2 * ATTN_QB), cols]
            outs, lses = [], []
            for hh in range(2):
                sel = low if hh == 0 else jnp.logical_not(low)
                qm = jnp.where(sel, qp, jnp.zeros_like(qp))
                s = lax.dot_general(qm, kp, (((1,), (1,)), ((), ())), preferred_element_type=F32) * scale
                s = jnp.where(mask, s + bias_ref[2 * hp + hh], NEG_INF)
                mx = jnp.max(s, axis=-1, keepdims=True)
                p = jnp.exp(s - mx)
                den = jnp.sum(p, axis=-1, keepdims=True)
                pv = jnp.dot(p.astype(BF16), vp, preferred_element_type=F32)
                outs.append(pv / den)
                lses.append(mx + jnp.log(den))
            o_ref[pl.ds(r0, ATTN_QB), cols] = jnp.where(low, outs[0], outs[1])
            l_ref[pl.ds(r0, ATTN_QB), cols] = jnp.where(low, lses[0], lses[1])
        return carry
    lax.fori_loop(0, tq // ATTN_QB, block, 0)


def dilated_group_attention(qkv, g, dil, n_side, bias_tiles):
    nb, d, m_len, _ = qkv.shape
    width = HEADS_PER_GROUP * HEAD_DIM
    assert d == dil and m_len % ATTN_QB == 0 and n_side <= ATTN_QB and n_side % 16 == 0
    tq = min(512, m_len)
    kernel = functools.partial(_attn_kernel, m_len=m_len, n_side=n_side, tq=tq)
    return pl.pallas_call(
        kernel,
        grid=(nb, dil, m_len // tq),
        in_specs=[
            pl.BlockSpec((None, None, tq, width), lambda b, r, t: (b, r, t, 0)),
            pl.BlockSpec((None, None, m_len, width), lambda b, r, t: (b, r, 0, 1)),
            pl.BlockSpec((None, None, m_len, width), lambda b, r, t: (b, r, 0, 2)),
            pl.BlockSpec((None, HEADS_PER_GROUP, ATTN_QB, 2 * ATTN_QB), lambda b, r, t: (g, 0, 0, 0)),
        ],
        out_specs=[
            pl.BlockSpec((None, None, tq, width), lambda b, r, t: (b, r, t, 0)),
            pl.BlockSpec((None, None, tq, width), lambda b, r, t: (b, r, t, 0)),
        ],
        out_shape=[
            jax.ShapeDtypeStruct((nb, dil, m_len, width), F32),
            jax.ShapeDtypeStruct((nb, dil, m_len, width), F32),
        ],
        scratch_shapes=[
            pltpu.VMEM((m_len + 2 * ATTN_QB, width), BF16),
            pltpu.VMEM((m_len + 2 * ATTN_QB, width), BF16),
        ],
        compiler_params=_cparams(("arbitrary", "arbitrary", "arbitrary")),
        name=f"dilated_attention_g{g}",
    )(qkv, qkv, qkv, bias_tiles)


def _merge_kernel(yh_ref, *rest, dils):
    ng = len(dils)
    o_refs, l_refs = rest[:ng], rest[ng:2 * ng]
    gh_ref, ga_ref, bh_ref, ba_ref, wh_ref, wa_ref, out_ref, ya_ref, tok_ref = rest[2 * ng:]
    tm, aw = ya_ref.shape
    nslab = aw // LANES

    @pl.when(pl.program_id(1) == 0)
    def _():
        for gi, d in enumerate(dils):
            if d == 1:
                continue
            rows = tm // d
            for r in range(d):
                for s in range(nslab):
                    cols = slice(s * LANES, (s + 1) * LANES)
                    tok_ref[2 * gi, s, pl.ds(r, rows, stride=d), :] = o_refs[gi][r, :, cols]
                    tok_ref[2 * gi + 1, s, pl.ds(r, rows, stride=d), :] = l_refs[gi][r, :, cols]
        for s in range(nslab):
            cols = slice(s * LANES, (s + 1) * LANES)
            os_, ls_ = [], []
            for gi, d in enumerate(dils):
                if d == 1:
                    os_.append(o_refs[gi][0, :, cols])
                    ls_.append(l_refs[gi][0, :, cols])
                else:
                    os_.append(tok_ref[2 * gi, s])
                    ls_.append(tok_ref[2 * gi + 1, s])
            mx = functools.reduce(jnp.maximum, ls_)
            es = [jnp.exp(l - mx) for l in ls_]
            inv = 1.0 / functools.reduce(lambda a, b: a + b, es)
            ya = functools.reduce(lambda a, b: a + b, [(e * inv) * o for e, o in zip(es, os_)])
            ya_ref[:, cols] = ya.astype(BF16)

    acc_h = jnp.dot(yh_ref[...], wh_ref[...], preferred_element_type=F32)
    acc_a = jnp.dot(ya_ref[...], wa_ref[...], preferred_element_type=F32)
    sig = lambda v: 1.0 / (1.0 + jnp.exp(-v))
    g_h = sig(gh_ref[...].astype(F32) + bh_ref[...])
    g_a = sig(ga_ref[...].astype(F32) + ba_ref[...])
    out_ref[...] = (g_h * acc_h + g_a * acc_a).astype(out_ref.dtype)


def branch_merge(y_hy, outs, lses, dils, u2, gate_col0, b_gate, w_hy, w_at, seq_len, tm=512):
    t, hw = y_hy.shape
    d = w_hy.shape[1]
    aw = w_at.shape[0]
    tn = _pick_tile(math.gcd(gate_col0, d), 1024)
    gb = gate_col0 // tn
    nj = d // tn
    tps = seq_len // tm
    assert all(tm % (dd * SUBLANES) == 0 for dd in dils)
    grp = [pl.BlockSpec((None, dd, tm // dd, aw), lambda i, j: (i // tps, 0, i % tps, 0)) for dd in dils]
    return pl.pallas_call(
        functools.partial(_merge_kernel, dils=tuple(dils)),
        grid=(t // tm, nj),
        in_specs=[pl.BlockSpec((tm, hw), lambda i, j: (i, 0))] + grp + grp + [
            pl.BlockSpec((tm, tn), lambda i, j: (i, gb + j)),
            pl.BlockSpec((tm, tn), lambda i, j: (i, gb + nj + j)),
            pl.BlockSpec((1, tn), lambda i, j: (0, j)),
            pl.BlockSpec((1, tn), lambda i, j: (0, nj + j)),
            pl.BlockSpec((hw, tn), lambda i, j: (0, j)),
            pl.BlockSpec((aw, tn), lambda i, j: (0, j)),
        ],
        out_specs=pl.BlockSpec((tm, tn), lambda i, j: (i, j)),
        out_shape=jax.ShapeDtypeStruct((t, d), BF16),
        scratch_shapes=[pltpu.VMEM((tm, aw), BF16), pltpu.VMEM((2 * len(dils), aw // LANES, tm, LANES), F32)],
        compiler_params=_cparams(("arbitrary", "arbitrary")),
        name="branch_merge",
    )(y_hy, *outs, *lses, u2, u2, b_gate, b_gate, w_hy, w_at)


def _out_proj_kernel(x_ref, m_ref, gate_ref, w_ref, o_ref):
    acc = jnp.dot(m_ref[...], w_ref[...], preferred_element_type=F32)
    o_ref[...] = x_ref[...] + gate_ref[0] * acc


def out_proj_residual(x, m, mod3, gate_idx, w, seq_len, tm=512, tn=None, name="out_proj_residual"):
    t, d = x.shape
    k = m.shape[1]
    tn = d if tn is None else tn
    nj = d // tn
    tiles_per_seq = seq_len // tm
    return pl.pallas_call(
        _out_proj_kernel,
        grid=(t // tm, nj),
        in_specs=[
            pl.BlockSpec((tm, tn), lambda i, j: (i, j)),
            pl.BlockSpec((tm, k), lambda i, j: (i, 0)),
            pl.BlockSpec((1, 1, tn), lambda i, j: (i // tiles_per_seq, 0, gate_idx * nj + j)),
            pl.BlockSpec((k, tn), lambda i, j: (0, j)),
        ],
        out_specs=pl.BlockSpec((tm, tn), lambda i, j: (i, j)),
        out_shape=jax.ShapeDtypeStruct((t, d), F32),
        input_output_aliases={0: 0},
        compiler_params=_cparams(("arbitrary", "arbitrary")),
        name=name,
    )(x, m, mod3, w)


def _final_norm_kernel(x_ref, g_ref, o_ref):
    x = x_ref[...]
    ms = jnp.mean(x * x, axis=-1, keepdims=True)
    o_ref[...] = x * lax.rsqrt(ms + NORM_EPS) * g_ref[...]


def final_norm(x, g, row0, n_rows, tm=512):
    d = x.shape[1]
    off = row0 // tm
    return pl.pallas_call(
        _final_norm_kernel,
        grid=(n_rows // tm,),
        in_specs=[pl.BlockSpec((tm, d), lambda i: (off + i, 0)), pl.BlockSpec((1, d), lambda i: (0, 0))],
        out_specs=pl.BlockSpec((tm, d), lambda i: (i, 0)),
        out_shape=jax.ShapeDtypeStruct((n_rows, d), F32),
        compiler_params=_cparams(("arbitrary",)),
        name="final_norm",
    )(x, g.reshape(1, d))


def kernel(x_prompt, x_sample, c_prompt, c_sample, ada_w, ada_b, norm1_g, w_in, b_gate, hy_conv_w, hy_conv_b,
           filt_w1, filt_b1, filt_freq1, filt_w2, filt_b2, filt_freq2, filt_w3, hy_bias, rel_bias, w_br_hy,
           w_br_attn, w_out, norm2_g, ffn_up, ffn_conv_w, ffn_conv_b, ffn_down, final_g):
    bp, L, d = x_prompt.shape
    bs = x_sample.shape[0]
    assert x_sample.shape[1] == L
    nb = bp + bs
    depth = ada_w.shape[0]
    hw = hy_bias.shape[2]
    n_groups = len(ATTN_GROUPS)
    attn_w = n_groups * HEADS_PER_GROUP * HEAD_DIM
    hy_cols = 3 * hw
    gate_col0 = hy_cols + 3 * attn_w
    assert w_in.shape[2] == gate_col0 + 2 * d

    x = jnp.concatenate([x_prompt, x_sample], axis=0).reshape(nb * L, d)
    nb_pad = -(-nb // SUBLANES) * SUBLANES
    c_pad = jnp.zeros((nb_pad, d), F32).at[:nb].set(jnp.concatenate([c_prompt, c_sample], axis=0))
    mod = ada_modulation(c_pad, ada_w, ada_b)

    hy_ct = min(HYENA_CT, hw)
    khat = hyena_filter_spectra(L, filt_w1, filt_b1, filt_freq1, filt_w2, filt_b2, filt_freq2, filt_w3, hy_ct)
    n_sides = [(window // 2) // dil for window, dil in ATTN_GROUPS]
    bias_tiles = attention_bias_tiles(rel_bias, n_sides)

    dils = [dil for _, dil in ATTN_GROUPS]
    gw = HEADS_PER_GROUP * HEAD_DIM

    def group_cols(w, g):
        return [w[:, hy_cols + part * attn_w + g * gw: hy_cols + part * attn_w + (g + 1) * gw] for part in range(3)]

    for l in range(depth):
        mod3 = mod[l].reshape(nb_pad, 1, N_MOD * d)
        w_l = w_in[l]
        w_main = jnp.concatenate([w_l[:, :hy_cols], w_l[:, gate_col0:]], axis=1).astype(BF16)
        w_groups = jnp.concatenate([c for g in range(n_groups) for c in group_cols(w_l, g)], axis=1).astype(BF16)
        u, qkv = in_projection(x, norm1_g[l], mod3, 1, 0, w_main, w_groups, dils, nb, L)
        u3 = u.reshape(nb, L, u.shape[1])
        y_hy = hyena_mix(u3, 0, hy_conv_w[l], hy_conv_b[l].reshape(1, hy_cols), khat, l, hy_bias[l], hy_ct)
        outs, lses = [], []
        for g, dil in enumerate(dils):
            o_g, l_g = dilated_group_attention(qkv[g], g, dil, n_sides[g], bias_tiles)
            outs.append(o_g)
            lses.append(l_g)
        merged = branch_merge(y_hy.reshape(nb * L, hw), outs, lses, dils, u, hy_cols, b_gate[l].reshape(1, 2 * d),
                              w_br_hy[l].astype(BF16), w_br_attn[l].astype(BF16), L)
        x = out_proj_residual(x, merged, mod3, 2, w_out[l].astype(BF16), L)
        ff = ffn_down.shape[1]
        act = ffn_up_act(x, norm2_g[l], mod3, 4, 3, interleave_ffn_up(ffn_up[l]).astype(BF16),
                         ffn_conv_w[l], ffn_conv_b[l].reshape(1, ff), L)
        x = out_proj_residual(x, act, mod3, 5, ffn_down[l].astype(BF16), L, tn=_pick_tile(d, 512),
                              name="ffn_down_residual")

    y_prompt = final_norm(x, final_g, 0, bp * L).reshape(bp, L, d)
    y_sample = final_norm(x, final_g, bp * L, bs * L).reshape(bs, L, d)
    return (y_prompt, y_sample)
```

```python
import functools
import math

import jax
import jax.numpy as jnp
import numpy as np
from jax import lax
from jax.experimental import pallas as pl
from jax.experimental.pallas import tpu as pltpu

F32 = jnp.float32
BF16 = jnp.bfloat16
HIGHEST = lax.Precision.HIGHEST

NORM_EPS = 1e-6
N_MOD = 6
HYENA_ORDER = 2
N_BANDS = 16
FAST_DECAY_PCT = 0.3
SLOW_DECAY_PCT = 1.5
DECAY_TARGET = 1e-2
ATTN_GROUPS = ((128, 1), (512, 4), (2048, 16))
HEADS_PER_GROUP = 8
HEAD_DIM = 64
REL_BUCKETS = 32
REL_MAX_DIST = 1024
NEG_INF = -1e30

LANES = 128
SUBLANES = 8
VMEM_LIMIT = 56 * 1024 * 1024

FFT_N2 = 128
FFT_GROUP = SUBLANES
HYENA_CT = 256
ATTN_QB = 128


def _cparams(sem):
    return pltpu.CompilerParams(dimension_semantics=sem, vmem_limit_bytes=VMEM_LIMIT)


def _pick_tile(n, cap):
    best = None
    for t in range(LANES, min(n, cap) + 1, LANES):
        if n % t == 0:
            best = t
    assert best is not None, (n, cap)
    return best


def _ada_kernel(c_ref, w_ref, b_ref, o_ref):
    c = c_ref[...]
    cs = c * (1.0 / (1.0 + jnp.exp(-c)))
    o_ref[0] = jnp.dot(cs, w_ref[0], preferred_element_type=F32, precision=HIGHEST) + b_ref[0]


def ada_modulation(c_pad, ada_w, ada_b):
    depth, d, n = ada_w.shape
    nb = c_pad.shape[0]
    tn = _pick_tile(n, 1024)
    return pl.pallas_call(
        _ada_kernel,
        grid=(depth, n // tn),
        in_specs=[
            pl.BlockSpec((nb, d), lambda l, j: (0, 0)),
            pl.BlockSpec((1, d, tn), lambda l, j: (l, 0, j)),
            pl.BlockSpec((1, 1, tn), lambda l, j: (l, 0, j)),
        ],
        out_specs=pl.BlockSpec((1, nb, tn), lambda l, j: (l, 0, j)),
        out_shape=jax.ShapeDtypeStruct((depth, nb, n), F32),
        compiler_params=_cparams(("arbitrary", "arbitrary")),
        name="ada_modulation",
    )(c_pad, ada_w, ada_b.reshape(depth, 1, n))


def _normmod_rows(x, g, sc, sh):
    ms = jnp.mean(x * x, axis=-1, keepdims=True)
    return (x * lax.rsqrt(ms + NORM_EPS)) * (g * (1.0 + sc)) + sh


FFN_HALO = 16


def _ffn_up_act_kernel(x_ref, xp_ref, xn_ref, g_ref, sc_ref, sh_ref, w_ref, cw_ref, cb_ref,
                       o_ref, h_ref, *, tiles_per_seq):
    i = pl.program_id(0)
    tm = x_ref.shape[0]

    @pl.when(pl.program_id(1) == 0)
    def _():
        g, sc, sh = g_ref[...], sc_ref[0], sh_ref[0]
        h_ref[0:FFN_HALO] = _normmod_rows(xp_ref[...], g, sc, sh).astype(BF16)
        h_ref[FFN_HALO:FFN_HALO + tm] = _normmod_rows(x_ref[...], g, sc, sh).astype(BF16)
        h_ref[FFN_HALO + tm:] = _normmod_rows(xn_ref[...], g, sc, sh).astype(BF16)

    first = (i % tiles_per_seq) == 0
    last = (i % tiles_per_seq) == tiles_per_seq - 1
    row = lax.broadcasted_iota(jnp.int32, (tm + 2 * FFN_HALO, 1), 0)
    outside = ((row == FFN_HALO - 1) & first) | ((row == FFN_HALO + tm) & last)
    for c in range(o_ref.shape[1] // LANES):
        res = jnp.dot(h_ref[...], w_ref[:, 2 * c * LANES:2 * (c + 1) * LANES], preferred_element_type=F32)
        a = res[FFN_HALO:FFN_HALO + tm, :LANES]
        gt = jnp.where(outside, 0.0, res[:, LANES:])
        cols = slice(c * LANES, (c + 1) * LANES)
        conv = _dwconv3_rows(gt, cw_ref.at[:, cols], cb_ref.at[:, cols])[FFN_HALO:FFN_HALO + tm]
        o_ref[:, cols] = (conv * (1.0 / (1.0 + jnp.exp(-conv))) * a).astype(o_ref.dtype)


def interleave_ffn_up(w_up):
    d, two_ff = w_up.shape
    ff = two_ff // 2
    return w_up.reshape(d, 2, ff // LANES, LANES).transpose(0, 2, 1, 3).reshape(d, two_ff)


def ffn_up_act(x, g, mod3, sc_idx, sh_idx, w_il, conv_w, conv_b, seq_len, tm=512, tn_cap=1536):
    t, d = x.shape
    ff = w_il.shape[1] // 2
    tn = _pick_tile(ff, tn_cap)
    nj = ff // tn
    tps = seq_len // tm
    rb = tm // FFN_HALO
    last_rb = t // FFN_HALO - 1
    return pl.pallas_call(
        functools.partial(_ffn_up_act_kernel, tiles_per_seq=tps),
        grid=(t // tm, nj),
        in_specs=[
            pl.BlockSpec((tm, d), lambda i, j: (i, 0)),
            pl.BlockSpec((FFN_HALO, d), lambda i, j: (jnp.maximum(i * rb - 1, 0), 0)),
            pl.BlockSpec((FFN_HALO, d), lambda i, j: (jnp.minimum((i + 1) * rb, last_rb), 0)),
            pl.BlockSpec((1, d), lambda i, j: (0, 0)),
            pl.BlockSpec((1, 1, d), lambda i, j: (i // tps, 0, sc_idx)),
            pl.BlockSpec((1, 1, d), lambda i, j: (i // tps, 0, sh_idx)),
            pl.BlockSpec((d, 2 * tn), lambda i, j: (0, j)),
            pl.BlockSpec((3, tn), lambda i, j: (0, j)),
            pl.BlockSpec((1, tn), lambda i, j: (0, j)),
        ],
        out_specs=pl.BlockSpec((tm, tn), lambda i, j: (i, j)),
        out_shape=jax.ShapeDtypeStruct((t, ff), BF16),
        scratch_shapes=[pltpu.VMEM((tm + 2 * FFN_HALO, d), BF16)],
        compiler_params=_cparams(("arbitrary", "arbitrary")),
        name="ffn_up_act",
    )(x, x, x, g.reshape(1, d), mod3, mod3, w_il, conv_w, conv_b)


def _in_proj_kernel(x_ref, g_ref, sc_ref, sh_ref, wm_ref, wg_ref, om_ref, *rest, n_main, dils, chunk):
    og_refs = rest[:len(dils)]
    h_ref, perm_ref = rest[len(dils):]
    j = pl.program_id(1)
    tm = h_ref.shape[0]

    @pl.when(j == 0)
    def _():
        h_ref[...] = _normmod_rows(x_ref[...], g_ref[...], sc_ref[0], sh_ref[0]).astype(BF16)

    @pl.when(j < n_main)
    def _():
        om_ref[...] = jnp.dot(h_ref[...], wm_ref[...], preferred_element_type=F32).astype(om_ref.dtype)

    for gi, (o_ref, d) in enumerate(zip(og_refs, dils)):
        @pl.when(j == n_main + gi)
        def _():
            rows = tm // d
            for c0 in range(0, wg_ref.shape[1], chunk):
                res = jnp.dot(h_ref[...], wg_ref[:, c0:c0 + chunk], preferred_element_type=F32)
                if d == 1:
                    o_ref[0, 0, :, c0:c0 + chunk] = res.astype(o_ref.dtype)
                    continue
                for s in range(chunk // LANES):
                    perm_ref[s] = res[:, s * LANES:(s + 1) * LANES]
                for r in range(d):
                    for s in range(chunk // LANES):
                        cols = slice(c0 + s * LANES, c0 + (s + 1) * LANES)
                        o_ref[0, r, :, cols] = perm_ref[s, pl.ds(r, rows, stride=d), :].astype(o_ref.dtype)


def in_projection(x, g, mod3, sc_idx, sh_idx, w_main, w_groups, dils, nb, seq_len, tm=512, tn_cap=1792):
    t, d_model = x.shape
    n_main = w_main.shape[1]
    gw = w_groups.shape[1] // len(dils)
    tn = _pick_tile(n_main, tn_cap)
    nj_main = n_main // tn
    tps = seq_len // tm
    chunk = gw // 3
    assert chunk % LANES == 0 and all(tm % (dd * 16) == 0 for dd in dils)
    kernel = functools.partial(_in_proj_kernel, n_main=nj_main, dils=tuple(dils), chunk=chunk)
    outs = pl.pallas_call(
        kernel,
        grid=(t // tm, nj_main + len(dils)),
        in_specs=[
            pl.BlockSpec((tm, d_model), lambda i, j: (i, 0)),
            pl.BlockSpec((1, d_model), lambda i, j: (0, 0)),
            pl.BlockSpec((1, 1, d_model), lambda i, j: (i // tps, 0, sc_idx)),
            pl.BlockSpec((1, 1, d_model), lambda i, j: (i // tps, 0, sh_idx)),
            pl.BlockSpec((d_model, tn), lambda i, j: (0, jnp.minimum(j, nj_main - 1))),
            pl.BlockSpec((d_model, gw), lambda i, j: (0, jnp.clip(j - nj_main, 0, len(dils) - 1))),
        ],
        out_specs=[pl.BlockSpec((tm, tn), lambda i, j: (i, jnp.minimum(j, nj_main - 1)))] + [
            pl.BlockSpec((1, dd, tm // dd, gw), lambda i, j: (i // tps, 0, i % tps, 0)) for dd in dils],
        out_shape=[jax.ShapeDtypeStruct((t, n_main), BF16)] + [
            jax.ShapeDtypeStruct((nb, dd, seq_len // dd, gw), BF16) for dd in dils],
        scratch_shapes=[pltpu.VMEM((tm, d_model), BF16), pltpu.VMEM((chunk // LANES, tm, LANES), F32)],
        compiler_params=_cparams(("arbitrary", "arbitrary")),
        name="in_projection",
    )(x, g.reshape(1, d_model), mod3, mod3, w_main, w_groups)
    return outs[0], list(outs[1:])


def _fft_dims(seq_len):
    n_fft = 2 * seq_len
    n1 = n_fft // FFT_N2
    assert n1 * FFT_N2 == n_fft and n1 % 2 == 0
    nk1 = n1 // 2 + 1
    pad = -(-nk1 // SUBLANES) * SUBLANES
    return n_fft, n1, nk1, pad


@functools.lru_cache(maxsize=None)
def _fft_constants(seq_len):
    n_fft, n1, nk1, pad = _fft_dims(seq_len)
    n2 = FFT_N2
    k1 = np.arange(nk1)[None, :, None]
    i1 = np.arange(n1)[None, None, :]
    i2 = np.arange(n2)[:, None, None]
    ph = 2.0 * np.pi * ((k1 * (i1 * n2 + i2)) % n_fft) / n_fft
    f1 = np.zeros((n2, 2 * pad, n1), np.float64)
    f1[:, :nk1] = np.cos(ph)
    f1[:, pad:pad + nk1] = -np.sin(ph)
    ck = np.full((nk1,), 2.0)
    ck[0] = 1.0
    ck[-1] = 1.0
    g1 = np.zeros((n2, n1 // 2, 2 * pad), np.float64)
    pht = np.transpose(ph[:, :, :n1 // 2], (0, 2, 1))
    g1[:, :, :nk1] = np.cos(pht) * ck / n_fft
    g1[:, :, pad:pad + nk1] = -np.sin(pht) * ck / n_fft
    a = np.arange(n2)
    ph2 = 2.0 * np.pi * ((a[:, None] * a[None, :]) % n2) / n2
    c2, s2 = np.cos(ph2), np.sin(ph2)
    f2 = np.block([[c2, s2], [-s2, c2]])
    g2 = np.block([[c2, -s2], [s2, c2]])
    return (f1.astype(np.float32), f2.astype(np.float32), g2.astype(np.float32), g1.astype(np.float32))


def _interleaved_block_diag(mats):
    n, r, c = mats.shape
    out = np.zeros((n // FFT_GROUP, r * FFT_GROUP, c * FFT_GROUP), mats.dtype)
    for j in range(FFT_GROUP):
        out[:, j::FFT_GROUP, j::FFT_GROUP] = mats[j::FFT_GROUP]
    return out


def _group_rows(g):
    return pl.ds(pl.multiple_of(g * FFT_GROUP, FFT_GROUP), FFT_GROUP)


def _ld_group(ref, g):
    n_outer = ref.shape[1]
    return jnp.concatenate([ref[h, :, _group_rows(g), :].reshape(n_outer * FFT_GROUP, LANES)
                            for h in range(ref.shape[0])], axis=1)


def _st_group(ref, g, val):
    n_outer = ref.shape[1]
    for h in range(ref.shape[0]):
        ref[h, :, _group_rows(g), :] = val[:, h * LANES:(h + 1) * LANES].reshape(n_outer, FFT_GROUP, LANES)


def _ld_outer(ref, idx):
    return jnp.concatenate([ref[h, idx] for h in range(ref.shape[0])], axis=1)


def _st_outer(ref, idx, val):
    for h in range(ref.shape[0]):
        ref[h, idx] = val[:, h * LANES:(h + 1) * LANES]


def _st_rows(ref, row0, val):
    n = val.shape[0]
    for h in range(ref.shape[0]):
        ref[h, row0 // FFT_N2:(row0 + n) // FFT_N2] = val[:, h * LANES:(h + 1) * LANES].reshape(
            n // FFT_N2, FFT_N2, LANES)


def _fft_stage1(src_ref, f1b_ref, a_ref):
    def body(g, carry):
        a = jnp.dot(f1b_ref[g], _ld_group(src_ref, g).astype(BF16), preferred_element_type=F32)
        _st_group(a_ref, g, a)
        return carry
    lax.fori_loop(0, FFT_N2 // FFT_GROUP, body, 0, unroll=2)


def _fft_stage2_rhs(a_ref, k1, pad):
    return jnp.concatenate([_ld_outer(a_ref, k1), _ld_outer(a_ref, pad + k1)], axis=0).astype(BF16)


def _filter_kernel(feat_ref, featr_ref, w1_ref, b1_ref, fr1_ref, w2_ref, b2_ref, fr2_ref,
                   w3f_ref, w3b_ref, delta_ref, f1_ref, f2_ref,
                   o_ref, hid_ref, k2_ref, a_ref, *, seq_len):
    n_fft, n1, nk1, pad = _fft_dims(seq_len)
    L = seq_len

    @pl.when((pl.program_id(1) == 0) & (pl.program_id(2) == 0))
    def _():
        for d, fref in enumerate((feat_ref, featr_ref)):
            h = jnp.dot(fref[...], w1_ref[0], preferred_element_type=F32, precision=HIGHEST) + b1_ref[0]
            h = jnp.sin(fr1_ref[0] * h)
            h = jnp.dot(h, w2_ref[0], preferred_element_type=F32, precision=HIGHEST) + b2_ref[0]
            hid_ref[d] = jnp.sin(fr2_ref[0] * h)

    delta = delta_ref[...]
    chunk = min(L, 512)
    row = lax.broadcasted_iota(jnp.int32, (chunk, 1), 0)
    l1 = jnp.zeros(delta.shape, F32)
    for d, (w3_ref, fref) in enumerate(((w3f_ref, feat_ref), (w3b_ref, featr_ref))):
        for c in range(L // chunk):
            rows = pl.ds(c * chunk, chunk)
            h = jnp.dot(hid_ref[d, rows, :], w3_ref[0], preferred_element_type=F32, precision=HIGHEST)
            h = h * jnp.exp(-fref[rows, 0:1] * delta)
            if d == 1 and c == 0:
                h = jnp.where(row == 0, 0.0, h)
            l1 = l1 + jnp.sum(jnp.abs(h), axis=0, keepdims=True)
            _st_rows(k2_ref, d * L + c * chunk, h)
    inv = 1.0 / l1
    _fft_stage1(k2_ref, f1_ref, a_ref)

    def body(k1, carry):
        spec = jnp.dot(f2_ref[...], _fft_stage2_rhs(a_ref, k1, pad), preferred_element_type=F32) * inv
        o_ref[0, 0, k1] = spec.astype(o_ref.dtype)
        return carry
    lax.fori_loop(0, nk1, body, 0, unroll=3 if nk1 % 3 == 0 else 1)


def hyena_filter_spectra(seq_len, filt_w1, filt_b1, filt_freq1, filt_w2, filt_b2, filt_freq2, filt_w3, ct):
    depth, n_feat, hid = filt_w1.shape
    hw = filt_w3.shape[2] // (2 * HYENA_ORDER)
    n_fft, n1, nk1, pad = _fft_dims(seq_len)
    L = seq_len
    t = np.linspace(0.0, 1.0, L, dtype=np.float32).astype(np.float64)[:, None]
    pos = np.arange(L, dtype=np.float64)[:, None]
    bands = np.linspace(1e-4, N_BANDS - 1, N_BANDS, dtype=np.float32).astype(np.float64)[None, :]
    ang = (2.0 * math.pi / L) * pos * bands
    feat = np.concatenate([t, np.cos(ang), -np.sin(ang)], axis=-1)
    assert feat.shape[1] == n_feat
    feat_rev = np.concatenate([np.zeros((1, n_feat)), feat[:0:-1]], axis=0)
    n_feat_pad = -(-n_feat // LANES) * LANES
    feat = np.pad(feat, ((0, 0), (0, n_feat_pad - n_feat)))
    feat_rev = np.pad(feat_rev, ((0, 0), (0, n_feat_pad - n_feat)))
    filt_w1 = jnp.pad(filt_w1, ((0, 0), (0, n_feat_pad - n_feat), (0, 0)))
    n_feat = n_feat_pad
    max_decay = math.log(DECAY_TARGET) / FAST_DECAY_PCT
    min_decay = math.log(DECAY_TARGET) / SLOW_DECAY_PCT
    deltas = np.abs(np.linspace(min_decay, max_decay, hw, dtype=np.float32))[None, :]
    f1, f2, _, _ = _fft_constants(seq_len)
    f1 = _interleaved_block_diag(f1)
    nct = hw // ct
    nslab = ct // LANES
    assert L % FFT_N2 == 0 and (L < 512 or L % 512 == 0)

    def w3_spec(direction):
        return pl.BlockSpec((1, hid, ct), lambda l, c, o: (l, 0, (direction * HYENA_ORDER + o) * nct + c))

    def const(shape):
        return pl.BlockSpec(shape, lambda l, c, o: (0,) * len(shape), pipeline_mode=pl.Buffered(1))

    vec = lambda a: a.reshape(depth, 1, hid)
    vspec = pl.BlockSpec((1, 1, hid), lambda l, c, o: (l, 0, 0))
    return pl.pallas_call(
        functools.partial(_filter_kernel, seq_len=seq_len),
        grid=(depth, nct, HYENA_ORDER),
        in_specs=[
            const((L, n_feat)), const((L, n_feat)),
            pl.BlockSpec((1, n_feat, hid), lambda l, c, o: (l, 0, 0)), vspec, vspec,
            pl.BlockSpec((1, hid, hid), lambda l, c, o: (l, 0, 0)), vspec, vspec,
            w3_spec(0), w3_spec(1),
            pl.BlockSpec((1, ct), lambda l, c, o: (0, c)),
            const(f1.shape), const((2 * FFT_N2, 2 * FFT_N2)),
        ],
        out_specs=pl.BlockSpec((1, 1, nk1, 2 * FFT_N2, ct), lambda l, c, o: (l, o, 0, 0, c)),
        out_shape=jax.ShapeDtypeStruct((depth, HYENA_ORDER, nk1, 2 * FFT_N2, hw), BF16),
        scratch_shapes=[
            pltpu.VMEM((2, L, hid), F32),
            pltpu.VMEM((nslab, n1, FFT_N2, LANES), F32),
            pltpu.VMEM((nslab, 2 * pad, FFT_N2, LANES), F32),
        ],
        compiler_params=_cparams(("arbitrary", "arbitrary", "arbitrary")),
        name="hyena_filter_spectra",
    )(jnp.asarray(feat, F32), jnp.asarray(feat_rev, F32), filt_w1, vec(filt_b1), vec(filt_freq1),
      filt_w2, vec(filt_b2), vec(filt_freq2), filt_w3, filt_w3,
      jnp.asarray(deltas, F32), jnp.asarray(f1, BF16), jnp.asarray(f2, BF16))


def _dwconv3_rows(x, w_ref, b_ref, prev_row=None, next_row=None):
    n = x.shape[0]
    row = lax.broadcasted_iota(jnp.int32, (SUBLANES, 1), 0)
    xm = pltpu.roll(x, 1, axis=0)
    xp = pltpu.roll(x, n - 1, axis=0)
    head = jnp.where(row == 0, 0.0 if prev_row is None else prev_row, xm[:SUBLANES])
    tail = jnp.where(row == SUBLANES - 1, 0.0 if next_row is None else next_row, xp[n - SUBLANES:])
    xm = jnp.concatenate([head, xm[SUBLANES:]], axis=0)
    xp = jnp.concatenate([xp[:n - SUBLANES], tail], axis=0)
    return xm * w_ref[0:1, :] + x * w_ref[1:2, :] + xp * w_ref[2:3, :] + b_ref[...]


def _dwconv3_to_slabs(src_ref, w_ref, b_ref, dst_ref, seq_len, chunk):
    halo = 16
    n_chunks = seq_len // chunk
    for c in range(n_chunks):
        r0 = c * chunk
        x = src_ref[0, r0:r0 + chunk, :].astype(F32)
        prev_row = src_ref[0, r0 - halo:r0, :].astype(F32)[halo - 1:halo] if c > 0 else None
        next_row = src_ref[0, r0 + chunk:r0 + chunk + halo, :].astype(F32)[0:1] if c < n_chunks - 1 else None
        _st_rows(dst_ref, r0, _dwconv3_rows(x, w_ref, b_ref, prev_row, next_row))


def _hyena_conv_kernel(zin_ref, ux_ref, wz_ref, bz_ref, wx_ref, bx_ref, hb_ref, kh_ref,
                       f1_ref, f2_ref, g2_ref, g1_ref, o_ref, z_ref, xg_ref, a_ref, y_ref, *, seq_len, first):
    n_fft, n1, nk1, pad = _fft_dims(seq_len)
    L = seq_len
    half = n1 // 2
    nslab = z_ref.shape[0]
    chunk = min(L, 512)

    if first:
        _dwconv3_to_slabs(zin_ref, wz_ref, bz_ref, z_ref, L, chunk)
    else:
        for c in range(L // chunk):
            _st_rows(z_ref, c * chunk, zin_ref[0, c * chunk:(c + 1) * chunk, :])
    _dwconv3_to_slabs(ux_ref, wx_ref, bx_ref, xg_ref, L, chunk)

    _fft_stage1(z_ref, f1_ref, a_ref)

    u2 = 3 if nk1 % 3 == 0 else 1
    n_trips = nk1 // u2

    def spectrum_product(t):
        ys = []
        for j in range(u2):
            k1 = t * u2 + j
            x = jnp.dot(f2_ref[...], _fft_stage2_rhs(a_ref, k1, pad), preferred_element_type=F32)
            kk = kh_ref[0, 0, k1].astype(F32)
            xr, xi = x[:FFT_N2], x[FFT_N2:]
            kr, ki = kk[:FFT_N2], kk[FFT_N2:]
            ys.append(jnp.concatenate([xr * kr - xi * ki, xr * ki + xi * kr], axis=0).astype(BF16))
        return ys

    def inverse_dots(slot):
        return [jnp.dot(g2_ref[...], y_ref[slot, j], preferred_element_type=F32) for j in range(u2)]

    def store_inverse(t, bks):
        for j, bk in enumerate(bks):
            _st_outer(a_ref, t * u2 + j, bk[:FFT_N2])
            _st_outer(a_ref, pad + t * u2 + j, bk[FFT_N2:])

    def store_products(slot, ys):
        for j, y in enumerate(ys):
            y_ref[slot, j] = y

    store_products(0, spectrum_product(0))

    def body2(t, carry):
        ys = spectrum_product(t)
        bks = inverse_dots((t + 1) % 2)
        store_inverse(t - 1, bks)
        store_products(t % 2, ys)
        return carry
    lax.fori_loop(1, n_trips, body2, 0)
    store_inverse(n_trips - 1, inverse_dots((n_trips - 1) % 2))

    bias = hb_ref[...]
    u3 = 2

    def body3(t, carry):
        gs = [t * u3 + j for j in range(u3)]
        ins = [(_ld_group(a_ref, g).astype(BF16), _ld_group(xg_ref, g), _ld_group(z_ref, g)) for g in gs]
        outs = [xg * (jnp.dot(g1_ref[g], bb, preferred_element_type=F32) + bias * z_old)
                for g, (bb, xg, z_old) in zip(gs, ins)]
        for g, z_new in zip(gs, outs):
            _st_group(z_ref, g, z_new)
        return carry
    lax.fori_loop(0, FFT_N2 // FFT_GROUP // u3, body3, 0)

    for c in range(L // chunk):
        blk = jnp.concatenate([z_ref[h, c * chunk // FFT_N2:(c + 1) * chunk // FFT_N2].reshape(chunk, LANES)
                               for h in range(nslab)], axis=1)
        o_ref[0, c * chunk:(c + 1) * chunk, :] = blk.astype(o_ref.dtype)


def hyena_step(zin, zin_col0, u3, x_col0, w_cols, conv_w, conv_b, khat, layer, order, hy_bias_row, first, ct):
    nb, L, _ = u3.shape
    hw = hy_bias_row.shape[1]
    n_fft, n1, nk1, pad = _fft_dims(L)
    half = n1 // 2
    f1, f2, g2, g1 = _fft_constants(L)
    f1 = _interleaved_block_diag(f1[:, :, :half])
    g1 = _interleaved_block_diag(g1)
    nct = hw // ct
    nslab = ct // LANES
    assert zin_col0 % ct == 0 and x_col0 % ct == 0 and w_cols[0] % ct == 0 and w_cols[1] % ct == 0
    last = order == HYENA_ORDER - 1

    def col_spec(rows, col0):
        return pl.BlockSpec((rows, ct), lambda c, b: (0, col0 // ct + c))

    const = lambda shape: pl.BlockSpec(shape, lambda c, b: (0,) * len(shape), pipeline_mode=pl.Buffered(1))
    return pl.pallas_call(
        functools.partial(_hyena_conv_kernel, seq_len=L, first=first),
        grid=(nct, nb),
        in_specs=[
            pl.BlockSpec((1, L, ct), lambda c, b: (b, 0, zin_col0 // ct + c)),
            pl.BlockSpec((1, L, ct), lambda c, b: (b, 0, x_col0 // ct + c)),
            col_spec(3, w_cols[0]), col_spec(1, w_cols[0]), col_spec(3, w_cols[1]), col_spec(1, w_cols[1]),
            pl.BlockSpec((1, ct), lambda c, b: (0, c)),
            pl.BlockSpec((1, 1, nk1, 2 * FFT_N2, ct), lambda c, b: (layer, order, 0, 0, c),
                         pipeline_mode=pl.Buffered(1)),
            const(f1.shape), const((2 * FFT_N2, 2 * FFT_N2)),
            const((2 * FFT_N2, 2 * FFT_N2)), const(g1.shape),
        ],
        out_specs=pl.BlockSpec((1, L, ct), lambda c, b: (b, 0, c)),
        out_shape=jax.ShapeDtypeStruct((nb, L, hw), BF16 if last else F32),
        scratch_shapes=[
            pltpu.VMEM((nslab, half, FFT_N2, LANES), F32),
            pltpu.VMEM((nslab, half, FFT_N2, LANES), F32),
            pltpu.VMEM((nslab, 2 * pad, FFT_N2, LANES), F32),
            pltpu.VMEM((2, 3 if nk1 % 3 == 0 else 1, 2 * FFT_N2, ct), BF16),
        ],
        compiler_params=_cparams(("arbitrary", "arbitrary")),
        name=f"hyena_step{order}",
    )(zin, u3, conv_w, conv_b, conv_w, conv_b, hy_bias_row, khat,
      jnp.asarray(f1, BF16), jnp.asarray(f2, BF16), jnp.asarray(g2, BF16), jnp.asarray(g1, BF16))


def hyena_mix(u3, col0, conv_w, conv_b, khat, layer, hy_bias, ct):
    hw = hy_bias.shape[1]
    z = hyena_step(u3, col0, u3, col0 + hw, (0, hw), conv_w, conv_b, khat, layer, 0, hy_bias[0:1], True, ct)
    return hyena_step(z, 0, u3, col0 + 2 * hw, (0, 2 * hw), conv_w, conv_b, khat, layer, 1, hy_bias[1:2], False, ct)


def _t5_bucket(rel):
    nb = REL_BUCKETS // 2
    ret = (rel > 0).astype(np.int32) * nb
    n = np.abs(rel)
    max_exact = nb // 2
    large = max_exact + (np.log(np.maximum(n, 1) / max_exact) / np.log(REL_MAX_DIST / max_exact)
                         * (nb - max_exact)).astype(np.int32)
    large = np.minimum(large, nb - 1)
    return ret + np.where(n < max_exact, n, large)


ATTN_EDGE_VARIANTS = 4


def _bias_kernel(rb_ref, bkt_ref, o_ref, *, n_sides):
    n_groups = bkt_ref.shape[0]
    qq = lax.broadcasted_iota(jnp.int32, (ATTN_QB, 2 * ATTN_QB), 0)
    kk = lax.broadcasted_iota(jnp.int32, (ATTN_QB, 2 * ATTN_QB), 1)
    for g in range(n_groups):
        n_side = n_sides[g]
        rel = kk - n_side - qq
        band = (rel >= -n_side) & (rel <= n_side)
        before = kk < n_side
        after = kk >= ATTN_QB + n_side
        bkt = bkt_ref[g]
        for h in range(HEADS_PER_GROUP):
            tile = jnp.zeros(bkt.shape, F32)
            for b in range(REL_BUCKETS):
                tile = jnp.where(bkt == b, rb_ref[b, g * HEADS_PER_GROUP + h], tile)
            tile = jnp.where(band, tile, NEG_INF)
            o_ref[g, 0, h] = tile
            o_ref[g, 1, h] = jnp.where(before, NEG_INF, tile)
            o_ref[g, 2, h] = jnp.where(after, NEG_INF, tile)
            o_ref[g, 3, h] = jnp.where(before | after, NEG_INF, tile)


def attention_bias_tiles(rel_bias, n_side_list):
    n_groups = len(ATTN_GROUPS)
    qq = np.arange(ATTN_QB)[:, None]
    kk = np.arange(2 * ATTN_QB)[None, :]
    bkts = []
    for (window, dil), n_side in zip(ATTN_GROUPS, n_side_list):
        j = kk - n_side - qq
        bkts.append(_t5_bucket(j * dil))
    bkt = np.stack(bkts).astype(np.int32)
    return pl.pallas_call(
        functools.partial(_bias_kernel, n_sides=tuple(n_side_list)),
        in_specs=[pl.BlockSpec(memory_space=pltpu.SMEM), pl.BlockSpec(memory_space=pltpu.VMEM)],
        out_specs=pl.BlockSpec(memory_space=pltpu.VMEM),
        out_shape=jax.ShapeDtypeStruct((n_groups, ATTN_EDGE_VARIANTS, HEADS_PER_GROUP, ATTN_QB, 2 * ATTN_QB), F32),
        name="attention_bias_tiles",
    )(rel_bias, jnp.asarray(bkt))


def _attn_kernel(q_ref, k_ref, v_ref, bias_ref, o_ref, l_ref, ks_ref, vs_ref, *, m_len, n_side, tq):
    qt = pl.program_id(2)
    width = HEADS_PER_GROUP * HEAD_DIM
    halo = n_side

    @pl.when(qt == 0)
    def _():
        zeros = jnp.zeros((halo, width), BF16)
        for ref, src in ((ks_ref, k_ref), (vs_ref, v_ref)):
            ref[pl.ds(0, halo), :] = zeros
            ref[pl.ds(halo + m_len, 2 * ATTN_QB - halo), :] = jnp.zeros((2 * ATTN_QB - halo, width), BF16)
            ref[pl.ds(halo, m_len), :] = src[...]

    lane = lax.broadcasted_iota(jnp.int32, (ATTN_QB, 2 * HEAD_DIM), 1)
    low = lane < HEAD_DIM
    scale = 1.0 / math.sqrt(HEAD_DIM)

    def block(ib, carry):
        r0 = pl.multiple_of(ib * ATTN_QB, ATTN_QB)
        q0 = pl.multiple_of(qt * tq + r0, ATTN_QB)
        variant = (q0 == 0).astype(jnp.int32) + 2 * (q0 == m_len - ATTN_QB).astype(jnp.int32)
        for hp in range(HEADS_PER_GROUP // 2):
            cols = pl.ds(hp * 2 * HEAD_DIM, 2 * HEAD_DIM)
            qp = q_ref[pl.ds(r0, ATTN_QB), cols] * scale
            kp = ks_ref[pl.ds(q0, 2 * ATTN_QB), cols]
            vp = vs_ref[pl.ds(q0, 2 * ATTN_QB), cols]
            outs, lses = [], []
            for hh in range(2):
                sel = low if hh == 0 else jnp.logical_not(low)
                qm = jnp.where(sel, qp, jnp.zeros_like(qp))
                s = lax.dot_general(qm, kp, (((1,), (1,)), ((), ())), preferred_element_type=F32)
                s = s + bias_ref[variant, 2 * hp + hh]
                mx = jnp.max(s, axis=-1, keepdims=True)
                p = jnp.exp(s - mx)
                den = jnp.sum(p, axis=-1, keepdims=True)
                pv = jnp.dot(p.astype(BF16), vp, preferred_element_type=F32)
                outs.append(pv / den)
                lses.append(mx + jnp.log(den))
            o_ref[pl.ds(r0, ATTN_QB), cols] = jnp.where(low, outs[0], outs[1])
            l_ref[pl.ds(r0, ATTN_QB), cols] = jnp.where(low, lses[0], lses[1])
        return carry
    lax.fori_loop(0, tq // ATTN_QB, block, 0)


def dilated_group_attention(qkv, g, dil, n_side, bias_tiles):
    nb, d, m_len, _ = qkv.shape
    width = HEADS_PER_GROUP * HEAD_DIM
    assert d == dil and m_len % ATTN_QB == 0 and n_side <= ATTN_QB and n_side % 16 == 0
    tq = min(512, m_len)
    kernel = functools.partial(_attn_kernel, m_len=m_len, n_side=n_side, tq=tq)
    return pl.pallas_call(
        kernel,
        grid=(nb, dil, m_len // tq),
        in_specs=[
            pl.BlockSpec((None, None, tq, width), lambda b, r, t: (b, r, t, 0)),
            pl.BlockSpec((None, None, m_len, width), lambda b, r, t: (b, r, 0, 1)),
            pl.BlockSpec((None, None, m_len, width), lambda b, r, t: (b, r, 0, 2)),
            pl.BlockSpec((None, ATTN_EDGE_VARIANTS, HEADS_PER_GROUP, ATTN_QB, 2 * ATTN_QB),
                         lambda b, r, t: (g, 0, 0, 0, 0)),
        ],
        out_specs=[
            pl.BlockSpec((None, None, tq, width), lambda b, r, t: (b, r, t, 0)),
            pl.BlockSpec((None, None, tq, width), lambda b, r, t: (b, r, t, 0)),
        ],
        out_shape=[
            jax.ShapeDtypeStruct((nb, dil, m_len, width), F32),
            jax.ShapeDtypeStruct((nb, dil, m_len, width), F32),
        ],
        scratch_shapes=[
            pltpu.VMEM((m_len + 2 * ATTN_QB, width), BF16),
            pltpu.VMEM((m_len + 2 * ATTN_QB, width), BF16),
        ],
        compiler_params=_cparams(("arbitrary", "arbitrary", "arbitrary")),
        name=f"dilated_attention_g{g}",
    )(qkv, qkv, qkv, bias_tiles)


def _merge_kernel(yh_ref, *rest, dils):
    ng = len(dils)
    o_refs, l_refs = rest[:ng], rest[ng:2 * ng]
    gh_ref, ga_ref, bh_ref, ba_ref, wh_ref, wa_ref, out_ref, ya_ref, tok_ref = rest[2 * ng:]
    tm, aw = ya_ref.shape
    nslab = aw // LANES

    @pl.when(pl.program_id(1) == 0)
    def _():
        for gi, d in enumerate(dils):
            if d == 1:
                continue
            rows = tm // d
            for r in range(d):
                for s in range(nslab):
                    cols = slice(s * LANES, (s + 1) * LANES)
                    tok_ref[2 * gi, s, pl.ds(r, rows, stride=d), :] = o_refs[gi][r, :, cols]
                    tok_ref[2 * gi + 1, s, pl.ds(r, rows, stride=d), :] = l_refs[gi][r, :, cols]
        for s in range(nslab):
            cols = slice(s * LANES, (s + 1) * LANES)
            os_, ls_ = [], []
            for gi, d in enumerate(dils):
                if d == 1:
                    os_.append(o_refs[gi][0, :, cols])
                    ls_.append(l_refs[gi][0, :, cols])
                else:
                    os_.append(tok_ref[2 * gi, s])
                    ls_.append(tok_ref[2 * gi + 1, s])
            mx = functools.reduce(jnp.maximum, ls_)
            es = [jnp.exp(l - mx) for l in ls_]
            inv = 1.0 / functools.reduce(lambda a, b: a + b, es)
            ya = functools.reduce(lambda a, b: a + b, [(e * inv) * o for e, o in zip(es, os_)])
            ya_ref[:, cols] = ya.astype(BF16)

    acc_h = jnp.dot(yh_ref[...], wh_ref[...], preferred_element_type=F32)
    acc_a = jnp.dot(ya_ref[...], wa_ref[...], preferred_element_type=F32)
    sig = lambda v: 1.0 / (1.0 + jnp.exp(-v))
    g_h = sig(gh_ref[...].astype(F32) + bh_ref[...])
    g_a = sig(ga_ref[...].astype(F32) + ba_ref[...])
    out_ref[...] = (g_h * acc_h + g_a * acc_a).astype(out_ref.dtype)


def branch_merge(y_hy, outs, lses, dils, u2, gate_col0, b_gate, w_hy, w_at, seq_len, tm=512):
    t, hw = y_hy.shape
    d = w_hy.shape[1]
    aw = w_at.shape[0]
    tn = _pick_tile(math.gcd(gate_col0, d), 1024)
    gb = gate_col0 // tn
    nj = d // tn
    tps = seq_len // tm
    assert all(tm % (dd * SUBLANES) == 0 for dd in dils)
    grp = [pl.BlockSpec((None, dd, tm // dd, aw), lambda i, j: (i // tps, 0, i % tps, 0)) for dd in dils]
    return pl.pallas_call(
        functools.partial(_merge_kernel, dils=tuple(dils)),
        grid=(t // tm, nj),
        in_specs=[pl.BlockSpec((tm, hw), lambda i, j: (i, 0))] + grp + grp + [
            pl.BlockSpec((tm, tn), lambda i, j: (i, gb + j)),
            pl.BlockSpec((tm, tn), lambda i, j: (i, gb + nj + j)),
            pl.BlockSpec((1, tn), lambda i, j: (0, j)),
            pl.BlockSpec((1, tn), lambda i, j: (0, nj + j)),
            pl.BlockSpec((hw, tn), lambda i, j: (0, j)),
            pl.BlockSpec((aw, tn), lambda i, j: (0, j)),
        ],
        out_specs=pl.BlockSpec((tm, tn), lambda i, j: (i, j)),
        out_shape=jax.ShapeDtypeStruct((t, d), BF16),
        scratch_shapes=[pltpu.VMEM((tm, aw), BF16), pltpu.VMEM((2 * len(dils), aw // LANES, tm, LANES), F32)],
        compiler_params=_cparams(("arbitrary", "arbitrary")),
        name="branch_merge",
    )(y_hy, *outs, *lses, u2, u2, b_gate, b_gate, w_hy, w_at)


def _out_proj_kernel(x_ref, m_ref, gate_ref, w_ref, o_ref):
    acc = jnp.dot(m_ref[...], w_ref[...], preferred_element_type=F32)
    o_ref[...] = x_ref[...] + gate_ref[0] * acc


def out_proj_residual(x, m, mod3, gate_idx, w, seq_len, tm=512, tn=None, name="out_proj_residual"):
    t, d = x.shape
    k = m.shape[1]
    tn = d if tn is None else tn
    nj = d // tn
    tiles_per_seq = seq_len // tm
    return pl.pallas_call(
        _out_proj_kernel,
        grid=(t // tm, nj),
        in_specs=[
            pl.BlockSpec((tm, tn), lambda i, j: (i, j)),
            pl.BlockSpec((tm, k), lambda i, j: (i, 0)),
            pl.BlockSpec((1, 1, tn), lambda i, j: (i // tiles_per_seq, 0, gate_idx * nj + j)),
            pl.BlockSpec((k, tn), lambda i, j: (0, j)),
        ],
        out_specs=pl.BlockSpec((tm, tn), lambda i, j: (i, j)),
        out_shape=jax.ShapeDtypeStruct((t, d), F32),
        input_output_aliases={0: 0},
        compiler_params=_cparams(("arbitrary", "arbitrary")),
        name=name,
    )(x, m, mod3, w)


def _final_norm_kernel(x_ref, g_ref, o_ref):
    x = x_ref[...]
    ms = jnp.mean(x * x, axis=-1, keepdims=True)
    o_ref[...] = x * lax.rsqrt(ms + NORM_EPS) * g_ref[...]


def final_norm(x, g, row0, n_rows, tm=512):
    d = x.shape[1]
    off = row0 // tm
    return pl.pallas_call(
        _final_norm_kernel,
        grid=(n_rows // tm,),
        in_specs=[pl.BlockSpec((tm, d), lambda i: (off + i, 0)), pl.BlockSpec((1, d), lambda i: (0, 0))],
        out_specs=pl.BlockSpec((tm, d), lambda i: (i, 0)),
        out_shape=jax.ShapeDtypeStruct((n_rows, d), F32),
        compiler_params=_cparams(("arbitrary",)),
        name="final_norm",
    )(x, g.reshape(1, d))


def kernel(x_prompt, x_sample, c_prompt, c_sample, ada_w, ada_b, norm1_g, w_in, b_gate, hy_conv_w, hy_conv_b,
           filt_w1, filt_b1, filt_freq1, filt_w2, filt_b2, filt_freq2, filt_w3, hy_bias, rel_bias, w_br_hy,
           w_br_attn, w_out, norm2_g, ffn_up, ffn_conv_w, ffn_conv_b, ffn_down, final_g):
    bp, L, d = x_prompt.shape
    bs = x_sample.shape[0]
    assert x_sample.shape[1] == L
    nb = bp + bs
    depth = ada_w.shape[0]
    hw = hy_bias.shape[2]
    n_groups = len(ATTN_GROUPS)
    attn_w = n_groups * HEADS_PER_GROUP * HEAD_DIM
    hy_cols = 3 * hw
    gate_col0 = hy_cols + 3 * attn_w
    assert w_in.shape[2] == gate_col0 + 2 * d

    x = jnp.concatenate([x_prompt, x_sample], axis=0).reshape(nb * L, d)
    nb_pad = -(-nb // SUBLANES) * SUBLANES
    c_pad = jnp.zeros((nb_pad, d), F32).at[:nb].set(jnp.concatenate([c_prompt, c_sample], axis=0))
    mod = ada_modulation(c_pad, ada_w, ada_b)

    hy_ct = min(HYENA_CT, hw)
    khat = hyena_filter_spectra(L, filt_w1, filt_b1, filt_freq1, filt_w2, filt_b2, filt_freq2, filt_w3, hy_ct)
    n_sides = [(window // 2) // dil for window, dil in ATTN_GROUPS]
    bias_tiles = attention_bias_tiles(rel_bias, n_sides)

    dils = [dil for _, dil in ATTN_GROUPS]
    gw = HEADS_PER_GROUP * HEAD_DIM

    def group_cols(w, g):
        return [w[:, hy_cols + part * attn_w + g * gw: hy_cols + part * attn_w + (g + 1) * gw] for part in range(3)]

    for l in range(depth):
        mod3 = mod[l].reshape(nb_pad, 1, N_MOD * d)
        w_l = w_in[l]
        w_main = jnp.concatenate([w_l[:, :hy_cols], w_l[:, gate_col0:]], axis=1).astype(BF16)
        w_groups = jnp.concatenate([c for g in range(n_groups) for c in group_cols(w_l, g)], axis=1).astype(BF16)
        u, qkv = in_projection(x, norm1_g[l], mod3, 1, 0, w_main, w_groups, dils, nb, L)
        u3 = u.reshape(nb, L, u.shape[1])
        y_hy = hyena_mix(u3, 0, hy_conv_w[l], hy_conv_b[l].reshape(1, hy_cols), khat, l, hy_bias[l], hy_ct)
        outs, lses = [], []
        for g, dil in enumerate(dils):
            o_g, l_g = dilated_group_attention(qkv[g], g, dil, n_sides[g], bias_tiles)
            outs.append(o_g)
            lses.append(l_g)
        merged = branch_merge(y_hy.reshape(nb * L, hw), outs, lses, dils, u, hy_cols, b_gate[l].reshape(1, 2 * d),
                              w_br_hy[l].astype(BF16), w_br_attn[l].astype(BF16), L)
        x = out_proj_residual(x, merged, mod3, 2, w_out[l].astype(BF16), L)
        ff = ffn_down.shape[1]
        act = ffn_up_act(x, norm2_g[l], mod3, 4, 3, interleave_ffn_up(ffn_up[l]).astype(BF16),
                         ffn_conv_w[l], ffn_conv_b[l].reshape(1, ff), L)
        x = out_proj_residual(x, act, mod3, 5, ffn_down[l].astype(BF16), L, tm=min(1024, L),
                              tn=_pick_tile(d, 512), name="ffn_down_residual")

    y_prompt = final_norm(x, final_g, 0, bp * L).reshape(bp, L, d)
    y_sample = final_norm(x, final_g, bp * L, bs * L).reshape(bs, L, d)
    return (y_prompt, y_sample)
```

```python
import functools
import math

import jax
import jax.numpy as jnp
import numpy as np
from jax import lax
from jax.experimental import pallas as pl
from jax.experimental.pallas import tpu as pltpu

F32 = jnp.float32
BF16 = jnp.bfloat16
HIGHEST = lax.Precision.HIGHEST

NORM_EPS = 1e-6
N_MOD = 6
HYENA_ORDER = 2
N_BANDS = 16
FAST_DECAY_PCT = 0.3
SLOW_DECAY_PCT = 1.5
DECAY_TARGET = 1e-2
ATTN_GROUPS = ((128, 1), (512, 4), (2048, 16))
HEADS_PER_GROUP = 8
HEAD_DIM = 64
REL_BUCKETS = 32
REL_MAX_DIST = 1024
NEG_INF = -1e30

LANES = 128
SUBLANES = 8
VMEM_LIMIT = 56 * 1024 * 1024

FFT_N2 = 128
FFT_GROUP = SUBLANES
HYENA_CT = 256
ATTN_QB = 128
ATTN_ROWS_PER_STEP = 1024


def _cparams(sem):
    return pltpu.CompilerParams(dimension_semantics=sem, vmem_limit_bytes=VMEM_LIMIT)


def _pick_tile(n, cap):
    best = None
    for t in range(LANES, min(n, cap) + 1, LANES):
        if n % t == 0:
            best = t
    assert best is not None, (n, cap)
    return best


def _ada_kernel(c_ref, w_ref, b_ref, o_ref):
    c = c_ref[...]
    cs = c * (1.0 / (1.0 + jnp.exp(-c)))
    o_ref[0] = jnp.dot(cs, w_ref[0], preferred_element_type=F32, precision=HIGHEST) + b_ref[0]


def ada_modulation(c_pad, ada_w, ada_b):
    depth, d, n = ada_w.shape
    nb = c_pad.shape[0]
    tn = _pick_tile(n, 1024)
    return pl.pallas_call(
        _ada_kernel,
        grid=(depth, n // tn),
        in_specs=[
            pl.BlockSpec((nb, d), lambda l, j: (0, 0)),
            pl.BlockSpec((1, d, tn), lambda l, j: (l, 0, j)),
            pl.BlockSpec((1, 1, tn), lambda l, j: (l, 0, j)),
        ],
        out_specs=pl.BlockSpec((1, nb, tn), lambda l, j: (l, 0, j)),
        out_shape=jax.ShapeDtypeStruct((depth, nb, n), F32),
        compiler_params=_cparams(("arbitrary", "arbitrary")),
        name="ada_modulation",
    )(c_pad, ada_w, ada_b.reshape(depth, 1, n))


def _normmod_rows(x, g, sc, sh):
    ms = jnp.mean(x * x, axis=-1, keepdims=True)
    return (x * lax.rsqrt(ms + NORM_EPS)) * (g * (1.0 + sc)) + sh


FFN_HALO = 16


def _ffn_up_act_kernel(x_ref, xp_ref, xn_ref, g_ref, sc_ref, sh_ref, w_ref, cw_ref, cb_ref,
                       o_ref, h_ref, *, tiles_per_seq):
    i = pl.program_id(0)
    tm = x_ref.shape[0]

    @pl.when(pl.program_id(1) == 0)
    def _():
        g, sc, sh = g_ref[...], sc_ref[0], sh_ref[0]
        h_ref[0:FFN_HALO] = _normmod_rows(xp_ref[...], g, sc, sh).astype(BF16)
        h_ref[FFN_HALO:FFN_HALO + tm] = _normmod_rows(x_ref[...], g, sc, sh).astype(BF16)
        h_ref[FFN_HALO + tm:] = _normmod_rows(xn_ref[...], g, sc, sh).astype(BF16)

    first = (i % tiles_per_seq) == 0
    last = (i % tiles_per_seq) == tiles_per_seq - 1
    row = lax.broadcasted_iota(jnp.int32, (tm + 2 * FFN_HALO, 1), 0)
    outside = ((row == FFN_HALO - 1) & first) | ((row == FFN_HALO + tm) & last)
    for c in range(o_ref.shape[1] // LANES):
        res = jnp.dot(h_ref[...], w_ref[:, 2 * c * LANES:2 * (c + 1) * LANES], preferred_element_type=F32)
        a = res[FFN_HALO:FFN_HALO + tm, :LANES]
        gt = jnp.where(outside, 0.0, res[:, LANES:])
        cols = slice(c * LANES, (c + 1) * LANES)
        conv = _dwconv3_rows(gt, cw_ref.at[:, cols], cb_ref.at[:, cols])[FFN_HALO:FFN_HALO + tm]
        o_ref[:, cols] = (conv * (1.0 / (1.0 + jnp.exp(-conv))) * a).astype(o_ref.dtype)


def interleave_ffn_up(w_up):
    d, two_ff = w_up.shape
    ff = two_ff // 2
    return w_up.reshape(d, 2, ff // LANES, LANES).transpose(0, 2, 1, 3).reshape(d, two_ff)


def ffn_up_act(x, g, mod3, sc_idx, sh_idx, w_il, conv_w, conv_b, seq_len, tm=512, tn_cap=1536):
    t, d = x.shape
    ff = w_il.shape[1] // 2
    tn = _pick_tile(ff, tn_cap)
    nj = ff // tn
    tps = seq_len // tm
    rb = tm // FFN_HALO
    last_rb = t // FFN_HALO - 1
    return pl.pallas_call(
        functools.partial(_ffn_up_act_kernel, tiles_per_seq=tps),
        grid=(t // tm, nj),
        in_specs=[
            pl.BlockSpec((tm, d), lambda i, j: (i, 0)),
            pl.BlockSpec((FFN_HALO, d), lambda i, j: (jnp.maximum(i * rb - 1, 0), 0)),
            pl.BlockSpec((FFN_HALO, d), lambda i, j: (jnp.minimum((i + 1) * rb, last_rb), 0)),
            pl.BlockSpec((1, d), lambda i, j: (0, 0)),
            pl.BlockSpec((1, 1, d), lambda i, j: (i // tps, 0, sc_idx)),
            pl.BlockSpec((1, 1, d), lambda i, j: (i // tps, 0, sh_idx)),
            pl.BlockSpec((d, 2 * tn), lambda i, j: (0, j)),
            pl.BlockSpec((3, tn), lambda i, j: (0, j)),
            pl.BlockSpec((1, tn), lambda i, j: (0, j)),
        ],
        out_specs=pl.BlockSpec((tm, tn), lambda i, j: (i, j)),
        out_shape=jax.ShapeDtypeStruct((t, ff), BF16),
        scratch_shapes=[pltpu.VMEM((tm + 2 * FFN_HALO, d), BF16)],
        compiler_params=_cparams(("arbitrary", "arbitrary")),
        name="ffn_up_act",
    )(x, x, x, g.reshape(1, d), mod3, mod3, w_il, conv_w, conv_b)


def _in_proj_kernel(x_ref, g_ref, sc_ref, sh_ref, wm_ref, wg_ref, om_ref, *rest, n_main, dils, chunk):
    og_refs = rest[:len(dils)]
    h_ref, perm_ref = rest[len(dils):]
    j = pl.program_id(1)
    tm = h_ref.shape[0]

    @pl.when(j == 0)
    def _():
        h_ref[...] = _normmod_rows(x_ref[...], g_ref[...], sc_ref[0], sh_ref[0]).astype(BF16)

    @pl.when(j < n_main)
    def _():
        om_ref[...] = jnp.dot(h_ref[...], wm_ref[...], preferred_element_type=F32).astype(om_ref.dtype)

    for gi, (o_ref, d) in enumerate(zip(og_refs, dils)):
        @pl.when(j == n_main + gi)
        def _():
            rows = tm // d
            for c0 in range(0, wg_ref.shape[1], chunk):
                res = jnp.dot(h_ref[...], wg_ref[:, c0:c0 + chunk], preferred_element_type=F32)
                if d == 1:
                    o_ref[0, 0, :, c0:c0 + chunk] = res.astype(o_ref.dtype)
                    continue
                for s in range(chunk // LANES):
                    perm_ref[s] = res[:, s * LANES:(s + 1) * LANES]
                for r in range(d):
                    for s in range(chunk // LANES):
                        cols = slice(c0 + s * LANES, c0 + (s + 1) * LANES)
                        o_ref[0, r, :, cols] = perm_ref[s, pl.ds(r, rows, stride=d), :].astype(o_ref.dtype)


def in_projection(x, g, mod3, sc_idx, sh_idx, w_main, w_groups, dils, nb, seq_len, tm=512, tn_cap=1792):
    t, d_model = x.shape
    n_main = w_main.shape[1]
    gw = w_groups.shape[1] // len(dils)
    tn = _pick_tile(n_main, tn_cap)
    nj_main = n_main // tn
    tps = seq_len // tm
    chunk = gw // 3
    assert chunk % LANES == 0 and all(tm % (dd * 16) == 0 for dd in dils)
    kernel = functools.partial(_in_proj_kernel, n_main=nj_main, dils=tuple(dils), chunk=chunk)
    outs = pl.pallas_call(
        kernel,
        grid=(t // tm, nj_main + len(dils)),
        in_specs=[
            pl.BlockSpec((tm, d_model), lambda i, j: (i, 0)),
            pl.BlockSpec((1, d_model), lambda i, j: (0, 0)),
            pl.BlockSpec((1, 1, d_model), lambda i, j: (i // tps, 0, sc_idx)),
            pl.BlockSpec((1, 1, d_model), lambda i, j: (i // tps, 0, sh_idx)),
            pl.BlockSpec((d_model, tn), lambda i, j: (0, jnp.minimum(j, nj_main - 1))),
            pl.BlockSpec((d_model, gw), lambda i, j: (0, jnp.clip(j - nj_main, 0, len(dils) - 1))),
        ],
        out_specs=[pl.BlockSpec((tm, tn), lambda i, j: (i, jnp.minimum(j, nj_main - 1)))] + [
            pl.BlockSpec((1, dd, tm // dd, gw), lambda i, j: (i // tps, 0, i % tps, 0)) for dd in dils],
        out_shape=[jax.ShapeDtypeStruct((t, n_main), BF16)] + [
            jax.ShapeDtypeStruct((nb, dd, seq_len // dd, gw), BF16) for dd in dils],
        scratch_shapes=[pltpu.VMEM((tm, d_model), BF16), pltpu.VMEM((chunk // LANES, tm, LANES), F32)],
        compiler_params=_cparams(("arbitrary", "arbitrary")),
        name="in_projection",
    )(x, g.reshape(1, d_model), mod3, mod3, w_main, w_groups)
    return outs[0], list(outs[1:])


def _fft_dims(seq_len):
    n_fft = 2 * seq_len
    n1 = n_fft // FFT_N2
    assert n1 * FFT_N2 == n_fft and n1 % 2 == 0
    nk1 = n1 // 2 + 1
    pad = -(-nk1 // SUBLANES) * SUBLANES
    return n_fft, n1, nk1, pad


@functools.lru_cache(maxsize=None)
def _fft_constants(seq_len):
    n_fft, n1, nk1, pad = _fft_dims(seq_len)
    n2 = FFT_N2
    k1 = np.arange(nk1)[None, :, None]
    i1 = np.arange(n1)[None, None, :]
    i2 = np.arange(n2)[:, None, None]
    ph = 2.0 * np.pi * ((k1 * (i1 * n2 + i2)) % n_fft) / n_fft
    f1 = np.zeros((n2, 2 * pad, n1), np.float64)
    f1[:, :nk1] = np.cos(ph)
    f1[:, pad:pad + nk1] = -np.sin(ph)
    ck = np.full((nk1,), 2.0)
    ck[0] = 1.0
    ck[-1] = 1.0
    g1 = np.zeros((n2, n1 // 2, 2 * pad), np.float64)
    pht = np.transpose(ph[:, :, :n1 // 2], (0, 2, 1))
    g1[:, :, :nk1] = np.cos(pht) * ck / n_fft
    g1[:, :, pad:pad + nk1] = -np.sin(pht) * ck / n_fft
    a = np.arange(n2)
    ph2 = 2.0 * np.pi * ((a[:, None] * a[None, :]) % n2) / n2
    c2, s2 = np.cos(ph2), np.sin(ph2)
    f2 = np.block([[c2, s2], [-s2, c2]])
    g2 = np.block([[c2, -s2], [s2, c2]])
    return (f1.astype(np.float32), f2.astype(np.float32), g2.astype(np.float32), g1.astype(np.float32))


def _interleaved_block_diag(mats):
    n, r, c = mats.shape
    out = np.zeros((n // FFT_GROUP, r * FFT_GROUP, c * FFT_GROUP), mats.dtype)
    for j in range(FFT_GROUP):
        out[:, j::FFT_GROUP, j::FFT_GROUP] = mats[j::FFT_GROUP]
    return out


def _group_rows(g):
    return pl.ds(pl.multiple_of(g * FFT_GROUP, FFT_GROUP), FFT_GROUP)


def _ld_group(ref, g):
    n_outer = ref.shape[1]
    return jnp.concatenate([ref[h, :, _group_rows(g), :].reshape(n_outer * FFT_GROUP, LANES)
                            for h in range(ref.shape[0])], axis=1)


def _st_group(ref, g, val):
    n_outer = ref.shape[1]
    for h in range(ref.shape[0]):
        ref[h, :, _group_rows(g), :] = val[:, h * LANES:(h + 1) * LANES].reshape(n_outer, FFT_GROUP, LANES)


def _ld_outer(ref, idx):
    return jnp.concatenate([ref[h, idx] for h in range(ref.shape[0])], axis=1)


def _st_outer(ref, idx, val):
    for h in range(ref.shape[0]):
        ref[h, idx] = val[:, h * LANES:(h + 1) * LANES]


def _st_rows(ref, row0, val):
    n = val.shape[0]
    for h in range(ref.shape[0]):
        ref[h, row0 // FFT_N2:(row0 + n) // FFT_N2] = val[:, h * LANES:(h + 1) * LANES].reshape(
            n // FFT_N2, FFT_N2, LANES)


def _fft_stage1(src_ref, f1b_ref, a_ref):
    def body(g, carry):
        a = jnp.dot(f1b_ref[g], _ld_group(src_ref, g).astype(BF16), preferred_element_type=F32)
        _st_group(a_ref, g, a)
        return carry
    lax.fori_loop(0, FFT_N2 // FFT_GROUP, body, 0, unroll=2)


def _fft_stage2_rhs(a_ref, k1, pad):
    return jnp.concatenate([_ld_outer(a_ref, k1), _ld_outer(a_ref, pad + k1)], axis=0).astype(BF16)


def _filter_kernel(feat_ref, featr_ref, w1_ref, b1_ref, fr1_ref, w2_ref, b2_ref, fr2_ref,
                   w3f_ref, w3b_ref, delta_ref, f1_ref, f2_ref,
                   o_ref, hid_ref, k2_ref, a_ref, *, seq_len):
    n_fft, n1, nk1, pad = _fft_dims(seq_len)
    L = seq_len

    @pl.when((pl.program_id(1) == 0) & (pl.program_id(2) == 0))
    def _():
        for d, fref in enumerate((feat_ref, featr_ref)):
            h = jnp.dot(fref[...], w1_ref[0], preferred_element_type=F32, precision=HIGHEST) + b1_ref[0]
            h = jnp.sin(fr1_ref[0] * h)
            h = jnp.dot(h, w2_ref[0], preferred_element_type=F32, precision=HIGHEST) + b2_ref[0]
            hid_ref[d] = jnp.sin(fr2_ref[0] * h)

    delta = delta_ref[...]
    chunk = min(L, 512)
    row = lax.broadcasted_iota(jnp.int32, (chunk, 1), 0)
    l1 = jnp.zeros(delta.shape, F32)
    for d, (w3_ref, fref) in enumerate(((w3f_ref, feat_ref), (w3b_ref, featr_ref))):
        for c in range(L // chunk):
            rows = pl.ds(c * chunk, chunk)
            h = jnp.dot(hid_ref[d, rows, :], w3_ref[0], preferred_element_type=F32, precision=HIGHEST)
            h = h * jnp.exp(-fref[rows, 0:1] * delta)
            if d == 1 and c == 0:
                h = jnp.where(row == 0, 0.0, h)
            l1 = l1 + jnp.sum(jnp.abs(h), axis=0, keepdims=True)
            _st_rows(k2_ref, d * L + c * chunk, h)
    inv = 1.0 / l1
    _fft_stage1(k2_ref, f1_ref, a_ref)

    def body(k1, carry):
        spec = jnp.dot(f2_ref[...], _fft_stage2_rhs(a_ref, k1, pad), preferred_element_type=F32) * inv
        o_ref[0, 0, k1] = spec.astype(o_ref.dtype)
        return carry
    lax.fori_loop(0, nk1, body, 0, unroll=3 if nk1 % 3 == 0 else 1)


def hyena_filter_spectra(seq_len, filt_w1, filt_b1, filt_freq1, filt_w2, filt_b2, filt_freq2, filt_w3, ct):
    depth, n_feat, hid = filt_w1.shape
    hw = filt_w3.shape[2] // (2 * HYENA_ORDER)
    n_fft, n1, nk1, pad = _fft_dims(seq_len)
    L = seq_len
    t = np.linspace(0.0, 1.0, L, dtype=np.float32).astype(np.float64)[:, None]
    pos = np.arange(L, dtype=np.float64)[:, None]
    bands = np.linspace(1e-4, N_BANDS - 1, N_BANDS, dtype=np.float32).astype(np.float64)[None, :]
    ang = (2.0 * math.pi / L) * pos * bands
    feat = np.concatenate([t, np.cos(ang), -np.sin(ang)], axis=-1)
    assert feat.shape[1] == n_feat
    feat_rev = np.concatenate([np.zeros((1, n_feat)), feat[:0:-1]], axis=0)
    n_feat_pad = -(-n_feat // LANES) * LANES
    feat = np.pad(feat, ((0, 0), (0, n_feat_pad - n_feat)))
    feat_rev = np.pad(feat_rev, ((0, 0), (0, n_feat_pad - n_feat)))
    filt_w1 = jnp.pad(filt_w1, ((0, 0), (0, n_feat_pad - n_feat), (0, 0)))
    n_feat = n_feat_pad
    max_decay = math.log(DECAY_TARGET) / FAST_DECAY_PCT
    min_decay = math.log(DECAY_TARGET) / SLOW_DECAY_PCT
    deltas = np.abs(np.linspace(min_decay, max_decay, hw, dtype=np.float32))[None, :]
    f1, f2, _, _ = _fft_constants(seq_len)
    f1 = _interleaved_block_diag(f1)
    nct = hw // ct
    nslab = ct // LANES
    assert L % FFT_N2 == 0 and (L < 512 or L % 512 == 0)

    def w3_spec(direction):
        return pl.BlockSpec((1, hid, ct), lambda l, c, o: (l, 0, (direction * HYENA_ORDER + o) * nct + c))

    def const(shape):
        return pl.BlockSpec(shape, lambda l, c, o: (0,) * len(shape), pipeline_mode=pl.Buffered(1))

    vec = lambda a: a.reshape(depth, 1, hid)
    vspec = pl.BlockSpec((1, 1, hid), lambda l, c, o: (l, 0, 0))
    return pl.pallas_call(
        functools.partial(_filter_kernel, seq_len=seq_len),
        grid=(depth, nct, HYENA_ORDER),
        in_specs=[
            const((L, n_feat)), const((L, n_feat)),
            pl.BlockSpec((1, n_feat, hid), lambda l, c, o: (l, 0, 0)), vspec, vspec,
            pl.BlockSpec((1, hid, hid), lambda l, c, o: (l, 0, 0)), vspec, vspec,
            w3_spec(0), w3_spec(1),
            pl.BlockSpec((1, ct), lambda l, c, o: (0, c)),
            const(f1.shape), const((2 * FFT_N2, 2 * FFT_N2)),
        ],
        out_specs=pl.BlockSpec((1, 1, nk1, 2 * FFT_N2, ct), lambda l, c, o: (l, o, 0, 0, c)),
        out_shape=jax.ShapeDtypeStruct((depth, HYENA_ORDER, nk1, 2 * FFT_N2, hw), BF16),
        scratch_shapes=[
            pltpu.VMEM((2, L, hid), F32),
            pltpu.VMEM((nslab, n1, FFT_N2, LANES), F32),
            pltpu.VMEM((nslab, 2 * pad, FFT_N2, LANES), F32),
        ],
        compiler_params=_cparams(("arbitrary", "arbitrary", "arbitrary")),
        name="hyena_filter_spectra",
    )(jnp.asarray(feat, F32), jnp.asarray(feat_rev, F32), filt_w1, vec(filt_b1), vec(filt_freq1),
      filt_w2, vec(filt_b2), vec(filt_freq2), filt_w3, filt_w3,
      jnp.asarray(deltas, F32), jnp.asarray(f1, BF16), jnp.asarray(f2, BF16))


def _dwconv3_rows(x, w_ref, b_ref, prev_row=None, next_row=None):
    n = x.shape[0]
    row = lax.broadcasted_iota(jnp.int32, (SUBLANES, 1), 0)
    xm = pltpu.roll(x, 1, axis=0)
    xp = pltpu.roll(x, n - 1, axis=0)
    head = jnp.where(row == 0, 0.0 if prev_row is None else prev_row, xm[:SUBLANES])
    tail = jnp.where(row == SUBLANES - 1, 0.0 if next_row is None else next_row, xp[n - SUBLANES:])
    xm = jnp.concatenate([head, xm[SUBLANES:]], axis=0)
    xp = jnp.concatenate([xp[:n - SUBLANES], tail], axis=0)
    return xm * w_ref[0:1, :] + x * w_ref[1:2, :] + xp * w_ref[2:3, :] + b_ref[...]


def _dwconv3_to_slabs(src_ref, w_ref, b_ref, dst_ref, seq_len, chunk):
    halo = 16
    n_chunks = seq_len // chunk
    for c in range(n_chunks):
        r0 = c * chunk
        x = src_ref[0, r0:r0 + chunk, :].astype(F32)
        prev_row = src_ref[0, r0 - halo:r0, :].astype(F32)[halo - 1:halo] if c > 0 else None
        next_row = src_ref[0, r0 + chunk:r0 + chunk + halo, :].astype(F32)[0:1] if c < n_chunks - 1 else None
        _st_rows(dst_ref, r0, _dwconv3_rows(x, w_ref, b_ref, prev_row, next_row))


def _hyena_conv_kernel(zin_ref, ux_ref, wz_ref, bz_ref, wx_ref, bx_ref, hb_ref, kh_ref,
                       f1_ref, f2_ref, g2_ref, g1_ref, o_ref, z_ref, xg_ref, a_ref, y_ref, *, seq_len, first):
    n_fft, n1, nk1, pad = _fft_dims(seq_len)
    L = seq_len
    half = n1 // 2
    nslab = z_ref.shape[0]
    chunk = min(L, 512)

    if first:
        _dwconv3_to_slabs(zin_ref, wz_ref, bz_ref, z_ref, L, chunk)
    else:
        for c in range(L // chunk):
            _st_rows(z_ref, c * chunk, zin_ref[0, c * chunk:(c + 1) * chunk, :])
    _dwconv3_to_slabs(ux_ref, wx_ref, bx_ref, xg_ref, L, chunk)

    _fft_stage1(z_ref, f1_ref, a_ref)

    u2 = 3 if nk1 % 3 == 0 else 1
    n_trips = nk1 // u2

    def spectrum_product(t):
        ys = []
        for j in range(u2):
            k1 = t * u2 + j
            x = jnp.dot(f2_ref[...], _fft_stage2_rhs(a_ref, k1, pad), preferred_element_type=F32)
            kk = kh_ref[0, 0, k1].astype(F32)
            xr, xi = x[:FFT_N2], x[FFT_N2:]
            kr, ki = kk[:FFT_N2], kk[FFT_N2:]
            ys.append(jnp.concatenate([xr * kr - xi * ki, xr * ki + xi * kr], axis=0).astype(BF16))
        return ys

    def inverse_dots(slot):
        return [jnp.dot(g2_ref[...], y_ref[slot, j], preferred_element_type=F32) for j in range(u2)]

    def store_inverse(t, bks):
        for j, bk in enumerate(bks):
            _st_outer(a_ref, t * u2 + j, bk[:FFT_N2])
            _st_outer(a_ref, pad + t * u2 + j, bk[FFT_N2:])

    def store_products(slot, ys):
        for j, y in enumerate(ys):
            y_ref[slot, j] = y

    store_products(0, spectrum_product(0))

    def body2(t, carry):
        ys = spectrum_product(t)
        bks = inverse_dots((t + 1) % 2)
        store_inverse(t - 1, bks)
        store_products(t % 2, ys)
        return carry
    lax.fori_loop(1, n_trips, body2, 0)
    store_inverse(n_trips - 1, inverse_dots((n_trips - 1) % 2))

    bias = hb_ref[...]
    u3 = 2

    def body3(t, carry):
        gs = [t * u3 + j for j in range(u3)]
        ins = [(_ld_group(a_ref, g).astype(BF16), _ld_group(xg_ref, g), _ld_group(z_ref, g)) for g in gs]
        outs = [xg * (jnp.dot(g1_ref[g], bb, preferred_element_type=F32) + bias * z_old)
                for g, (bb, xg, z_old) in zip(gs, ins)]
        for g, z_new in zip(gs, outs):
            _st_group(z_ref, g, z_new)
        return carry
    lax.fori_loop(0, FFT_N2 // FFT_GROUP // u3, body3, 0)

    for c in range(L // chunk):
        blk = jnp.concatenate([z_ref[h, c * chunk // FFT_N2:(c + 1) * chunk // FFT_N2].reshape(chunk, LANES)
                               for h in range(nslab)], axis=1)
        o_ref[0, c * chunk:(c + 1) * chunk, :] = blk.astype(o_ref.dtype)


def hyena_step(zin, zin_col0, u3, x_col0, w_cols, conv_w, conv_b, khat, layer, order, hy_bias_row, first, ct):
    nb, L, _ = u3.shape
    hw = hy_bias_row.shape[1]
    n_fft, n1, nk1, pad = _fft_dims(L)
    half = n1 // 2
    f1, f2, g2, g1 = _fft_constants(L)
    f1 = _interleaved_block_diag(f1[:, :, :half])
    g1 = _interleaved_block_diag(g1)
    nct = hw // ct
    nslab = ct // LANES
    assert zin_col0 % ct == 0 and x_col0 % ct == 0 and w_cols[0] % ct == 0 and w_cols[1] % ct == 0
    last = order == HYENA_ORDER - 1

    def col_spec(rows, col0):
        return pl.BlockSpec((rows, ct), lambda c, b: (0, col0 // ct + c))

    const = lambda shape: pl.BlockSpec(shape, lambda c, b: (0,) * len(shape), pipeline_mode=pl.Buffered(1))
    return pl.pallas_call(
        functools.partial(_hyena_conv_kernel, seq_len=L, first=first),
        grid=(nct, nb),
        in_specs=[
            pl.BlockSpec((1, L, ct), lambda c, b: (b, 0, zin_col0 // ct + c)),
            pl.BlockSpec((1, L, ct), lambda c, b: (b, 0, x_col0 // ct + c)),
            col_spec(3, w_cols[0]), col_spec(1, w_cols[0]), col_spec(3, w_cols[1]), col_spec(1, w_cols[1]),
            pl.BlockSpec((1, ct), lambda c, b: (0, c)),
            pl.BlockSpec((1, 1, nk1, 2 * FFT_N2, ct), lambda c, b: (layer, order, 0, 0, c),
                         pipeline_mode=pl.Buffered(1)),
            const(f1.shape), const((2 * FFT_N2, 2 * FFT_N2)),
            const((2 * FFT_N2, 2 * FFT_N2)), const(g1.shape),
        ],
        out_specs=pl.BlockSpec((1, L, ct), lambda c, b: (b, 0, c)),
        out_shape=jax.ShapeDtypeStruct((nb, L, hw), BF16 if last else F32),
        scratch_shapes=[
            pltpu.VMEM((nslab, half, FFT_N2, LANES), F32),
            pltpu.VMEM((nslab, half, FFT_N2, LANES), F32),
            pltpu.VMEM((nslab, 2 * pad, FFT_N2, LANES), F32),
            pltpu.VMEM((2, 3 if nk1 % 3 == 0 else 1, 2 * FFT_N2, ct), BF16),
        ],
        compiler_params=_cparams(("arbitrary", "arbitrary")),
        name=f"hyena_step{order}",
    )(zin, u3, conv_w, conv_b, conv_w, conv_b, hy_bias_row, khat,
      jnp.asarray(f1, BF16), jnp.asarray(f2, BF16), jnp.asarray(g2, BF16), jnp.asarray(g1, BF16))


def hyena_mix(u3, col0, conv_w, conv_b, khat, layer, hy_bias, ct):
    hw = hy_bias.shape[1]
    z = hyena_step(u3, col0, u3, col0 + hw, (0, hw), conv_w, conv_b, khat, layer, 0, hy_bias[0:1], True, ct)
    return hyena_step(z, 0, u3, col0 + 2 * hw, (0, 2 * hw), conv_w, conv_b, khat, layer, 1, hy_bias[1:2], False, ct)


def _t5_bucket(rel):
    nb = REL_BUCKETS // 2
    ret = (rel > 0).astype(np.int32) * nb
    n = np.abs(rel)
    max_exact = nb // 2
    large = max_exact + (np.log(np.maximum(n, 1) / max_exact) / np.log(REL_MAX_DIST / max_exact)
                         * (nb - max_exact)).astype(np.int32)
    large = np.minimum(large, nb - 1)
    return ret + np.where(n < max_exact, n, large)


ATTN_EDGE_VARIANTS = 4


def _bias_kernel(rb_ref, bkt_ref, o_ref, *, n_sides):
    n_groups = bkt_ref.shape[0]
    qq = lax.broadcasted_iota(jnp.int32, (ATTN_QB, 2 * ATTN_QB), 0)
    kk = lax.broadcasted_iota(jnp.int32, (ATTN_QB, 2 * ATTN_QB), 1)
    for g in range(n_groups):
        n_side = n_sides[g]
        rel = kk - n_side - qq
        band = (rel >= -n_side) & (rel <= n_side)
        before = kk < n_side
        after = kk >= ATTN_QB + n_side
        bkt = bkt_ref[g]
        for h in range(HEADS_PER_GROUP):
            tile = jnp.zeros(bkt.shape, F32)
            for b in range(REL_BUCKETS):
                tile = jnp.where(bkt == b, rb_ref[b, g * HEADS_PER_GROUP + h], tile)
            tile = jnp.where(band, tile, NEG_INF)
            o_ref[g, 0, h] = tile
            o_ref[g, 1, h] = jnp.where(before, NEG_INF, tile)
            o_ref[g, 2, h] = jnp.where(after, NEG_INF, tile)
            o_ref[g, 3, h] = jnp.where(before | after, NEG_INF, tile)


def attention_bias_tiles(rel_bias, n_side_list):
    n_groups = len(ATTN_GROUPS)
    qq = np.arange(ATTN_QB)[:, None]
    kk = np.arange(2 * ATTN_QB)[None, :]
    bkts = []
    for (window, dil), n_side in zip(ATTN_GROUPS, n_side_list):
        j = kk - n_side - qq
        bkts.append(_t5_bucket(j * dil))
    bkt = np.stack(bkts).astype(np.int32)
    return pl.pallas_call(
        functools.partial(_bias_kernel, n_sides=tuple(n_side_list)),
        in_specs=[pl.BlockSpec(memory_space=pltpu.SMEM), pl.BlockSpec(memory_space=pltpu.VMEM)],
        out_specs=pl.BlockSpec(memory_space=pltpu.VMEM),
        out_shape=jax.ShapeDtypeStruct((n_groups, ATTN_EDGE_VARIANTS, HEADS_PER_GROUP, ATTN_QB, 2 * ATTN_QB), F32),
        name="attention_bias_tiles",
    )(rel_bias, jnp.asarray(bkt))


def _attn_kernel(q_ref, k_ref, v_ref, bias_ref, o_ref, l_ref, ks_ref, vs_ref, *, m_len, n_side, tq):
    qt = pl.program_id(2)
    width = HEADS_PER_GROUP * HEAD_DIM
    halo = n_side

    n_cls = q_ref.shape[0]
    n_blk = tq // ATTN_QB

    @pl.when(qt == 0)
    def _():
        zeros = jnp.zeros((halo, width), BF16)
        for c in range(n_cls):
            for ref, src in ((ks_ref, k_ref), (vs_ref, v_ref)):
                ref[c, pl.ds(0, halo), :] = zeros
                ref[c, pl.ds(halo + m_len, 2 * ATTN_QB - halo), :] = jnp.zeros((2 * ATTN_QB - halo, width), BF16)
                ref[c, pl.ds(halo, m_len), :] = src[c]

    lane = lax.broadcasted_iota(jnp.int32, (ATTN_QB, 2 * HEAD_DIM), 1)
    low = lane < HEAD_DIM
    scale = 1.0 / math.sqrt(HEAD_DIM)

    def block(idx, carry):
        c = idx // n_blk
        r0 = pl.multiple_of((idx % n_blk) * ATTN_QB, ATTN_QB)
        q0 = pl.multiple_of(qt * tq + r0, ATTN_QB)
        variant = (q0 == 0).astype(jnp.int32) + 2 * (q0 == m_len - ATTN_QB).astype(jnp.int32)
        for hp in range(HEADS_PER_GROUP // 2):
            cols = pl.ds(hp * 2 * HEAD_DIM, 2 * HEAD_DIM)
            qp = q_ref[c, pl.ds(r0, ATTN_QB), cols] * scale
            kp = ks_ref[c, pl.ds(q0, 2 * ATTN_QB), cols]
            vp = vs_ref[c, pl.ds(q0, 2 * ATTN_QB), cols]
            outs, lses = [], []
            for hh in range(2):
                sel = low if hh == 0 else jnp.logical_not(low)
                qm = jnp.where(sel, qp, jnp.zeros_like(qp))
                s = lax.dot_general(qm, kp, (((1,), (1,)), ((), ())), preferred_element_type=F32)
                s = s + bias_ref[variant, 2 * hp + hh]
                mx = jnp.max(s, axis=-1, keepdims=True)
                p = jnp.exp(s - mx)
                den = jnp.sum(p, axis=-1, keepdims=True)
                pv = jnp.dot(p.astype(BF16), vp, preferred_element_type=F32)
                outs.append(pv / den)
                lses.append(mx + jnp.log(den))
            o_ref[c, pl.ds(r0, ATTN_QB), cols] = jnp.where(low, outs[0], outs[1])
            l_ref[c, pl.ds(r0, ATTN_QB), cols] = jnp.where(low, lses[0], lses[1])
        return carry
    lax.fori_loop(0, n_cls * n_blk, block, 0, unroll=2 if (n_cls * n_blk) % 2 == 0 else 1)


def dilated_group_attention(qkv, g, dil, n_side, bias_tiles):
    nb, d, m_len, _ = qkv.shape
    width = HEADS_PER_GROUP * HEAD_DIM
    assert d == dil and m_len % ATTN_QB == 0 and n_side <= ATTN_QB and n_side % 16 == 0
    tq = min(ATTN_ROWS_PER_STEP, m_len)
    cg = math.gcd(dil, ATTN_ROWS_PER_STEP // tq)
    kernel = functools.partial(_attn_kernel, m_len=m_len, n_side=n_side, tq=tq)
    return pl.pallas_call(
        kernel,
        grid=(nb, dil // cg, m_len // tq),
        in_specs=[
            pl.BlockSpec((None, cg, tq, width), lambda b, r, t: (b, r, t, 0)),
            pl.BlockSpec((None, cg, m_len, width), lambda b, r, t: (b, r, 0, 1)),
            pl.BlockSpec((None, cg, m_len, width), lambda b, r, t: (b, r, 0, 2)),
            pl.BlockSpec((None, ATTN_EDGE_VARIANTS, HEADS_PER_GROUP, ATTN_QB, 2 * ATTN_QB),
                         lambda b, r, t: (g, 0, 0, 0, 0)),
        ],
        out_specs=[
            pl.BlockSpec((None, cg, tq, width), lambda b, r, t: (b, r, t, 0)),
            pl.BlockSpec((None, cg, tq, width), lambda b, r, t: (b, r, t, 0)),
        ],
        out_shape=[
            jax.ShapeDtypeStruct((nb, dil, m_len, width), F32),
            jax.ShapeDtypeStruct((nb, dil, m_len, width), F32),
        ],
        scratch_shapes=[
            pltpu.VMEM((cg, m_len + 2 * ATTN_QB, width), BF16),
            pltpu.VMEM((cg, m_len + 2 * ATTN_QB, width), BF16),
        ],
        compiler_params=_cparams(("arbitrary", "arbitrary", "arbitrary")),
        name=f"dilated_attention_g{g}",
    )(qkv, qkv, qkv, bias_tiles)


def _merge_kernel(yh_ref, *rest, dils):
    ng = len(dils)
    o_refs, l_refs = rest[:ng], rest[ng:2 * ng]
    gh_ref, ga_ref, bh_ref, ba_ref, wh_ref, wa_ref, out_ref, ya_ref, tok_ref = rest[2 * ng:]
    tm, aw = ya_ref.shape
    nslab = aw // LANES

    @pl.when(pl.program_id(1) == 0)
    def _():
        for gi, d in enumerate(dils):
            if d == 1:
                continue
            rows = tm // d
            for r in range(d):
                for s in range(nslab):
                    cols = slice(s * LANES, (s + 1) * LANES)
                    tok_ref[2 * gi, s, pl.ds(r, rows, stride=d), :] = o_refs[gi][r, :, cols]
                    tok_ref[2 * gi + 1, s, pl.ds(r, rows, stride=d), :] = l_refs[gi][r, :, cols]
        for s in range(nslab):
            cols = slice(s * LANES, (s + 1) * LANES)
            os_, ls_ = [], []
            for gi, d in enumerate(dils):
                if d == 1:
                    os_.append(o_refs[gi][0, :, cols])
                    ls_.append(l_refs[gi][0, :, cols])
                else:
                    os_.append(tok_ref[2 * gi, s])
                    ls_.append(tok_ref[2 * gi + 1, s])
            mx = functools.reduce(jnp.maximum, ls_)
            es = [jnp.exp(l - mx) for l in ls_]
            inv = 1.0 / functools.reduce(lambda a, b: a + b, es)
            ya = functools.reduce(lambda a, b: a + b, [(e * inv) * o for e, o in zip(es, os_)])
            ya_ref[:, cols] = ya.astype(BF16)

    acc_h = jnp.dot(yh_ref[...], wh_ref[...], preferred_element_type=F32)
    acc_a = jnp.dot(ya_ref[...], wa_ref[...], preferred_element_type=F32)
    sig = lambda v: 1.0 / (1.0 + jnp.exp(-v))
    g_h = sig(gh_ref[...].astype(F32) + bh_ref[...])
    g_a = sig(ga_ref[...].astype(F32) + ba_ref[...])
    out_ref[...] = (g_h * acc_h + g_a * acc_a).astype(out_ref.dtype)


def branch_merge(y_hy, outs, lses, dils, u2, gate_col0, b_gate, w_hy, w_at, seq_len, tm=512):
    t, hw = y_hy.shape
    d = w_hy.shape[1]
    aw = w_at.shape[0]
    tn = _pick_tile(math.gcd(gate_col0, d), 1024)
    gb = gate_col0 // tn
    nj = d // tn
    tps = seq_len // tm
    assert all(tm % (dd * SUBLANES) == 0 for dd in dils)
    grp = [pl.BlockSpec((None, dd, tm // dd, aw), lambda i, j: (i // tps, 0, i % tps, 0)) for dd in dils]
    return pl.pallas_call(
        functools.partial(_merge_kernel, dils=tuple(dils)),
        grid=(t // tm, nj),
        in_specs=[pl.BlockSpec((tm, hw), lambda i, j: (i, 0))] + grp + grp + [
            pl.BlockSpec((tm, tn), lambda i, j: (i, gb + j)),
            pl.BlockSpec((tm, tn), lambda i, j: (i, gb + nj + j)),
            pl.BlockSpec((1, tn), lambda i, j: (0, j)),
            pl.BlockSpec((1, tn), lambda i, j: (0, nj + j)),
            pl.BlockSpec((hw, tn), lambda i, j: (0, j)),
            pl.BlockSpec((aw, tn), lambda i, j: (0, j)),
        ],
        out_specs=pl.BlockSpec((tm, tn), lambda i, j: (i, j)),
        out_shape=jax.ShapeDtypeStruct((t, d), BF16),
        scratch_shapes=[pltpu.VMEM((tm, aw), BF16), pltpu.VMEM((2 * len(dils), aw // LANES, tm, LANES), F32)],
        compiler_params=_cparams(("arbitrary", "arbitrary")),
        name="branch_merge",
    )(y_hy, *outs, *lses, u2, u2, b_gate, b_gate, w_hy, w_at)


def _out_proj_kernel(x_ref, m_ref, gate_ref, w_ref, o_ref):
    acc = jnp.dot(m_ref[...], w_ref[...], preferred_element_type=F32)
    o_ref[...] = x_ref[...] + gate_ref[0] * acc


def out_proj_residual(x, m, mod3, gate_idx, w, seq_len, tm=512, tn=None, name="out_proj_residual"):
    t, d = x.shape
    k = m.shape[1]
    tn = d if tn is None else tn
    nj = d // tn
    tiles_per_seq = seq_len // tm
    return pl.pallas_call(
        _out_proj_kernel,
        grid=(t // tm, nj),
        in_specs=[
            pl.BlockSpec((tm, tn), lambda i, j: (i, j)),
            pl.BlockSpec((tm, k), lambda i, j: (i, 0)),
            pl.BlockSpec((1, 1, tn), lambda i, j: (i // tiles_per_seq, 0, gate_idx * nj + j)),
            pl.BlockSpec((k, tn), lambda i, j: (0, j)),
        ],
        out_specs=pl.BlockSpec((tm, tn), lambda i, j: (i, j)),
        out_shape=jax.ShapeDtypeStruct((t, d), F32),
        input_output_aliases={0: 0},
        compiler_params=_cparams(("arbitrary", "arbitrary")),
        name=name,
    )(x, m, mod3, w)


def _final_norm_kernel(x_ref, g_ref, o_ref):
    x = x_ref[...]
    ms = jnp.mean(x * x, axis=-1, keepdims=True)
    o_ref[...] = x * lax.rsqrt(ms + NORM_EPS) * g_ref[...]


def final_norm(x, g, row0, n_rows, tm=512):
    d = x.shape[1]
    off = row0 // tm
    return pl.pallas_call(
        _final_norm_kernel,
        grid=(n_rows // tm,),
        in_specs=[pl.BlockSpec((tm, d), lambda i: (off + i, 0)), pl.BlockSpec((1, d), lambda i: (0, 0))],
        out_specs=pl.BlockSpec((tm, d), lambda i: (i, 0)),
        out_shape=jax.ShapeDtypeStruct((n_rows, d), F32),
        compiler_params=_cparams(("arbitrary",)),
        name="final_norm",
    )(x, g.reshape(1, d))


def kernel(x_prompt, x_sample, c_prompt, c_sample, ada_w, ada_b, norm1_g, w_in, b_gate, hy_conv_w, hy_conv_b,
           filt_w1, filt_b1, filt_freq1, filt_w2, filt_b2, filt_freq2, filt_w3, hy_bias, rel_bias, w_br_hy,
           w_br_attn, w_out, norm2_g, ffn_up, ffn_conv_w, ffn_conv_b, ffn_down, final_g):
    bp, L, d = x_prompt.shape
    bs = x_sample.shape[0]
    assert x_sample.shape[1] == L
    nb = bp + bs
    depth = ada_w.shape[0]
    hw = hy_bias.shape[2]
    n_groups = len(ATTN_GROUPS)
    attn_w = n_groups * HEADS_PER_GROUP * HEAD_DIM
    hy_cols = 3 * hw
    gate_col0 = hy_cols + 3 * attn_w
    assert w_in.shape[2] == gate_col0 + 2 * d

    x = jnp.concatenate([x_prompt, x_sample], axis=0).reshape(nb * L, d)
    nb_pad = -(-nb // SUBLANES) * SUBLANES
    c_pad = jnp.zeros((nb_pad, d), F32).at[:nb].set(jnp.concatenate([c_prompt, c_sample], axis=0))
    mod = ada_modulation(c_pad, ada_w, ada_b)

    hy_ct = min(HYENA_CT, hw)
    khat = hyena_filter_spectra(L, filt_w1, filt_b1, filt_freq1, filt_w2, filt_b2, filt_freq2, filt_w3, hy_ct)
    n_sides = [(window // 2) // dil for window, dil in ATTN_GROUPS]
    bias_tiles = attention_bias_tiles(rel_bias, n_sides)

    dils = [dil for _, dil in ATTN_GROUPS]
    gw = HEADS_PER_GROUP * HEAD_DIM

    def group_cols(w, g):
        return [w[:, hy_cols + part * attn_w + g * gw: hy_cols + part * attn_w + (g + 1) * gw] for part in range(3)]

    for l in range(depth):
        mod3 = mod[l].reshape(nb_pad, 1, N_MOD * d)
        w_l = w_in[l]
        w_main = jnp.concatenate([w_l[:, :hy_cols], w_l[:, gate_col0:]], axis=1).astype(BF16)
        w_groups = jnp.concatenate([c for g in range(n_groups) for c in group_cols(w_l, g)], axis=1).astype(BF16)
        u, qkv = in_projection(x, norm1_g[l], mod3, 1, 0, w_main, w_groups, dils, nb, L)
        u3 = u.reshape(nb, L, u.shape[1])
        y_hy = hyena_mix(u3, 0, hy_conv_w[l], hy_conv_b[l].reshape(1, hy_cols), khat, l, hy_bias[l], hy_ct)
        outs, lses = [], []
        for g, dil in enumerate(dils):
            o_g, l_g = dilated_group_attention(qkv[g], g, dil, n_sides[g], bias_tiles)
            outs.append(o_g)
            lses.append(l_g)
        merged = branch_merge(y_hy.reshape(nb * L, hw), outs, lses, dils, u, hy_cols, b_gate[l].reshape(1, 2 * d),
                              w_br_hy[l].astype(BF16), w_br_attn[l].astype(BF16), L)
        x = out_proj_residual(x, merged, mod3, 2, w_out[l].astype(BF16), L)
        ff = ffn_down.shape[1]
        act = ffn_up_act(x, norm2_g[l], mod3, 4, 3, interleave_ffn_up(ffn_up[l].astype(BF16)),
                         ffn_conv_w[l], ffn_conv_b[l].reshape(1, ff), L)
        x = out_proj_residual(x, act, mod3, 5, ffn_down[l].astype(BF16), L, tm=min(1024, L),
                              tn=_pick_tile(d, 512), name="ffn_down_residual")

    y_prompt = final_norm(x, final_g, 0, bp * L).reshape(bp, L, d)
    y_sample = final_norm(x, final_g, bp * L, bs * L).reshape(bs, L, d)
    return (y_prompt, y_sample)
```

```python
import functools
import math

import jax
import jax.numpy as jnp
import numpy as np
from jax import lax
from jax.experimental import pallas as pl
from jax.experimental.pallas import tpu as pltpu

F32 = jnp.float32
BF16 = jnp.bfloat16
HIGHEST = lax.Precision.HIGHEST

NORM_EPS = 1e-6
N_MOD = 6
HYENA_ORDER = 2
N_BANDS = 16
FAST_DECAY_PCT = 0.3
SLOW_DECAY_PCT = 1.5
DECAY_TARGET = 1e-2
ATTN_GROUPS = ((128, 1), (512, 4), (2048, 16))
HEADS_PER_GROUP = 8
HEAD_DIM = 64
REL_BUCKETS = 32
REL_MAX_DIST = 1024
NEG_INF = -1e30

LANES = 128
SUBLANES = 8
VMEM_LIMIT = 56 * 1024 * 1024

FFT_N2 = 128
FFT_GROUP = SUBLANES
HYENA_CT = 256
ATTN_QB = 128
ATTN_ROWS_PER_STEP = 1024


def _cparams(sem):
    return pltpu.CompilerParams(dimension_semantics=sem, vmem_limit_bytes=VMEM_LIMIT)


def _pick_tile(n, cap):
    best = None
    for t in range(LANES, min(n, cap) + 1, LANES):
        if n % t == 0:
            best = t
    assert best is not None, (n, cap)
    return best


def _ada_kernel(c_ref, w_ref, b_ref, o_ref):
    c = c_ref[...]
    cs = c * (1.0 / (1.0 + jnp.exp(-c)))
    o_ref[0] = jnp.dot(cs, w_ref[0], preferred_element_type=F32, precision=HIGHEST) + b_ref[0]


def ada_modulation(c_pad, ada_w, ada_b):
    depth, d, n = ada_w.shape
    nb = c_pad.shape[0]
    tn = _pick_tile(n, 1024)
    return pl.pallas_call(
        _ada_kernel,
        grid=(depth, n // tn),
        in_specs=[
            pl.BlockSpec((nb, d), lambda l, j: (0, 0)),
            pl.BlockSpec((1, d, tn), lambda l, j: (l, 0, j)),
            pl.BlockSpec((1, 1, tn), lambda l, j: (l, 0, j)),
        ],
        out_specs=pl.BlockSpec((1, nb, tn), lambda l, j: (l, 0, j)),
        out_shape=jax.ShapeDtypeStruct((depth, nb, n), F32),
        compiler_params=_cparams(("arbitrary", "arbitrary")),
        name="ada_modulation",
    )(c_pad, ada_w, ada_b.reshape(depth, 1, n))


def _normmod_rows(x, g, sc, sh):
    ms = jnp.mean(x * x, axis=-1, keepdims=True)
    return (x * lax.rsqrt(ms + NORM_EPS)) * (g * (1.0 + sc)) + sh


FFN_HALO = 16


def _ffn_up_act_kernel(x_ref, xp_ref, xn_ref, g_ref, sc_ref, sh_ref, w_ref, cw_ref, cb_ref,
                       o_ref, h_ref, *, tiles_per_seq):
    i = pl.program_id(0)
    tm = x_ref.shape[0]

    @pl.when(pl.program_id(1) == 0)
    def _():
        g, sc, sh = g_ref[...], sc_ref[0], sh_ref[0]
        h_ref[0:FFN_HALO] = _normmod_rows(xp_ref[...], g, sc, sh).astype(BF16)
        h_ref[FFN_HALO:FFN_HALO + tm] = _normmod_rows(x_ref[...], g, sc, sh).astype(BF16)
        h_ref[FFN_HALO + tm:] = _normmod_rows(xn_ref[...], g, sc, sh).astype(BF16)

    first = (i % tiles_per_seq) == 0
    last = (i % tiles_per_seq) == tiles_per_seq - 1
    row = lax.broadcasted_iota(jnp.int32, (tm + 2 * FFN_HALO, 1), 0)
    outside = ((row == FFN_HALO - 1) & first) | ((row == FFN_HALO + tm) & last)
    for c in range(o_ref.shape[1] // LANES):
        res = jnp.dot(h_ref[...], w_ref[:, 2 * c * LANES:2 * (c + 1) * LANES], preferred_element_type=F32)
        a = res[FFN_HALO:FFN_HALO + tm, :LANES]
        gt = jnp.where(outside, 0.0, res[:, LANES:])
        cols = slice(c * LANES, (c + 1) * LANES)
        conv = _dwconv3_rows(gt, cw_ref.at[:, cols], cb_ref.at[:, cols])[FFN_HALO:FFN_HALO + tm]
        half = 0.5 * conv
        o_ref[:, cols] = (half * (1.0 + jnp.tanh(half)) * a).astype(o_ref.dtype)


def interleave_ffn_up(w_up):
    d, two_ff = w_up.shape
    ff = two_ff // 2
    return w_up.reshape(d, 2, ff // LANES, LANES).transpose(0, 2, 1, 3).reshape(d, two_ff)


def ffn_up_act(x, g, mod3, sc_idx, sh_idx, w_il, conv_w, conv_b, seq_len, tm=512, tn_cap=1536):
    t, d = x.shape
    ff = w_il.shape[1] // 2
    tn = _pick_tile(ff, tn_cap)
    nj = ff // tn
    tps = seq_len // tm
    rb = tm // FFN_HALO
    last_rb = t // FFN_HALO - 1
    return pl.pallas_call(
        functools.partial(_ffn_up_act_kernel, tiles_per_seq=tps),
        grid=(t // tm, nj),
        in_specs=[
            pl.BlockSpec((tm, d), lambda i, j: (i, 0)),
            pl.BlockSpec((FFN_HALO, d), lambda i, j: (jnp.maximum(i * rb - 1, 0), 0)),
            pl.BlockSpec((FFN_HALO, d), lambda i, j: (jnp.minimum((i + 1) * rb, last_rb), 0)),
            pl.BlockSpec((1, d), lambda i, j: (0, 0)),
            pl.BlockSpec((1, 1, d), lambda i, j: (i // tps, 0, sc_idx)),
            pl.BlockSpec((1, 1, d), lambda i, j: (i // tps, 0, sh_idx)),
            pl.BlockSpec((d, 2 * tn), lambda i, j: (0, j)),
            pl.BlockSpec((3, tn), lambda i, j: (0, j)),
            pl.BlockSpec((1, tn), lambda i, j: (0, j)),
        ],
        out_specs=pl.BlockSpec((tm, tn), lambda i, j: (i, j)),
        out_shape=jax.ShapeDtypeStruct((t, ff), BF16),
        scratch_shapes=[pltpu.VMEM((tm + 2 * FFN_HALO, d), BF16)],
        compiler_params=_cparams(("arbitrary", "arbitrary")),
        name="ffn_up_act",
    )(x, x, x, g.reshape(1, d), mod3, mod3, w_il, conv_w, conv_b)


def _in_proj_kernel(x_ref, g_ref, sc_ref, sh_ref, wm_ref, wg_ref, om_ref, *rest, n_main, dils, chunk):
    og_refs = rest[:len(dils)]
    h_ref, perm_ref = rest[len(dils):]
    j = pl.program_id(1)
    tm = h_ref.shape[0]

    @pl.when(j == 0)
    def _():
        h_ref[...] = _normmod_rows(x_ref[...], g_ref[...], sc_ref[0], sh_ref[0]).astype(BF16)

    @pl.when(j < n_main)
    def _():
        om_ref[...] = jnp.dot(h_ref[...], wm_ref[...], preferred_element_type=F32).astype(om_ref.dtype)

    for gi, (o_ref, d) in enumerate(zip(og_refs, dils)):
        @pl.when(j == n_main + gi)
        def _():
            rows = tm // d
            for c0 in range(0, wg_ref.shape[1], chunk):
                res = jnp.dot(h_ref[...], wg_ref[:, c0:c0 + chunk], preferred_element_type=F32)
                if d == 1:
                    o_ref[0, 0, :, c0:c0 + chunk] = res.astype(o_ref.dtype)
                    continue
                for s in range(chunk // LANES):
                    perm_ref[s] = res[:, s * LANES:(s + 1) * LANES]
                for r in range(d):
                    for s in range(chunk // LANES):
                        cols = slice(c0 + s * LANES, c0 + (s + 1) * LANES)
                        o_ref[0, r, :, cols] = perm_ref[s, pl.ds(r, rows, stride=d), :].astype(o_ref.dtype)


def in_projection(x, g, mod3, sc_idx, sh_idx, w_main, w_groups, dils, nb, seq_len, tm=512, tn_cap=1792):
    t, d_model = x.shape
    n_main = w_main.shape[1]
    gw = w_groups.shape[1] // len(dils)
    tn = _pick_tile(n_main, tn_cap)
    nj_main = n_main // tn
    tps = seq_len // tm
    chunk = gw // 3
    assert chunk % LANES == 0 and all(tm % (dd * 16) == 0 for dd in dils)
    kernel = functools.partial(_in_proj_kernel, n_main=nj_main, dils=tuple(dils), chunk=chunk)
    outs = pl.pallas_call(
        kernel,
        grid=(t // tm, nj_main + len(dils)),
        in_specs=[
            pl.BlockSpec((tm, d_model), lambda i, j: (i, 0)),
            pl.BlockSpec((1, d_model), lambda i, j: (0, 0)),
            pl.BlockSpec((1, 1, d_model), lambda i, j: (i // tps, 0, sc_idx)),
            pl.BlockSpec((1, 1, d_model), lambda i, j: (i // tps, 0, sh_idx)),
            pl.BlockSpec((d_model, tn), lambda i, j: (0, jnp.minimum(j, nj_main - 1))),
            pl.BlockSpec((d_model, gw), lambda i, j: (0, jnp.clip(j - nj_main, 0, len(dils) - 1))),
        ],
        out_specs=[pl.BlockSpec((tm, tn), lambda i, j: (i, jnp.minimum(j, nj_main - 1)))] + [
            pl.BlockSpec((1, dd, tm // dd, gw), lambda i, j: (i // tps, 0, i % tps, 0)) for dd in dils],
        out_shape=[jax.ShapeDtypeStruct((t, n_main), BF16)] + [
            jax.ShapeDtypeStruct((nb, dd, seq_len // dd, gw), BF16) for dd in dils],
        scratch_shapes=[pltpu.VMEM((tm, d_model), BF16), pltpu.VMEM((chunk // LANES, tm, LANES), F32)],
        compiler_params=_cparams(("arbitrary", "arbitrary")),
        name="in_projection",
    )(x, g.reshape(1, d_model), mod3, mod3, w_main, w_groups)
    return outs[0], list(outs[1:])


def _fft_dims(seq_len):
    n_fft = 2 * seq_len
    n1 = n_fft // FFT_N2
    assert n1 * FFT_N2 == n_fft and n1 % 2 == 0
    nk1 = n1 // 2 + 1
    pad = -(-nk1 // SUBLANES) * SUBLANES
    return n_fft, n1, nk1, pad


@functools.lru_cache(maxsize=None)
def _fft_constants(seq_len):
    n_fft, n1, nk1, pad = _fft_dims(seq_len)
    n2 = FFT_N2
    k1 = np.arange(nk1)[None, :, None]
    i1 = np.arange(n1)[None, None, :]
    i2 = np.arange(n2)[:, None, None]
    ph = 2.0 * np.pi * ((k1 * (i1 * n2 + i2)) % n_fft) / n_fft
    f1 = np.zeros((n2, 2 * pad, n1), np.float64)
    f1[:, :nk1] = np.cos(ph)
    f1[:, pad:pad + nk1] = -np.sin(ph)
    ck = np.full((nk1,), 2.0)
    ck[0] = 1.0
    ck[-1] = 1.0
    g1 = np.zeros((n2, n1 // 2, 2 * pad), np.float64)
    pht = np.transpose(ph[:, :, :n1 // 2], (0, 2, 1))
    g1[:, :, :nk1] = np.cos(pht) * ck / n_fft
    g1[:, :, pad:pad + nk1] = -np.sin(pht) * ck / n_fft
    a = np.arange(n2)
    ph2 = 2.0 * np.pi * ((a[:, None] * a[None, :]) % n2) / n2
    c2, s2 = np.cos(ph2), np.sin(ph2)
    f2 = np.block([[c2, s2], [-s2, c2]])
    g2 = np.block([[c2, -s2], [s2, c2]])
    return (f1.astype(np.float32), f2.astype(np.float32), g2.astype(np.float32), g1.astype(np.float32))


def _interleaved_block_diag(mats):
    n, r, c = mats.shape
    out = np.zeros((n // FFT_GROUP, r * FFT_GROUP, c * FFT_GROUP), mats.dtype)
    for j in range(FFT_GROUP):
        out[:, j::FFT_GROUP, j::FFT_GROUP] = mats[j::FFT_GROUP]
    return out


def _group_rows(g):
    return pl.ds(pl.multiple_of(g * FFT_GROUP, FFT_GROUP), FFT_GROUP)


def _ld_group(ref, g):
    n_outer = ref.shape[1]
    return jnp.concatenate([ref[h, :, _group_rows(g), :].reshape(n_outer * FFT_GROUP, LANES)
                            for h in range(ref.shape[0])], axis=1)


def _st_group(ref, g, val):
    n_outer = ref.shape[1]
    for h in range(ref.shape[0]):
        ref[h, :, _group_rows(g), :] = val[:, h * LANES:(h + 1) * LANES].reshape(n_outer, FFT_GROUP, LANES)


def _ld_outer(ref, idx):
    return jnp.concatenate([ref[h, idx] for h in range(ref.shape[0])], axis=1)


def _st_outer(ref, idx, val):
    for h in range(ref.shape[0]):
        ref[h, idx] = val[:, h * LANES:(h + 1) * LANES]


def _st_rows(ref, row0, val):
    n = val.shape[0]
    for h in range(ref.shape[0]):
        ref[h, row0 // FFT_N2:(row0 + n) // FFT_N2] = val[:, h * LANES:(h + 1) * LANES].reshape(
            n // FFT_N2, FFT_N2, LANES)


def _fft_stage1(src_ref, f1b_ref, a_ref):
    def body(g, carry):
        a = jnp.dot(f1b_ref[g], _ld_group(src_ref, g).astype(BF16), preferred_element_type=F32)
        _st_group(a_ref, g, a)
        return carry
    lax.fori_loop(0, FFT_N2 // FFT_GROUP, body, 0, unroll=2)


def _fft_stage2_rhs(a_ref, k1, pad):
    return jnp.concatenate([_ld_outer(a_ref, k1), _ld_outer(a_ref, pad + k1)], axis=0).astype(BF16)


def _filter_kernel(feat_ref, featr_ref, w1_ref, b1_ref, fr1_ref, w2_ref, b2_ref, fr2_ref,
                   w3f_ref, w3b_ref, delta_ref, f1_ref, f2_ref,
                   o_ref, hid_ref, k2_ref, a_ref, *, seq_len):
    n_fft, n1, nk1, pad = _fft_dims(seq_len)
    L = seq_len

    @pl.when((pl.program_id(1) == 0) & (pl.program_id(2) == 0))
    def _():
        for d, fref in enumerate((feat_ref, featr_ref)):
            h = jnp.dot(fref[...], w1_ref[0], preferred_element_type=F32, precision=HIGHEST) + b1_ref[0]
            h = jnp.sin(fr1_ref[0] * h)
            h = jnp.dot(h, w2_ref[0], preferred_element_type=F32, precision=HIGHEST) + b2_ref[0]
            hid_ref[d] = jnp.sin(fr2_ref[0] * h)

    delta = delta_ref[...]
    chunk = min(L, 512)
    row = lax.broadcasted_iota(jnp.int32, (chunk, 1), 0)
    l1 = jnp.zeros(delta.shape, F32)
    for d, (w3_ref, fref) in enumerate(((w3f_ref, feat_ref), (w3b_ref, featr_ref))):
        for c in range(L // chunk):
            rows = pl.ds(c * chunk, chunk)
            h = jnp.dot(hid_ref[d, rows, :], w3_ref[0], preferred_element_type=F32, precision=HIGHEST)
            h = h * jnp.exp(-fref[rows, 0:1] * delta)
            if d == 1 and c == 0:
                h = jnp.where(row == 0, 0.0, h)
            l1 = l1 + jnp.sum(jnp.abs(h), axis=0, keepdims=True)
            _st_rows(k2_ref, d * L + c * chunk, h)
    inv = 1.0 / l1
    _fft_stage1(k2_ref, f1_ref, a_ref)

    def body(k1, carry):
        spec = jnp.dot(f2_ref[...], _fft_stage2_rhs(a_ref, k1, pad), preferred_element_type=F32) * inv
        o_ref[0, 0, k1] = spec.astype(o_ref.dtype)
        return carry
    lax.fori_loop(0, nk1, body, 0, unroll=3 if nk1 % 3 == 0 else 1)


def hyena_filter_spectra(seq_len, filt_w1, filt_b1, filt_freq1, filt_w2, filt_b2, filt_freq2, filt_w3, ct):
    depth, n_feat, hid = filt_w1.shape
    hw = filt_w3.shape[2] // (2 * HYENA_ORDER)
    n_fft, n1, nk1, pad = _fft_dims(seq_len)
    L = seq_len
    t = np.linspace(0.0, 1.0, L, dtype=np.float32).astype(np.float64)[:, None]
    pos = np.arange(L, dtype=np.float64)[:, None]
    bands = np.linspace(1e-4, N_BANDS - 1, N_BANDS, dtype=np.float32).astype(np.float64)[None, :]
    ang = (2.0 * math.pi / L) * pos * bands
    feat = np.concatenate([t, np.cos(ang), -np.sin(ang)], axis=-1)
    assert feat.shape[1] == n_feat
    feat_rev = np.concatenate([np.zeros((1, n_feat)), feat[:0:-1]], axis=0)
    n_feat_pad = -(-n_feat // LANES) * LANES
    feat = np.pad(feat, ((0, 0), (0, n_feat_pad - n_feat)))
    feat_rev = np.pad(feat_rev, ((0, 0), (0, n_feat_pad - n_feat)))
    filt_w1 = jnp.pad(filt_w1, ((0, 0), (0, n_feat_pad - n_feat), (0, 0)))
    n_feat = n_feat_pad
    max_decay = math.log(DECAY_TARGET) / FAST_DECAY_PCT
    min_decay = math.log(DECAY_TARGET) / SLOW_DECAY_PCT
    deltas = np.abs(np.linspace(min_decay, max_decay, hw, dtype=np.float32))[None, :]
    f1, f2, _, _ = _fft_constants(seq_len)
    f1 = _interleaved_block_diag(f1)
    nct = hw // ct
    nslab = ct // LANES
    assert L % FFT_N2 == 0 and (L < 512 or L % 512 == 0)

    def w3_spec(direction):
        return pl.BlockSpec((1, hid, ct), lambda l, c, o: (l, 0, (direction * HYENA_ORDER + o) * nct + c))

    def const(shape):
        return pl.BlockSpec(shape, lambda l, c, o: (0,) * len(shape), pipeline_mode=pl.Buffered(1))

    vec = lambda a: a.reshape(depth, 1, hid)
    vspec = pl.BlockSpec((1, 1, hid), lambda l, c, o: (l, 0, 0))
    return pl.pallas_call(
        functools.partial(_filter_kernel, seq_len=seq_len),
        grid=(depth, nct, HYENA_ORDER),
        in_specs=[
            const((L, n_feat)), const((L, n_feat)),
            pl.BlockSpec((1, n_feat, hid), lambda l, c, o: (l, 0, 0)), vspec, vspec,
            pl.BlockSpec((1, hid, hid), lambda l, c, o: (l, 0, 0)), vspec, vspec,
            w3_spec(0), w3_spec(1),
            pl.BlockSpec((1, ct), lambda l, c, o: (0, c)),
            const(f1.shape), const((2 * FFT_N2, 2 * FFT_N2)),
        ],
        out_specs=pl.BlockSpec((1, 1, nk1, 2 * FFT_N2, ct), lambda l, c, o: (l, o, 0, 0, c)),
        out_shape=jax.ShapeDtypeStruct((depth, HYENA_ORDER, nk1, 2 * FFT_N2, hw), BF16),
        scratch_shapes=[
            pltpu.VMEM((2, L, hid), F32),
            pltpu.VMEM((nslab, n1, FFT_N2, LANES), F32),
            pltpu.VMEM((nslab, 2 * pad, FFT_N2, LANES), F32),
        ],
        compiler_params=_cparams(("arbitrary", "arbitrary", "arbitrary")),
        name="hyena_filter_spectra",
    )(jnp.asarray(feat, F32), jnp.asarray(feat_rev, F32), filt_w1, vec(filt_b1), vec(filt_freq1),
      filt_w2, vec(filt_b2), vec(filt_freq2), filt_w3, filt_w3,
      jnp.asarray(deltas, F32), jnp.asarray(f1, BF16), jnp.asarray(f2, BF16))


def _dwconv3_rows(x, w_ref, b_ref, prev_row=None, next_row=None):
    n = x.shape[0]
    row = lax.broadcasted_iota(jnp.int32, (SUBLANES, 1), 0)
    xm = pltpu.roll(x, 1, axis=0)
    xp = pltpu.roll(x, n - 1, axis=0)
    head = jnp.where(row == 0, 0.0 if prev_row is None else prev_row, xm[:SUBLANES])
    tail = jnp.where(row == SUBLANES - 1, 0.0 if next_row is None else next_row, xp[n - SUBLANES:])
    xm = jnp.concatenate([head, xm[SUBLANES:]], axis=0)
    xp = jnp.concatenate([xp[:n - SUBLANES], tail], axis=0)
    return xm * w_ref[0:1, :] + x * w_ref[1:2, :] + xp * w_ref[2:3, :] + b_ref[...]


def _dwconv3_to_slabs(src_ref, w_ref, b_ref, dst_ref, seq_len, chunk):
    halo = 16
    n_chunks = seq_len // chunk
    for c in range(n_chunks):
        r0 = c * chunk
        x = src_ref[0, r0:r0 + chunk, :].astype(F32)
        prev_row = src_ref[0, r0 - halo:r0, :].astype(F32)[halo - 1:halo] if c > 0 else None
        next_row = src_ref[0, r0 + chunk:r0 + chunk + halo, :].astype(F32)[0:1] if c < n_chunks - 1 else None
        _st_rows(dst_ref, r0, _dwconv3_rows(x, w_ref, b_ref, prev_row, next_row))


def _hyena_conv_kernel(zin_ref, ux_ref, wz_ref, bz_ref, wx_ref, bx_ref, hb_ref, kh_ref,
                       f1_ref, f2_ref, g2_ref, g1_ref, o_ref, z_ref, xg_ref, a_ref, y_ref, *, seq_len, first):
    n_fft, n1, nk1, pad = _fft_dims(seq_len)
    L = seq_len
    half = n1 // 2
    nslab = z_ref.shape[0]
    chunk = min(L, 512)

    if first:
        _dwconv3_to_slabs(zin_ref, wz_ref, bz_ref, z_ref, L, chunk)
    else:
        for c in range(L // chunk):
            _st_rows(z_ref, c * chunk, zin_ref[0, c * chunk:(c + 1) * chunk, :])
    _dwconv3_to_slabs(ux_ref, wx_ref, bx_ref, xg_ref, L, chunk)

    _fft_stage1(z_ref, f1_ref, a_ref)

    u2 = 3 if nk1 % 3 == 0 else 1
    n_trips = nk1 // u2

    def spectrum_product(t):
        ys = []
        for j in range(u2):
            k1 = t * u2 + j
            x = jnp.dot(f2_ref[...], _fft_stage2_rhs(a_ref, k1, pad), preferred_element_type=F32)
            kk = kh_ref[0, 0, k1].astype(F32)
            xr, xi = x[:FFT_N2], x[FFT_N2:]
            kr, ki = kk[:FFT_N2], kk[FFT_N2:]
            ys.append(jnp.concatenate([xr * kr - xi * ki, xr * ki + xi * kr], axis=0).astype(BF16))
        return ys

    def inverse_dots(slot):
        return [jnp.dot(g2_ref[...], y_ref[slot, j], preferred_element_type=F32) for j in range(u2)]

    def store_inverse(t, bks):
        for j, bk in enumerate(bks):
            _st_outer(a_ref, t * u2 + j, bk[:FFT_N2])
            _st_outer(a_ref, pad + t * u2 + j, bk[FFT_N2:])

    def store_products(slot, ys):
        for j, y in enumerate(ys):
            y_ref[slot, j] = y

    store_products(0, spectrum_product(0))

    def body2(t, carry):
        ys = spectrum_product(t)
        bks = inverse_dots((t + 1) % 2)
        store_inverse(t - 1, bks)
        store_products(t % 2, ys)
        return carry
    lax.fori_loop(1, n_trips, body2, 0)
    store_inverse(n_trips - 1, inverse_dots((n_trips - 1) % 2))

    bias = hb_ref[...]
    u3 = 2

    def body3(t, carry):
        gs = [t * u3 + j for j in range(u3)]
        ins = [(_ld_group(a_ref, g).astype(BF16), _ld_group(xg_ref, g), _ld_group(z_ref, g)) for g in gs]
        outs = [xg * (jnp.dot(g1_ref[g], bb, preferred_element_type=F32) + bias * z_old)
                for g, (bb, xg, z_old) in zip(gs, ins)]
        for g, z_new in zip(gs, outs):
            _st_group(z_ref, g, z_new)
        return carry
    lax.fori_loop(0, FFT_N2 // FFT_GROUP // u3, body3, 0)

    for c in range(L // chunk):
        blk = jnp.concatenate([z_ref[h, c * chunk // FFT_N2:(c + 1) * chunk // FFT_N2].reshape(chunk, LANES)
                               for h in range(nslab)], axis=1)
        o_ref[0, c * chunk:(c + 1) * chunk, :] = blk.astype(o_ref.dtype)


def hyena_step(zin, zin_col0, u3, x_col0, w_cols, conv_w, conv_b, khat, layer, order, hy_bias_row, first, ct):
    nb, L, _ = u3.shape
    hw = hy_bias_row.shape[1]
    n_fft, n1, nk1, pad = _fft_dims(L)
    half = n1 // 2
    f1, f2, g2, g1 = _fft_constants(L)
    f1 = _interleaved_block_diag(f1[:, :, :half])
    g1 = _interleaved_block_diag(g1)
    nct = hw // ct
    nslab = ct // LANES
    assert zin_col0 % ct == 0 and x_col0 % ct == 0 and w_cols[0] % ct == 0 and w_cols[1] % ct == 0
    last = order == HYENA_ORDER - 1

    def col_spec(rows, col0):
        return pl.BlockSpec((rows, ct), lambda c, b: (0, col0 // ct + c))

    const = lambda shape: pl.BlockSpec(shape, lambda c, b: (0,) * len(shape), pipeline_mode=pl.Buffered(1))
    return pl.pallas_call(
        functools.partial(_hyena_conv_kernel, seq_len=L, first=first),
        grid=(nct, nb),
        in_specs=[
            pl.BlockSpec((1, L, ct), lambda c, b: (b, 0, zin_col0 // ct + c)),
            pl.BlockSpec((1, L, ct), lambda c, b: (b, 0, x_col0 // ct + c)),
            col_spec(3, w_cols[0]), col_spec(1, w_cols[0]), col_spec(3, w_cols[1]), col_spec(1, w_cols[1]),
            pl.BlockSpec((1, ct), lambda c, b: (0, c)),
            pl.BlockSpec((1, 1, nk1, 2 * FFT_N2, ct), lambda c, b: (layer, order, 0, 0, c),
                         pipeline_mode=pl.Buffered(1)),
            const(f1.shape), const((2 * FFT_N2, 2 * FFT_N2)),
            const((2 * FFT_N2, 2 * FFT_N2)), const(g1.shape),
        ],
        out_specs=pl.BlockSpec((1, L, ct), lambda c, b: (b, 0, c)),
        out_shape=jax.ShapeDtypeStruct((nb, L, hw), BF16 if last else F32),
        scratch_shapes=[
            pltpu.VMEM((nslab, half, FFT_N2, LANES), F32),
            pltpu.VMEM((nslab, half, FFT_N2, LANES), F32),
            pltpu.VMEM((nslab, 2 * pad, FFT_N2, LANES), F32),
            pltpu.VMEM((2, 3 if nk1 % 3 == 0 else 1, 2 * FFT_N2, ct), BF16),
        ],
        compiler_params=_cparams(("arbitrary", "arbitrary")),
        name=f"hyena_step{order}",
    )(zin, u3, conv_w, conv_b, conv_w, conv_b, hy_bias_row, khat,
      jnp.asarray(f1, BF16), jnp.asarray(f2, BF16), jnp.asarray(g2, BF16), jnp.asarray(g1, BF16))


def hyena_mix(u3, col0, conv_w, conv_b, khat, layer, hy_bias, ct):
    hw = hy_bias.shape[1]
    z = hyena_step(u3, col0, u3, col0 + hw, (0, hw), conv_w, conv_b, khat, layer, 0, hy_bias[0:1], True, ct)
    return hyena_step(z, 0, u3, col0 + 2 * hw, (0, 2 * hw), conv_w, conv_b, khat, layer, 1, hy_bias[1:2], False, ct)


def _t5_bucket(rel):
    nb = REL_BUCKETS // 2
    ret = (rel > 0).astype(np.int32) * nb
    n = np.abs(rel)
    max_exact = nb // 2
    large = max_exact + (np.log(np.maximum(n, 1) / max_exact) / np.log(REL_MAX_DIST / max_exact)
                         * (nb - max_exact)).astype(np.int32)
    large = np.minimum(large, nb - 1)
    return ret + np.where(n < max_exact, n, large)


ATTN_EDGE_VARIANTS = 4


def _bias_kernel(rb_ref, bkt_ref, o_ref, *, n_sides):
    n_groups = bkt_ref.shape[0]
    qq = lax.broadcasted_iota(jnp.int32, (ATTN_QB, 2 * ATTN_QB), 0)
    kk = lax.broadcasted_iota(jnp.int32, (ATTN_QB, 2 * ATTN_QB), 1)
    for g in range(n_groups):
        n_side = n_sides[g]
        rel = kk - n_side - qq
        band = (rel >= -n_side) & (rel <= n_side)
        before = kk < n_side
        after = kk >= ATTN_QB + n_side
        bkt = bkt_ref[g]
        for h in range(HEADS_PER_GROUP):
            tile = jnp.zeros(bkt.shape, F32)
            for b in range(REL_BUCKETS):
                tile = jnp.where(bkt == b, rb_ref[b, g * HEADS_PER_GROUP + h], tile)
            tile = jnp.where(band, tile, NEG_INF)
            o_ref[g, 0, h] = tile
            o_ref[g, 1, h] = jnp.where(before, NEG_INF, tile)
            o_ref[g, 2, h] = jnp.where(after, NEG_INF, tile)
            o_ref[g, 3, h] = jnp.where(before | after, NEG_INF, tile)


def attention_bias_tiles(rel_bias, n_side_list):
    n_groups = len(ATTN_GROUPS)
    qq = np.arange(ATTN_QB)[:, None]
    kk = np.arange(2 * ATTN_QB)[None, :]
    bkts = []
    for (window, dil), n_side in zip(ATTN_GROUPS, n_side_list):
        j = kk - n_side - qq
        bkts.append(_t5_bucket(j * dil))
    bkt = np.stack(bkts).astype(np.int32)
    return pl.pallas_call(
        functools.partial(_bias_kernel, n_sides=tuple(n_side_list)),
        in_specs=[pl.BlockSpec(memory_space=pltpu.SMEM), pl.BlockSpec(memory_space=pltpu.VMEM)],
        out_specs=pl.BlockSpec(memory_space=pltpu.VMEM),
        out_shape=jax.ShapeDtypeStruct((n_groups, ATTN_EDGE_VARIANTS, HEADS_PER_GROUP, ATTN_QB, 2 * ATTN_QB), F32),
        name="attention_bias_tiles",
    )(rel_bias, jnp.asarray(bkt))


def _attn_kernel(q_ref, k_ref, v_ref, bias_ref, o_ref, l_ref, ks_ref, vs_ref, *, m_len, n_side, tq):
    qt = pl.program_id(2)
    width = HEADS_PER_GROUP * HEAD_DIM
    halo = n_side

    n_cls = q_ref.shape[0]
    n_blk = tq // ATTN_QB

    @pl.when(qt == 0)
    def _():
        zeros = jnp.zeros((halo, width), BF16)
        for c in range(n_cls):
            for ref, src in ((ks_ref, k_ref), (vs_ref, v_ref)):
                ref[c, pl.ds(0, halo), :] = zeros
                ref[c, pl.ds(halo + m_len, 2 * ATTN_QB - halo), :] = jnp.zeros((2 * ATTN_QB - halo, width), BF16)
                ref[c, pl.ds(halo, m_len), :] = src[c]

    lane = lax.broadcasted_iota(jnp.int32, (ATTN_QB, 2 * HEAD_DIM), 1)
    low = lane < HEAD_DIM
    scale = 1.0 / math.sqrt(HEAD_DIM)

    def block(idx, carry):
        c = idx // n_blk
        r0 = pl.multiple_of((idx % n_blk) * ATTN_QB, ATTN_QB)
        q0 = pl.multiple_of(qt * tq + r0, ATTN_QB)
        variant = (q0 == 0).astype(jnp.int32) + 2 * (q0 == m_len - ATTN_QB).astype(jnp.int32)
        for hp in range(HEADS_PER_GROUP // 2):
            cols = pl.ds(hp * 2 * HEAD_DIM, 2 * HEAD_DIM)
            qp = q_ref[c, pl.ds(r0, ATTN_QB), cols] * scale
            kp = ks_ref[c, pl.ds(q0, 2 * ATTN_QB), cols]
            vp = vs_ref[c, pl.ds(q0, 2 * ATTN_QB), cols]
            outs, lses = [], []
            for hh in range(2):
                sel = low if hh == 0 else jnp.logical_not(low)
                qm = jnp.where(sel, qp, jnp.zeros_like(qp))
                s = lax.dot_general(qm, kp, (((1,), (1,)), ((), ())), preferred_element_type=F32)
                s = s + bias_ref[variant, 2 * hp + hh]
                mx = jnp.max(s, axis=-1, keepdims=True)
                p = jnp.exp(s - mx)
                den = jnp.sum(p, axis=-1, keepdims=True)
                pv = jnp.dot(p.astype(BF16), vp, preferred_element_type=F32)
                outs.append(pv / den)
                lses.append(mx + jnp.log(den))
            o_ref[c, pl.ds(r0, ATTN_QB), cols] = jnp.where(low, outs[0], outs[1]).astype(o_ref.dtype)
            l_ref[c, pl.ds(r0, ATTN_QB), cols] = jnp.where(low, lses[0], lses[1])
        return carry
    lax.fori_loop(0, n_cls * n_blk, block, 0, unroll=2 if (n_cls * n_blk) % 2 == 0 else 1)


def dilated_group_attention(qkv, g, dil, n_side, bias_tiles):
    nb, d, m_len, _ = qkv.shape
    width = HEADS_PER_GROUP * HEAD_DIM
    assert d == dil and m_len % ATTN_QB == 0 and n_side <= ATTN_QB and n_side % 16 == 0
    tq = min(ATTN_ROWS_PER_STEP, m_len)
    cg = math.gcd(dil, ATTN_ROWS_PER_STEP // tq)
    kernel = functools.partial(_attn_kernel, m_len=m_len, n_side=n_side, tq=tq)
    return pl.pallas_call(
        kernel,
        grid=(nb, dil // cg, m_len // tq),
        in_specs=[
            pl.BlockSpec((None, cg, tq, width), lambda b, r, t: (b, r, t, 0)),
            pl.BlockSpec((None, cg, m_len, width), lambda b, r, t: (b, r, 0, 1)),
            pl.BlockSpec((None, cg, m_len, width), lambda b, r, t: (b, r, 0, 2)),
            pl.BlockSpec((None, ATTN_EDGE_VARIANTS, HEADS_PER_GROUP, ATTN_QB, 2 * ATTN_QB),
                         lambda b, r, t: (g, 0, 0, 0, 0)),
        ],
        out_specs=[
            pl.BlockSpec((None, cg, tq, width), lambda b, r, t: (b, r, t, 0)),
            pl.BlockSpec((None, cg, tq, width), lambda b, r, t: (b, r, t, 0)),
        ],
        out_shape=[
            jax.ShapeDtypeStruct((nb, dil, m_len, width), BF16),
            jax.ShapeDtypeStruct((nb, dil, m_len, width), F32),
        ],
        scratch_shapes=[
            pltpu.VMEM((cg, m_len + 2 * ATTN_QB, width), BF16),
            pltpu.VMEM((cg, m_len + 2 * ATTN_QB, width), BF16),
        ],
        compiler_params=_cparams(("arbitrary", "arbitrary", "arbitrary")),
        name=f"dilated_attention_g{g}",
    )(qkv, qkv, qkv, bias_tiles)


def _merge_kernel(yh_ref, *rest, dils):
    ng = len(dils)
    o_refs, l_refs = rest[:ng], rest[ng:2 * ng]
    gh_ref, ga_ref, bh_ref, ba_ref, wh_ref, wa_ref, out_ref, ya_ref, tok_ref = rest[2 * ng:]
    tm, aw = ya_ref.shape
    nslab = aw // LANES

    @pl.when(pl.program_id(1) == 0)
    def _():
        for gi, d in enumerate(dils):
            if d == 1:
                continue
            rows = tm // d
            for r in range(d):
                for s in range(nslab):
                    cols = slice(s * LANES, (s + 1) * LANES)
                    tok_ref[2 * gi, s, pl.ds(r, rows, stride=d), :] = o_refs[gi][r, :, cols].astype(F32)
                    tok_ref[2 * gi + 1, s, pl.ds(r, rows, stride=d), :] = l_refs[gi][r, :, cols]
        for s in range(nslab):
            cols = slice(s * LANES, (s + 1) * LANES)
            os_, ls_ = [], []
            for gi, d in enumerate(dils):
                if d == 1:
                    os_.append(o_refs[gi][0, :, cols].astype(F32))
                    ls_.append(l_refs[gi][0, :, cols])
                else:
                    os_.append(tok_ref[2 * gi, s])
                    ls_.append(tok_ref[2 * gi + 1, s])
            mx = functools.reduce(jnp.maximum, ls_)
            es = [jnp.exp(l - mx) for l in ls_]
            inv = 1.0 / functools.reduce(lambda a, b: a + b, es)
            ya = functools.reduce(lambda a, b: a + b, [(e * inv) * o for e, o in zip(es, os_)])
            ya_ref[:, cols] = ya.astype(BF16)

    acc_h = jnp.dot(yh_ref[...], wh_ref[...], preferred_element_type=F32)
    acc_a = jnp.dot(ya_ref[...], wa_ref[...], preferred_element_type=F32)
    t_h = jnp.tanh(0.5 * (gh_ref[...].astype(F32) + bh_ref[...]))
    t_a = jnp.tanh(0.5 * (ga_ref[...].astype(F32) + ba_ref[...]))
    out_ref[...] = (0.5 * ((1.0 + t_h) * acc_h + (1.0 + t_a) * acc_a)).astype(out_ref.dtype)


def branch_merge(y_hy, outs, lses, dils, u2, gate_col0, b_gate, w_hy, w_at, seq_len, tm=512):
    t, hw = y_hy.shape
    d = w_hy.shape[1]
    aw = w_at.shape[0]
    tn = _pick_tile(math.gcd(gate_col0, d) if gate_col0 else d, d)
    gb = gate_col0 // tn
    nj = d // tn
    tps = seq_len // tm
    assert all(tm % (dd * 2 * SUBLANES) == 0 for dd in dils)
    grp = [pl.BlockSpec((None, dd, tm // dd, aw), lambda i, j: (i // tps, 0, i % tps, 0)) for dd in dils]
    w_mode = dict(pipeline_mode=pl.Buffered(1)) if nj == 1 else {}
    return pl.pallas_call(
        functools.partial(_merge_kernel, dils=tuple(dils)),
        grid=(t // tm, nj),
        in_specs=[pl.BlockSpec((tm, hw), lambda i, j: (i, 0))] + grp + grp + [
            pl.BlockSpec((tm, tn), lambda i, j: (i, gb + j)),
            pl.BlockSpec((tm, tn), lambda i, j: (i, gb + nj + j)),
            pl.BlockSpec((1, tn), lambda i, j: (0, j)),
            pl.BlockSpec((1, tn), lambda i, j: (0, nj + j)),
            pl.BlockSpec((hw, tn), lambda i, j: (0, j), **w_mode),
            pl.BlockSpec((aw, tn), lambda i, j: (0, j), **w_mode),
        ],
        out_specs=pl.BlockSpec((tm, tn), lambda i, j: (i, j)),
        out_shape=jax.ShapeDtypeStruct((t, d), BF16),
        scratch_shapes=[pltpu.VMEM((tm, aw), BF16), pltpu.VMEM((2 * len(dils), aw // LANES, tm, LANES), F32)],
        compiler_params=_cparams(("arbitrary", "arbitrary")),
        name="branch_merge",
    )(y_hy, *outs, *lses, u2, u2, b_gate, b_gate, w_hy, w_at)


def _out_proj_kernel(x_ref, m_ref, gate_ref, w_ref, o_ref):
    acc = jnp.dot(m_ref[...], w_ref[...], preferred_element_type=F32)
    o_ref[...] = x_ref[...] + gate_ref[0] * acc


def out_proj_residual(x, m, mod3, gate_idx, w, seq_len, tm=512, tn=None, name="out_proj_residual"):
    t, d = x.shape
    k = m.shape[1]
    tn = d if tn is None else tn
    nj = d // tn
    tiles_per_seq = seq_len // tm
    return pl.pallas_call(
        _out_proj_kernel,
        grid=(t // tm, nj),
        in_specs=[
            pl.BlockSpec((tm, tn), lambda i, j: (i, j)),
            pl.BlockSpec((tm, k), lambda i, j: (i, 0)),
            pl.BlockSpec((1, 1, tn), lambda i, j: (i // tiles_per_seq, 0, gate_idx * nj + j)),
            pl.BlockSpec((k, tn), lambda i, j: (0, j)),
        ],
        out_specs=pl.BlockSpec((tm, tn), lambda i, j: (i, j)),
        out_shape=jax.ShapeDtypeStruct((t, d), F32),
        input_output_aliases={0: 0},
        compiler_params=_cparams(("arbitrary", "arbitrary")),
        name=name,
    )(x, m, mod3, w)


def _final_norm_kernel(x_ref, g_ref, o_ref):
    x = x_ref[...]
    ms = jnp.mean(x * x, axis=-1, keepdims=True)
    o_ref[...] = x * lax.rsqrt(ms + NORM_EPS) * g_ref[...]


def final_norm(x, g, row0, n_rows, tm=512):
    d = x.shape[1]
    off = row0 // tm
    return pl.pallas_call(
        _final_norm_kernel,
        grid=(n_rows // tm,),
        in_specs=[pl.BlockSpec((tm, d), lambda i: (off + i, 0)), pl.BlockSpec((1, d), lambda i: (0, 0))],
        out_specs=pl.BlockSpec((tm, d), lambda i: (i, 0)),
        out_shape=jax.ShapeDtypeStruct((n_rows, d), F32),
        compiler_params=_cparams(("arbitrary",)),
        name="final_norm",
    )(x, g.reshape(1, d))


def kernel(x_prompt, x_sample, c_prompt, c_sample, ada_w, ada_b, norm1_g, w_in, b_gate, hy_conv_w, hy_conv_b,
           filt_w1, filt_b1, filt_freq1, filt_w2, filt_b2, filt_freq2, filt_w3, hy_bias, rel_bias, w_br_hy,
           w_br_attn, w_out, norm2_g, ffn_up, ffn_conv_w, ffn_conv_b, ffn_down, final_g):
    bp, L, d = x_prompt.shape
    bs = x_sample.shape[0]
    assert x_sample.shape[1] == L
    nb = bp + bs
    depth = ada_w.shape[0]
    hw = hy_bias.shape[2]
    n_groups = len(ATTN_GROUPS)
    attn_w = n_groups * HEADS_PER_GROUP * HEAD_DIM
    hy_cols = 3 * hw
    gate_col0 = hy_cols + 3 * attn_w
    assert w_in.shape[2] == gate_col0 + 2 * d

    x = jnp.concatenate([x_prompt, x_sample], axis=0).reshape(nb * L, d)
    nb_pad = -(-nb // SUBLANES) * SUBLANES
    c_pad = jnp.zeros((nb_pad, d), F32).at[:nb].set(jnp.concatenate([c_prompt, c_sample], axis=0))
    mod = ada_modulation(c_pad, ada_w, ada_b)

    hy_ct = min(HYENA_CT, hw)
    khat = hyena_filter_spectra(L, filt_w1, filt_b1, filt_freq1, filt_w2, filt_b2, filt_freq2, filt_w3, hy_ct)
    n_sides = [(window // 2) // dil for window, dil in ATTN_GROUPS]
    bias_tiles = attention_bias_tiles(rel_bias, n_sides)

    dils = [dil for _, dil in ATTN_GROUPS]
    gw = HEADS_PER_GROUP * HEAD_DIM

    def group_cols(w, g):
        return [w[:, hy_cols + part * attn_w + g * gw: hy_cols + part * attn_w + (g + 1) * gw] for part in range(3)]

    for l in range(depth):
        mod3 = mod[l].reshape(nb_pad, 1, N_MOD * d)
        w_l = w_in[l]
        w_main = jnp.concatenate([w_l[:, gate_col0:], w_l[:, :hy_cols]], axis=1).astype(BF16)
        w_groups = jnp.concatenate([c for g in range(n_groups) for c in group_cols(w_l, g)], axis=1).astype(BF16)
        u, qkv = in_projection(x, norm1_g[l], mod3, 1, 0, w_main, w_groups, dils, nb, L)
        u3 = u.reshape(nb, L, u.shape[1])
        y_hy = hyena_mix(u3, 2 * d, hy_conv_w[l], hy_conv_b[l].reshape(1, hy_cols), khat, l, hy_bias[l], hy_ct)
        outs, lses = [], []
        for g, dil in enumerate(dils):
            o_g, l_g = dilated_group_attention(qkv[g], g, dil, n_sides[g], bias_tiles)
            outs.append(o_g)
            lses.append(l_g)
        merged = branch_merge(y_hy.reshape(nb * L, hw), outs, lses, dils, u, 0, b_gate[l].reshape(1, 2 * d),
                              w_br_hy[l].astype(BF16), w_br_attn[l].astype(BF16), L)
        x = out_proj_residual(x, merged, mod3, 2, w_out[l].astype(BF16), L)
        ff = ffn_down.shape[1]
        act = ffn_up_act(x, norm2_g[l], mod3, 4, 3, interleave_ffn_up(ffn_up[l].astype(BF16)),
                         ffn_conv_w[l], ffn_conv_b[l].reshape(1, ff), L)
        x = out_proj_residual(x, act, mod3, 5, ffn_down[l].astype(BF16), L, tm=min(1024, L),
                              tn=_pick_tile(d, 512), name="ffn_down_residual")

    y_prompt = final_norm(x, final_g, 0, bp * L).reshape(bp, L, d)
    y_sample = final_norm(x, final_g, bp * L, bs * L).reshape(bs, L, d)
    return (y_prompt, y_sample)
```

```python
import functools
import math

import jax
import jax.numpy as jnp
import numpy as np
from jax import lax
from jax.experimental import pallas as pl
from jax.experimental.pallas import tpu as pltpu

F32 = jnp.float32
BF16 = jnp.bfloat16
HIGHEST = lax.Precision.HIGHEST

NORM_EPS = 1e-6
N_MOD = 6
HYENA_ORDER = 2
N_BANDS = 16
FAST_DECAY_PCT = 0.3
SLOW_DECAY_PCT = 1.5
DECAY_TARGET = 1e-2
ATTN_GROUPS = ((128, 1), (512, 4), (2048, 16))
HEADS_PER_GROUP = 8
HEAD_DIM = 64
REL_BUCKETS = 32
REL_MAX_DIST = 1024
NEG_INF = -1e30

LANES = 128
SUBLANES = 8
VMEM_LIMIT = 56 * 1024 * 1024

FFT_N2 = 128
FFT_GROUP = SUBLANES
HYENA_CT = 256
ATTN_QB = 128
ATTN_ROWS_PER_STEP = 1024


def _cparams(sem):
    return pltpu.CompilerParams(dimension_semantics=sem, vmem_limit_bytes=VMEM_LIMIT)


def _pick_tile(n, cap):
    best = None
    for t in range(LANES, min(n, cap) + 1, LANES):
        if n % t == 0:
            best = t
    assert best is not None, (n, cap)
    return best


def _ada_kernel(c_ref, w_ref, b_ref, o_ref):
    c = c_ref[...]
    cs = c * (1.0 / (1.0 + jnp.exp(-c)))
    o_ref[0] = jnp.dot(cs, w_ref[0], preferred_element_type=F32, precision=HIGHEST) + b_ref[0]


def ada_modulation(c_pad, ada_w, ada_b):
    depth, d, n = ada_w.shape
    nb = c_pad.shape[0]
    tn = _pick_tile(n, 1024)
    return pl.pallas_call(
        _ada_kernel,
        grid=(depth, n // tn),
        in_specs=[
            pl.BlockSpec((nb, d), lambda l, j: (0, 0)),
            pl.BlockSpec((1, d, tn), lambda l, j: (l, 0, j)),
            pl.BlockSpec((1, 1, tn), lambda l, j: (l, 0, j)),
        ],
        out_specs=pl.BlockSpec((1, nb, tn), lambda l, j: (l, 0, j)),
        out_shape=jax.ShapeDtypeStruct((depth, nb, n), F32),
        compiler_params=_cparams(("arbitrary", "arbitrary")),
        name="ada_modulation",
    )(c_pad, ada_w, ada_b.reshape(depth, 1, n))


def _normmod_rows(x, g, sc, sh):
    ms = jnp.mean(x * x, axis=-1, keepdims=True)
    return (x * lax.rsqrt(ms + NORM_EPS)) * (g * (1.0 + sc)) + sh


FFN_HALO = 16


def _ffn_up_act_kernel(x_ref, xp_ref, xn_ref, g_ref, sc_ref, sh_ref, w_ref, cw_ref, cb_ref,
                       o_ref, h_ref, *, tiles_per_seq):
    i = pl.program_id(0)
    tm = x_ref.shape[0]

    @pl.when(pl.program_id(1) == 0)
    def _():
        g, sc, sh = g_ref[...], sc_ref[0], sh_ref[0]
        h_ref[0:FFN_HALO] = _normmod_rows(xp_ref[...], g, sc, sh).astype(BF16)
        h_ref[FFN_HALO:FFN_HALO + tm] = _normmod_rows(x_ref[...], g, sc, sh).astype(BF16)
        h_ref[FFN_HALO + tm:] = _normmod_rows(xn_ref[...], g, sc, sh).astype(BF16)

    first = (i % tiles_per_seq) == 0
    last = (i % tiles_per_seq) == tiles_per_seq - 1
    row = lax.broadcasted_iota(jnp.int32, (tm + 2 * FFN_HALO, 1), 0)
    outside = ((row == FFN_HALO - 1) & first) | ((row == FFN_HALO + tm) & last)
    for c in range(o_ref.shape[1] // LANES):
        res = jnp.dot(h_ref[...], w_ref[:, 2 * c * LANES:2 * (c + 1) * LANES], preferred_element_type=F32)
        a = res[FFN_HALO:FFN_HALO + tm, :LANES]
        gt = jnp.where(outside, 0.0, res[:, LANES:])
        cols = slice(c * LANES, (c + 1) * LANES)
        conv = _dwconv3_rows(gt, cw_ref.at[:, cols], cb_ref.at[:, cols])[FFN_HALO:FFN_HALO + tm]
        half = 0.5 * conv
        o_ref[:, cols] = (half * (1.0 + jnp.tanh(half)) * a).astype(o_ref.dtype)


def interleave_ffn_up(w_up):
    ff = w_up.shape[1] // 2
    chunks = []
    for c in range(ff // LANES):
        chunks += [w_up[:, c * LANES:(c + 1) * LANES], w_up[:, ff + c * LANES:ff + (c + 1) * LANES]]
    return jnp.concatenate(chunks, axis=1)


def ffn_up_act(x, g, mod3, sc_idx, sh_idx, w_il, conv_w, conv_b, seq_len, tm=512, tn_cap=1536):
    t, d = x.shape
    ff = w_il.shape[1] // 2
    tn = _pick_tile(ff, tn_cap)
    nj = ff // tn
    tps = seq_len // tm
    rb = tm // FFN_HALO
    last_rb = t // FFN_HALO - 1
    return pl.pallas_call(
        functools.partial(_ffn_up_act_kernel, tiles_per_seq=tps),
        grid=(t // tm, nj),
        in_specs=[
            pl.BlockSpec((tm, d), lambda i, j: (i, 0)),
            pl.BlockSpec((FFN_HALO, d), lambda i, j: (jnp.maximum(i * rb - 1, 0), 0)),
            pl.BlockSpec((FFN_HALO, d), lambda i, j: (jnp.minimum((i + 1) * rb, last_rb), 0)),
            pl.BlockSpec((1, d), lambda i, j: (0, 0)),
            pl.BlockSpec((1, 1, d), lambda i, j: (i // tps, 0, sc_idx)),
            pl.BlockSpec((1, 1, d), lambda i, j: (i // tps, 0, sh_idx)),
            pl.BlockSpec((d, 2 * tn), lambda i, j: (0, j)),
            pl.BlockSpec((3, tn), lambda i, j: (0, j)),
            pl.BlockSpec((1, tn), lambda i, j: (0, j)),
        ],
        out_specs=pl.BlockSpec((tm, tn), lambda i, j: (i, j)),
        out_shape=jax.ShapeDtypeStruct((t, ff), BF16),
        scratch_shapes=[pltpu.VMEM((tm + 2 * FFN_HALO, d), BF16)],
        compiler_params=_cparams(("arbitrary", "arbitrary")),
        name="ffn_up_act",
    )(x, x, x, g.reshape(1, d), mod3, mod3, w_il, conv_w, conv_b)


def _in_proj_kernel(x_ref, g_ref, sc_ref, sh_ref, wm_ref, wg_ref, om_ref, *rest, n_main, dils, chunk):
    og_refs = rest[:len(dils)]
    h_ref, perm_ref = rest[len(dils):]
    j = pl.program_id(1)
    tm = h_ref.shape[0]

    @pl.when(j == 0)
    def _():
        h_ref[...] = _normmod_rows(x_ref[...], g_ref[...], sc_ref[0], sh_ref[0]).astype(BF16)

    @pl.when(j < n_main)
    def _():
        om_ref[...] = jnp.dot(h_ref[...], wm_ref[...], preferred_element_type=F32).astype(om_ref.dtype)

    for gi, (o_ref, d) in enumerate(zip(og_refs, dils)):
        @pl.when(j == n_main + gi)
        def _():
            rows = tm // d
            for c0 in range(0, wg_ref.shape[1], chunk):
                res = jnp.dot(h_ref[...], wg_ref[:, c0:c0 + chunk], preferred_element_type=F32)
                if d == 1:
                    o_ref[0, 0, :, c0:c0 + chunk] = res.astype(o_ref.dtype)
                    continue
                for s in range(chunk // LANES):
                    perm_ref[s] = res[:, s * LANES:(s + 1) * LANES]
                for r in range(d):
                    for s in range(chunk // LANES):
                        cols = slice(c0 + s * LANES, c0 + (s + 1) * LANES)
                        o_ref[0, r, :, cols] = perm_ref[s, pl.ds(r, rows, stride=d), :].astype(o_ref.dtype)


def in_projection(x, g, mod3, sc_idx, sh_idx, w_main, w_groups, dils, nb, seq_len, tm=512, tn_cap=1792):
    t, d_model = x.shape
    n_main = w_main.shape[1]
    gw = w_groups.shape[1] // len(dils)
    tn = _pick_tile(n_main, tn_cap)
    nj_main = n_main // tn
    tps = seq_len // tm
    chunk = gw // 3
    assert chunk % LANES == 0 and all(tm % (dd * 16) == 0 for dd in dils)
    kernel = functools.partial(_in_proj_kernel, n_main=nj_main, dils=tuple(dils), chunk=chunk)
    outs = pl.pallas_call(
        kernel,
        grid=(t // tm, nj_main + len(dils)),
        in_specs=[
            pl.BlockSpec((tm, d_model), lambda i, j: (i, 0)),
            pl.BlockSpec((1, d_model), lambda i, j: (0, 0)),
            pl.BlockSpec((1, 1, d_model), lambda i, j: (i // tps, 0, sc_idx)),
            pl.BlockSpec((1, 1, d_model), lambda i, j: (i // tps, 0, sh_idx)),
            pl.BlockSpec((d_model, tn), lambda i, j: (0, jnp.minimum(j, nj_main - 1))),
            pl.BlockSpec((d_model, gw), lambda i, j: (0, jnp.clip(j - nj_main, 0, len(dils) - 1))),
        ],
        out_specs=[pl.BlockSpec((tm, tn), lambda i, j: (i, jnp.minimum(j, nj_main - 1)))] + [
            pl.BlockSpec((1, dd, tm // dd, gw), lambda i, j: (i // tps, 0, i % tps, 0)) for dd in dils],
        out_shape=[jax.ShapeDtypeStruct((t, n_main), BF16)] + [
            jax.ShapeDtypeStruct((nb, dd, seq_len // dd, gw), BF16) for dd in dils],
        scratch_shapes=[pltpu.VMEM((tm, d_model), BF16), pltpu.VMEM((chunk // LANES, tm, LANES), F32)],
        compiler_params=_cparams(("arbitrary", "arbitrary")),
        name="in_projection",
    )(x, g.reshape(1, d_model), mod3, mod3, w_main, w_groups)
    return outs[0], list(outs[1:])


def _fft_dims(seq_len):
    n_fft = 2 * seq_len
    n1 = n_fft // FFT_N2
    assert n1 * FFT_N2 == n_fft and n1 % 2 == 0
    nk1 = n1 // 2 + 1
    pad = -(-nk1 // SUBLANES) * SUBLANES
    return n_fft, n1, nk1, pad


@functools.lru_cache(maxsize=None)
def _fft_constants(seq_len):
    n_fft, n1, nk1, pad = _fft_dims(seq_len)
    n2 = FFT_N2
    k1 = np.arange(nk1)[None, :, None]
    i1 = np.arange(n1)[None, None, :]
    i2 = np.arange(n2)[:, None, None]
    ph = 2.0 * np.pi * ((k1 * (i1 * n2 + i2)) % n_fft) / n_fft
    f1 = np.zeros((n2, 2 * pad, n1), np.float64)
    f1[:, :nk1] = np.cos(ph)
    f1[:, pad:pad + nk1] = -np.sin(ph)
    ck = np.full((nk1,), 2.0)
    ck[0] = 1.0
    ck[-1] = 1.0
    g1 = np.zeros((n2, n1 // 2, 2 * pad), np.float64)
    pht = np.transpose(ph[:, :, :n1 // 2], (0, 2, 1))
    g1[:, :, :nk1] = np.cos(pht) * ck / n_fft
    g1[:, :, pad:pad + nk1] = -np.sin(pht) * ck / n_fft
    a = np.arange(n2)
    ph2 = 2.0 * np.pi * ((a[:, None] * a[None, :]) % n2) / n2
    c2, s2 = np.cos(ph2), np.sin(ph2)
    f2 = np.block([[c2, s2], [-s2, c2]])
    g2 = np.block([[c2, -s2], [s2, c2]])
    return (f1.astype(np.float32), f2.astype(np.float32), g2.astype(np.float32), g1.astype(np.float32))


def _interleaved_block_diag(mats):
    n, r, c = mats.shape
    out = np.zeros((n // FFT_GROUP, r * FFT_GROUP, c * FFT_GROUP), mats.dtype)
    for j in range(FFT_GROUP):
        out[:, j::FFT_GROUP, j::FFT_GROUP] = mats[j::FFT_GROUP]
    return out


def _group_rows(g):
    return pl.ds(pl.multiple_of(g * FFT_GROUP, FFT_GROUP), FFT_GROUP)


def _ld_group(ref, g):
    n_outer = ref.shape[1]
    return jnp.concatenate([ref[h, :, _group_rows(g), :].reshape(n_outer * FFT_GROUP, LANES)
                            for h in range(ref.shape[0])], axis=1)


def _st_group(ref, g, val):
    n_outer = ref.shape[1]
    for h in range(ref.shape[0]):
        ref[h, :, _group_rows(g), :] = val[:, h * LANES:(h + 1) * LANES].reshape(n_outer, FFT_GROUP, LANES)


def _ld_outer(ref, idx):
    return jnp.concatenate([ref[h, idx] for h in range(ref.shape[0])], axis=1)


def _st_outer(ref, idx, val):
    for h in range(ref.shape[0]):
        ref[h, idx] = val[:, h * LANES:(h + 1) * LANES]


def _st_rows(ref, row0, val):
    n = val.shape[0]
    for h in range(ref.shape[0]):
        ref[h, row0 // FFT_N2:(row0 + n) // FFT_N2] = val[:, h * LANES:(h + 1) * LANES].reshape(
            n // FFT_N2, FFT_N2, LANES)


def _fft_stage1(src_ref, f1b_ref, a_ref):
    def body(g, carry):
        a = jnp.dot(f1b_ref[g], _ld_group(src_ref, g).astype(BF16), preferred_element_type=F32)
        _st_group(a_ref, g, a)
        return carry
    lax.fori_loop(0, FFT_N2 // FFT_GROUP, body, 0, unroll=2)


def _fft_stage2_rhs(a_ref, k1, pad):
    return jnp.concatenate([_ld_outer(a_ref, k1), _ld_outer(a_ref, pad + k1)], axis=0).astype(BF16)


def _filter_kernel(feat_ref, featr_ref, w1_ref, b1_ref, fr1_ref, w2_ref, b2_ref, fr2_ref,
                   w3f_ref, w3b_ref, delta_ref, f1_ref, f2_ref,
                   o_ref, hid_ref, k2_ref, a_ref, *, seq_len):
    n_fft, n1, nk1, pad = _fft_dims(seq_len)
    L = seq_len

    @pl.when((pl.program_id(1) == 0) & (pl.program_id(2) == 0))
    def _():
        for d, fref in enumerate((feat_ref, featr_ref)):
            h = jnp.dot(fref[...], w1_ref[0], preferred_element_type=F32, precision=HIGHEST) + b1_ref[0]
            h = jnp.sin(fr1_ref[0] * h)
            h = jnp.dot(h, w2_ref[0], preferred_element_type=F32, precision=HIGHEST) + b2_ref[0]
            hid_ref[d] = jnp.sin(fr2_ref[0] * h)

    delta = delta_ref[...]
    chunk = min(L, 512)
    row = lax.broadcasted_iota(jnp.int32, (chunk, 1), 0)
    l1 = jnp.zeros(delta.shape, F32)
    for d, (w3_ref, fref) in enumerate(((w3f_ref, feat_ref), (w3b_ref, featr_ref))):
        for c in range(L // chunk):
            rows = pl.ds(c * chunk, chunk)
            h = jnp.dot(hid_ref[d, rows, :], w3_ref[0], preferred_element_type=F32, precision=HIGHEST)
            h = h * jnp.exp(-fref[rows, 0:1] * delta)
            if d == 1 and c == 0:
                h = jnp.where(row == 0, 0.0, h)
            l1 = l1 + jnp.sum(jnp.abs(h), axis=0, keepdims=True)
            _st_rows(k2_ref, d * L + c * chunk, h)
    inv = 1.0 / l1
    _fft_stage1(k2_ref, f1_ref, a_ref)

    def body(k1, carry):
        spec = jnp.dot(f2_ref[...], _fft_stage2_rhs(a_ref, k1, pad), preferred_element_type=F32) * inv
        o_ref[0, 0, k1] = spec.astype(o_ref.dtype)
        return carry
    lax.fori_loop(0, nk1, body, 0, unroll=3 if nk1 % 3 == 0 else 1)


def hyena_filter_spectra(seq_len, filt_w1, filt_b1, filt_freq1, filt_w2, filt_b2, filt_freq2, filt_w3, ct):
    depth, n_feat, hid = filt_w1.shape
    hw = filt_w3.shape[2] // (2 * HYENA_ORDER)
    n_fft, n1, nk1, pad = _fft_dims(seq_len)
    L = seq_len
    t = np.linspace(0.0, 1.0, L, dtype=np.float32).astype(np.float64)[:, None]
    pos = np.arange(L, dtype=np.float64)[:, None]
    bands = np.linspace(1e-4, N_BANDS - 1, N_BANDS, dtype=np.float32).astype(np.float64)[None, :]
    ang = (2.0 * math.pi / L) * pos * bands
    feat = np.concatenate([t, np.cos(ang), -np.sin(ang)], axis=-1)
    assert feat.shape[1] == n_feat
    feat_rev = np.concatenate([np.zeros((1, n_feat)), feat[:0:-1]], axis=0)
    n_feat_pad = -(-n_feat // LANES) * LANES
    feat = np.pad(feat, ((0, 0), (0, n_feat_pad - n_feat)))
    feat_rev = np.pad(feat_rev, ((0, 0), (0, n_feat_pad - n_feat)))
    filt_w1 = jnp.pad(filt_w1, ((0, 0), (0, n_feat_pad - n_feat), (0, 0)))
    n_feat = n_feat_pad
    max_decay = math.log(DECAY_TARGET) / FAST_DECAY_PCT
    min_decay = math.log(DECAY_TARGET) / SLOW_DECAY_PCT
    deltas = np.abs(np.linspace(min_decay, max_decay, hw, dtype=np.float32))[None, :]
    f1, f2, _, _ = _fft_constants(seq_len)
    f1 = _interleaved_block_diag(f1)
    nct = hw // ct
    nslab = ct // LANES
    assert L % FFT_N2 == 0 and (L < 512 or L % 512 == 0)

    def w3_spec(direction):
        return pl.BlockSpec((1, hid, ct), lambda l, c, o: (l, 0, (direction * HYENA_ORDER + o) * nct + c))

    def const(shape):
        return pl.BlockSpec(shape, lambda l, c, o: (0,) * len(shape), pipeline_mode=pl.Buffered(1))

    vec = lambda a: a.reshape(depth, 1, hid)
    vspec = pl.BlockSpec((1, 1, hid), lambda l, c, o: (l, 0, 0))
    return pl.pallas_call(
        functools.partial(_filter_kernel, seq_len=seq_len),
        grid=(depth, nct, HYENA_ORDER),
        in_specs=[
            const((L, n_feat)), const((L, n_feat)),
            pl.BlockSpec((1, n_feat, hid), lambda l, c, o: (l, 0, 0)), vspec, vspec,
            pl.BlockSpec((1, hid, hid), lambda l, c, o: (l, 0, 0)), vspec, vspec,
            w3_spec(0), w3_spec(1),
            pl.BlockSpec((1, ct), lambda l, c, o: (0, c)),
            const(f1.shape), const((2 * FFT_N2, 2 * FFT_N2)),
        ],
        out_specs=pl.BlockSpec((1, 1, nk1, 2 * FFT_N2, ct), lambda l, c, o: (l, o, 0, 0, c)),
        out_shape=jax.ShapeDtypeStruct((depth, HYENA_ORDER, nk1, 2 * FFT_N2, hw), BF16),
        scratch_shapes=[
            pltpu.VMEM((2, L, hid), F32),
            pltpu.VMEM((nslab, n1, FFT_N2, LANES), F32),
            pltpu.VMEM((nslab, 2 * pad, FFT_N2, LANES), F32),
        ],
        compiler_params=_cparams(("arbitrary", "arbitrary", "arbitrary")),
        name="hyena_filter_spectra",
    )(jnp.asarray(feat, F32), jnp.asarray(feat_rev, F32), filt_w1, vec(filt_b1), vec(filt_freq1),
      filt_w2, vec(filt_b2), vec(filt_freq2), filt_w3, filt_w3,
      jnp.asarray(deltas, F32), jnp.asarray(f1, BF16), jnp.asarray(f2, BF16))


def _dwconv3_rows(x, w_ref, b_ref, prev_row=None, next_row=None):
    n = x.shape[0]
    row = lax.broadcasted_iota(jnp.int32, (SUBLANES, 1), 0)
    xm = pltpu.roll(x, 1, axis=0)
    xp = pltpu.roll(x, n - 1, axis=0)
    head = jnp.where(row == 0, 0.0 if prev_row is None else prev_row, xm[:SUBLANES])
    tail = jnp.where(row == SUBLANES - 1, 0.0 if next_row is None else next_row, xp[n - SUBLANES:])
    xm = jnp.concatenate([head, xm[SUBLANES:]], axis=0)
    xp = jnp.concatenate([xp[:n - SUBLANES], tail], axis=0)
    return xm * w_ref[0:1, :] + x * w_ref[1:2, :] + xp * w_ref[2:3, :] + b_ref[...]


def _dwconv3_to_slabs(src_ref, w_ref, b_ref, dst_ref, seq_len, chunk):
    halo = 16
    n_chunks = seq_len // chunk
    for c in range(n_chunks):
        r0 = c * chunk
        x = src_ref[0, r0:r0 + chunk, :].astype(F32)
        prev_row = src_ref[0, r0 - halo:r0, :].astype(F32)[halo - 1:halo] if c > 0 else None
        next_row = src_ref[0, r0 + chunk:r0 + chunk + halo, :].astype(F32)[0:1] if c < n_chunks - 1 else None
        _st_rows(dst_ref, r0, _dwconv3_rows(x, w_ref, b_ref, prev_row, next_row))


def _hyena_conv_kernel(zin_ref, ux_ref, wz_ref, bz_ref, wx_ref, bx_ref, hb_ref, kh_ref,
                       f1_ref, f2_ref, g2_ref, g1_ref, o_ref, z_ref, xg_ref, a_ref, y_ref, *, seq_len, first):
    n_fft, n1, nk1, pad = _fft_dims(seq_len)
    L = seq_len
    half = n1 // 2
    nslab = z_ref.shape[0]
    chunk = min(L, 512)

    if first:
        _dwconv3_to_slabs(zin_ref, wz_ref, bz_ref, z_ref, L, chunk)
    else:
        for c in range(L // chunk):
            _st_rows(z_ref, c * chunk, zin_ref[0, c * chunk:(c + 1) * chunk, :])
    _dwconv3_to_slabs(ux_ref, wx_ref, bx_ref, xg_ref, L, chunk)

    _fft_stage1(z_ref, f1_ref, a_ref)

    u2 = 3 if nk1 % 3 == 0 else 1
    n_trips = nk1 // u2

    def spectrum_product(t):
        ys = []
        for j in range(u2):
            k1 = t * u2 + j
            x = jnp.dot(f2_ref[...], _fft_stage2_rhs(a_ref, k1, pad), preferred_element_type=F32)
            kk = kh_ref[0, 0, k1].astype(F32)
            xr, xi = x[:FFT_N2], x[FFT_N2:]
            kr, ki = kk[:FFT_N2], kk[FFT_N2:]
            ys.append(jnp.concatenate([xr * kr - xi * ki, xr * ki + xi * kr], axis=0).astype(BF16))
        return ys

    def inverse_dots(slot):
        return [jnp.dot(g2_ref[...], y_ref[slot, j], preferred_element_type=F32) for j in range(u2)]

    def store_inverse(t, bks):
        for j, bk in enumerate(bks):
            _st_outer(a_ref, t * u2 + j, bk[:FFT_N2])
            _st_outer(a_ref, pad + t * u2 + j, bk[FFT_N2:])

    def store_products(slot, ys):
        for j, y in enumerate(ys):
            y_ref[slot, j] = y

    store_products(0, spectrum_product(0))

    def body2(t, carry):
        ys = spectrum_product(t)
        bks = inverse_dots((t + 1) % 2)
        store_inverse(t - 1, bks)
        store_products(t % 2, ys)
        return carry
    lax.fori_loop(1, n_trips, body2, 0)
    store_inverse(n_trips - 1, inverse_dots((n_trips - 1) % 2))

    bias = hb_ref[...]
    u3 = 2

    def body3(t, carry):
        gs = [t * u3 + j for j in range(u3)]
        ins = [(_ld_group(a_ref, g).astype(BF16), _ld_group(xg_ref, g), _ld_group(z_ref, g)) for g in gs]
        outs = [xg * (jnp.dot(g1_ref[g], bb, preferred_element_type=F32) + bias * z_old)
                for g, (bb, xg, z_old) in zip(gs, ins)]
        for g, z_new in zip(gs, outs):
            _st_group(z_ref, g, z_new)
        return carry
    lax.fori_loop(0, FFT_N2 // FFT_GROUP // u3, body3, 0)

    for c in range(L // chunk):
        blk = jnp.concatenate([z_ref[h, c * chunk // FFT_N2:(c + 1) * chunk // FFT_N2].reshape(chunk, LANES)
                               for h in range(nslab)], axis=1)
        o_ref[0, c * chunk:(c + 1) * chunk, :] = blk.astype(o_ref.dtype)


def hyena_step(zin, zin_col0, u3, x_col0, w_cols, conv_w, conv_b, khat, layer, order, hy_bias_row, first, ct):
    nb, L, _ = u3.shape
    hw = hy_bias_row.shape[1]
    n_fft, n1, nk1, pad = _fft_dims(L)
    half = n1 // 2
    f1, f2, g2, g1 = _fft_constants(L)
    f1 = _interleaved_block_diag(f1[:, :, :half])
    g1 = _interleaved_block_diag(g1)
    nct = hw // ct
    nslab = ct // LANES
    assert zin_col0 % ct == 0 and x_col0 % ct == 0 and w_cols[0] % ct == 0 and w_cols[1] % ct == 0
    last = order == HYENA_ORDER - 1

    def col_spec(rows, col0):
        return pl.BlockSpec((rows, ct), lambda c, b: (0, col0 // ct + c))

    const = lambda shape: pl.BlockSpec(shape, lambda c, b: (0,) * len(shape), pipeline_mode=pl.Buffered(1))
    return pl.pallas_call(
        functools.partial(_hyena_conv_kernel, seq_len=L, first=first),
        grid=(nct, nb),
        in_specs=[
            pl.BlockSpec((1, L, ct), lambda c, b: (b, 0, zin_col0 // ct + c)),
            pl.BlockSpec((1, L, ct), lambda c, b: (b, 0, x_col0 // ct + c)),
            col_spec(3, w_cols[0]), col_spec(1, w_cols[0]), col_spec(3, w_cols[1]), col_spec(1, w_cols[1]),
            pl.BlockSpec((1, ct), lambda c, b: (0, c)),
            pl.BlockSpec((1, 1, nk1, 2 * FFT_N2, ct), lambda c, b: (layer, order, 0, 0, c),
                         pipeline_mode=pl.Buffered(1)),
            const(f1.shape), const((2 * FFT_N2, 2 * FFT_N2)),
            const((2 * FFT_N2, 2 * FFT_N2)), const(g1.shape),
        ],
        out_specs=pl.BlockSpec((1, L, ct), lambda c, b: (b, 0, c)),
        out_shape=jax.ShapeDtypeStruct((nb, L, hw), BF16 if last else F32),
        scratch_shapes=[
            pltpu.VMEM((nslab, half, FFT_N2, LANES), F32),
            pltpu.VMEM((nslab, half, FFT_N2, LANES), F32),
            pltpu.VMEM((nslab, 2 * pad, FFT_N2, LANES), F32),
            pltpu.VMEM((2, 3 if nk1 % 3 == 0 else 1, 2 * FFT_N2, ct), BF16),
        ],
        compiler_params=_cparams(("arbitrary", "arbitrary")),
        name=f"hyena_step{order}",
    )(zin, u3, conv_w, conv_b, conv_w, conv_b, hy_bias_row, khat,
      jnp.asarray(f1, BF16), jnp.asarray(f2, BF16), jnp.asarray(g2, BF16), jnp.asarray(g1, BF16))


def hyena_mix(u3, col0, conv_w, conv_b, khat, layer, hy_bias, ct):
    hw = hy_bias.shape[1]
    z = hyena_step(u3, col0, u3, col0 + hw, (0, hw), conv_w, conv_b, khat, layer, 0, hy_bias[0:1], True, ct)
    return hyena_step(z, 0, u3, col0 + 2 * hw, (0, 2 * hw), conv_w, conv_b, khat, layer, 1, hy_bias[1:2], False, ct)


def _t5_bucket(rel):
    nb = REL_BUCKETS // 2
    ret = (rel > 0).astype(np.int32) * nb
    n = np.abs(rel)
    max_exact = nb // 2
    large = max_exact + (np.log(np.maximum(n, 1) / max_exact) / np.log(REL_MAX_DIST / max_exact)
                         * (nb - max_exact)).astype(np.int32)
    large = np.minimum(large, nb - 1)
    return ret + np.where(n < max_exact, n, large)


ATTN_EDGE_VARIANTS = 4


def _bias_kernel(rb_ref, bkt_ref, o_ref, *, n_sides):
    n_groups = bkt_ref.shape[0]
    qq = lax.broadcasted_iota(jnp.int32, (ATTN_QB, 2 * ATTN_QB), 0)
    kk = lax.broadcasted_iota(jnp.int32, (ATTN_QB, 2 * ATTN_QB), 1)
    for g in range(n_groups):
        n_side = n_sides[g]
        rel = kk - n_side - qq
        band = (rel >= -n_side) & (rel <= n_side)
        before = kk < n_side
        after = kk >= ATTN_QB + n_side
        bkt = bkt_ref[g]
        for h in range(HEADS_PER_GROUP):
            tile = jnp.zeros(bkt.shape, F32)
            for b in range(REL_BUCKETS):
                tile = jnp.where(bkt == b, rb_ref[b, g * HEADS_PER_GROUP + h], tile)
            tile = jnp.where(band, tile, NEG_INF)
            o_ref[g, 0, h] = tile
            o_ref[g, 1, h] = jnp.where(before, NEG_INF, tile)
            o_ref[g, 2, h] = jnp.where(after, NEG_INF, tile)
            o_ref[g, 3, h] = jnp.where(before | after, NEG_INF, tile)


def attention_bias_tiles(rel_bias, n_side_list):
    n_groups = len(ATTN_GROUPS)
    qq = np.arange(ATTN_QB)[:, None]
    kk = np.arange(2 * ATTN_QB)[None, :]
    bkts = []
    for (window, dil), n_side in zip(ATTN_GROUPS, n_side_list):
        j = kk - n_side - qq
        bkts.append(_t5_bucket(j * dil))
    bkt = np.stack(bkts).astype(np.int32)
    return pl.pallas_call(
        functools.partial(_bias_kernel, n_sides=tuple(n_side_list)),
        in_specs=[pl.BlockSpec(memory_space=pltpu.SMEM), pl.BlockSpec(memory_space=pltpu.VMEM)],
        out_specs=pl.BlockSpec(memory_space=pltpu.VMEM),
        out_shape=jax.ShapeDtypeStruct((n_groups, ATTN_EDGE_VARIANTS, HEADS_PER_GROUP, ATTN_QB, 2 * ATTN_QB), F32),
        name="attention_bias_tiles",
    )(rel_bias, jnp.asarray(bkt))


def _attn_kernel(q_ref, k_ref, v_ref, bias_ref, o_ref, l_ref, ks_ref, vs_ref, *, m_len, n_side, tq):
    qt = pl.program_id(2)
    width = HEADS_PER_GROUP * HEAD_DIM
    halo = n_side

    n_cls = q_ref.shape[0]
    n_blk = tq // ATTN_QB

    @pl.when(qt == 0)
    def _():
        zeros = jnp.zeros((halo, width), BF16)
        for c in range(n_cls):
            for ref, src in ((ks_ref, k_ref), (vs_ref, v_ref)):
                ref[c, pl.ds(0, halo), :] = zeros
                ref[c, pl.ds(halo + m_len, 2 * ATTN_QB - halo), :] = jnp.zeros((2 * ATTN_QB - halo, width), BF16)
                ref[c, pl.ds(halo, m_len), :] = src[c]

    lane = lax.broadcasted_iota(jnp.int32, (ATTN_QB, 2 * HEAD_DIM), 1)
    low = lane < HEAD_DIM
    scale = 1.0 / math.sqrt(HEAD_DIM)

    def block(idx, carry):
        c = idx // n_blk
        r0 = pl.multiple_of((idx % n_blk) * ATTN_QB, ATTN_QB)
        q0 = pl.multiple_of(qt * tq + r0, ATTN_QB)
        variant = (q0 == 0).astype(jnp.int32) + 2 * (q0 == m_len - ATTN_QB).astype(jnp.int32)
        for hp in range(HEADS_PER_GROUP // 2):
            cols = pl.ds(hp * 2 * HEAD_DIM, 2 * HEAD_DIM)
            qp = q_ref[c, pl.ds(r0, ATTN_QB), cols] * scale
            kp = ks_ref[c, pl.ds(q0, 2 * ATTN_QB), cols]
            vp = vs_ref[c, pl.ds(q0, 2 * ATTN_QB), cols]
            outs, lses = [], []
            for hh in range(2):
                sel = low if hh == 0 else jnp.logical_not(low)
                qm = jnp.where(sel, qp, jnp.zeros_like(qp))
                s = lax.dot_general(qm, kp, (((1,), (1,)), ((), ())), preferred_element_type=F32)
                s = s + bias_ref[variant, 2 * hp + hh]
                mx = jnp.max(s, axis=-1, keepdims=True)
                p = jnp.exp(s - mx)
                den = jnp.sum(p, axis=-1, keepdims=True)
                pv = jnp.dot(p.astype(BF16), vp, preferred_element_type=F32)
                outs.append(pv / den)
                lses.append(mx + jnp.log(den))
            o_ref[c, pl.ds(r0, ATTN_QB), cols] = jnp.where(low, outs[0], outs[1]).astype(o_ref.dtype)
            l_ref[c, pl.ds(r0, ATTN_QB), cols] = jnp.where(low, lses[0], lses[1])
        return carry
    n_trip_blocks = n_cls * n_blk
    lax.fori_loop(0, n_trip_blocks, block, 0, unroll=next(u for u in (4, 2, 1) if n_trip_blocks % u == 0))


def dilated_group_attention(qkv, g, dil, n_side, bias_tiles):
    nb, d, m_len, _ = qkv.shape
    width = HEADS_PER_GROUP * HEAD_DIM
    assert d == dil and m_len % ATTN_QB == 0 and n_side <= ATTN_QB and n_side % 16 == 0
    tq = min(ATTN_ROWS_PER_STEP, m_len)
    cg = math.gcd(dil, ATTN_ROWS_PER_STEP // tq)
    kernel = functools.partial(_attn_kernel, m_len=m_len, n_side=n_side, tq=tq)
    return pl.pallas_call(
        kernel,
        grid=(nb, dil // cg, m_len // tq),
        in_specs=[
            pl.BlockSpec((None, cg, tq, width), lambda b, r, t: (b, r, t, 0)),
            pl.BlockSpec((None, cg, m_len, width), lambda b, r, t: (b, r, 0, 1)),
            pl.BlockSpec((None, cg, m_len, width), lambda b, r, t: (b, r, 0, 2)),
            pl.BlockSpec((None, ATTN_EDGE_VARIANTS, HEADS_PER_GROUP, ATTN_QB, 2 * ATTN_QB),
                         lambda b, r, t: (g, 0, 0, 0, 0)),
        ],
        out_specs=[
            pl.BlockSpec((None, cg, tq, width), lambda b, r, t: (b, r, t, 0)),
            pl.BlockSpec((None, cg, tq, width), lambda b, r, t: (b, r, t, 0)),
        ],
        out_shape=[
            jax.ShapeDtypeStruct((nb, dil, m_len, width), BF16),
            jax.ShapeDtypeStruct((nb, dil, m_len, width), F32),
        ],
        scratch_shapes=[
            pltpu.VMEM((cg, m_len + 2 * ATTN_QB, width), BF16),
            pltpu.VMEM((cg, m_len + 2 * ATTN_QB, width), BF16),
        ],
        compiler_params=_cparams(("arbitrary", "arbitrary", "arbitrary")),
        name=f"dilated_attention_g{g}",
    )(qkv, qkv, qkv, bias_tiles)


def _merge_kernel(yh_ref, *rest, dils):
    ng = len(dils)
    o_refs, l_refs = rest[:ng], rest[ng:2 * ng]
    gh_ref, ga_ref, bh_ref, ba_ref, wh_ref, wa_ref, out_ref, ya_ref, tok_ref = rest[2 * ng:]
    tm, aw = ya_ref.shape
    nslab = aw // LANES

    @pl.when(pl.program_id(1) == 0)
    def _():
        for gi, d in enumerate(dils):
            if d == 1:
                continue
            rows = tm // d
            for r in range(d):
                for s in range(nslab):
                    cols = slice(s * LANES, (s + 1) * LANES)
                    tok_ref[2 * gi, s, pl.ds(r, rows, stride=d), :] = o_refs[gi][r, :, cols].astype(F32)
                    tok_ref[2 * gi + 1, s, pl.ds(r, rows, stride=d), :] = l_refs[gi][r, :, cols]
        for s in range(nslab):
            cols = slice(s * LANES, (s + 1) * LANES)
            os_, ls_ = [], []
            for gi, d in enumerate(dils):
                if d == 1:
                    os_.append(o_refs[gi][0, :, cols].astype(F32))
                    ls_.append(l_refs[gi][0, :, cols])
                else:
                    os_.append(tok_ref[2 * gi, s])
                    ls_.append(tok_ref[2 * gi + 1, s])
            mx = functools.reduce(jnp.maximum, ls_)
            es = [jnp.exp(l - mx) for l in ls_]
            inv = 1.0 / functools.reduce(lambda a, b: a + b, es)
            ya = functools.reduce(lambda a, b: a + b, [(e * inv) * o for e, o in zip(es, os_)])
            ya_ref[:, cols] = ya.astype(BF16)

    acc_h = jnp.dot(yh_ref[...], wh_ref[...], preferred_element_type=F32)
    acc_a = jnp.dot(ya_ref[...], wa_ref[...], preferred_element_type=F32)
    t_h = jnp.tanh(0.5 * (gh_ref[...].astype(F32) + bh_ref[...]))
    t_a = jnp.tanh(0.5 * (ga_ref[...].astype(F32) + ba_ref[...]))
    out_ref[...] = (0.5 * ((1.0 + t_h) * acc_h + (1.0 + t_a) * acc_a)).astype(out_ref.dtype)


def branch_merge(y_hy, outs, lses, dils, u2, gate_col0, b_gate, w_hy, w_at, seq_len, tm=512):
    t, hw = y_hy.shape
    d = w_hy.shape[1]
    aw = w_at.shape[0]
    tn = _pick_tile(math.gcd(gate_col0, d) if gate_col0 else d, d)
    gb = gate_col0 // tn
    nj = d // tn
    tps = seq_len // tm
    assert all(tm % (dd * 2 * SUBLANES) == 0 for dd in dils)
    grp = [pl.BlockSpec((None, dd, tm // dd, aw), lambda i, j: (i // tps, 0, i % tps, 0)) for dd in dils]
    w_mode = dict(pipeline_mode=pl.Buffered(1)) if nj == 1 else {}
    return pl.pallas_call(
        functools.partial(_merge_kernel, dils=tuple(dils)),
        grid=(t // tm, nj),
        in_specs=[pl.BlockSpec((tm, hw), lambda i, j: (i, 0))] + grp + grp + [
            pl.BlockSpec((tm, tn), lambda i, j: (i, gb + j)),
            pl.BlockSpec((tm, tn), lambda i, j: (i, gb + nj + j)),
            pl.BlockSpec((1, tn), lambda i, j: (0, j)),
            pl.BlockSpec((1, tn), lambda i, j: (0, nj + j)),
            pl.BlockSpec((hw, tn), lambda i, j: (0, j), **w_mode),
            pl.BlockSpec((aw, tn), lambda i, j: (0, j), **w_mode),
        ],
        out_specs=pl.BlockSpec((tm, tn), lambda i, j: (i, j)),
        out_shape=jax.ShapeDtypeStruct((t, d), BF16),
        scratch_shapes=[pltpu.VMEM((tm, aw), BF16), pltpu.VMEM((2 * len(dils), aw // LANES, tm, LANES), F32)],
        compiler_params=_cparams(("arbitrary", "arbitrary")),
        name="branch_merge",
    )(y_hy, *outs, *lses, u2, u2, b_gate, b_gate, w_hy, w_at)


def _out_proj_kernel(x_ref, m_ref, gate_ref, w_ref, o_ref):
    acc = jnp.dot(m_ref[...], w_ref[...], preferred_element_type=F32)
    o_ref[...] = x_ref[...] + gate_ref[0] * acc


def out_proj_residual(x, m, mod3, gate_idx, w, seq_len, tm=512, tn=None, name="out_proj_residual"):
    t, d = x.shape
    k = m.shape[1]
    tn = d if tn is None else tn
    nj = d // tn
    tiles_per_seq = seq_len // tm
    return pl.pallas_call(
        _out_proj_kernel,
        grid=(t // tm, nj),
        in_specs=[
            pl.BlockSpec((tm, tn), lambda i, j: (i, j)),
            pl.BlockSpec((tm, k), lambda i, j: (i, 0)),
            pl.BlockSpec((1, 1, tn), lambda i, j: (i // tiles_per_seq, 0, gate_idx * nj + j)),
            pl.BlockSpec((k, tn), lambda i, j: (0, j)),
        ],
        out_specs=pl.BlockSpec((tm, tn), lambda i, j: (i, j)),
        out_shape=jax.ShapeDtypeStruct((t, d), F32),
        input_output_aliases={0: 0},
        compiler_params=_cparams(("arbitrary", "arbitrary")),
        name=name,
    )(x, m, mod3, w)


def _final_norm_kernel(x_ref, g_ref, o_ref):
    x = x_ref[...]
    ms = jnp.mean(x * x, axis=-1, keepdims=True)
    o_ref[...] = x * lax.rsqrt(ms + NORM_EPS) * g_ref[...]


def final_norm(x, g, row0, n_rows, tm=512):
    d = x.shape[1]
    off = row0 // tm
    return pl.pallas_call(
        _final_norm_kernel,
        grid=(n_rows // tm,),
        in_specs=[pl.BlockSpec((tm, d), lambda i: (off + i, 0)), pl.BlockSpec((1, d), lambda i: (0, 0))],
        out_specs=pl.BlockSpec((tm, d), lambda i: (i, 0)),
        out_shape=jax.ShapeDtypeStruct((n_rows, d), F32),
        compiler_params=_cparams(("arbitrary",)),
        name="final_norm",
    )(x, g.reshape(1, d))


def kernel(x_prompt, x_sample, c_prompt, c_sample, ada_w, ada_b, norm1_g, w_in, b_gate, hy_conv_w, hy_conv_b,
           filt_w1, filt_b1, filt_freq1, filt_w2, filt_b2, filt_freq2, filt_w3, hy_bias, rel_bias, w_br_hy,
           w_br_attn, w_out, norm2_g, ffn_up, ffn_conv_w, ffn_conv_b, ffn_down, final_g):
    bp, L, d = x_prompt.shape
    bs = x_sample.shape[0]
    assert x_sample.shape[1] == L
    nb = bp + bs
    depth = ada_w.shape[0]
    hw = hy_bias.shape[2]
    n_groups = len(ATTN_GROUPS)
    attn_w = n_groups * HEADS_PER_GROUP * HEAD_DIM
    hy_cols = 3 * hw
    gate_col0 = hy_cols + 3 * attn_w
    assert w_in.shape[2] == gate_col0 + 2 * d

    x = jnp.concatenate([x_prompt, x_sample], axis=0).reshape(nb * L, d)
    nb_pad = -(-nb // SUBLANES) * SUBLANES
    c_pad = jnp.zeros((nb_pad, d), F32).at[:nb].set(jnp.concatenate([c_prompt, c_sample], axis=0))
    mod = ada_modulation(c_pad, ada_w, ada_b)

    hy_ct = min(HYENA_CT, hw)
    khat = hyena_filter_spectra(L, filt_w1, filt_b1, filt_freq1, filt_w2, filt_b2, filt_freq2, filt_w3, hy_ct)
    n_sides = [(window // 2) // dil for window, dil in ATTN_GROUPS]
    bias_tiles = attention_bias_tiles(rel_bias, n_sides)

    dils = [dil for _, dil in ATTN_GROUPS]
    gw = HEADS_PER_GROUP * HEAD_DIM

    def group_cols(w, g):
        return [w[:, hy_cols + part * attn_w + g * gw: hy_cols + part * attn_w + (g + 1) * gw] for part in range(3)]

    for l in range(depth):
        mod3 = mod[l].reshape(nb_pad, 1, N_MOD * d)
        w_l = w_in[l]
        w_main = jnp.concatenate([w_l[:, gate_col0:], w_l[:, :hy_cols]], axis=1).astype(BF16)
        w_groups = jnp.concatenate([c for g in range(n_groups) for c in group_cols(w_l, g)], axis=1).astype(BF16)
        u, qkv = in_projection(x, norm1_g[l], mod3, 1, 0, w_main, w_groups, dils, nb, L)
        u3 = u.reshape(nb, L, u.shape[1])
        y_hy = hyena_mix(u3, 2 * d, hy_conv_w[l], hy_conv_b[l].reshape(1, hy_cols), khat, l, hy_bias[l], hy_ct)
        outs, lses = [], []
        for g, dil in enumerate(dils):
            o_g, l_g = dilated_group_attention(qkv[g], g, dil, n_sides[g], bias_tiles)
            outs.append(o_g)
            lses.append(l_g)
        merged = branch_merge(y_hy.reshape(nb * L, hw), outs, lses, dils, u, 0, b_gate[l].reshape(1, 2 * d),
                              w_br_hy[l].astype(BF16), w_br_attn[l].astype(BF16), L)
        x = out_proj_residual(x, merged, mod3, 2, w_out[l].astype(BF16), L)
        ff = ffn_down.shape[1]
        act = ffn_up_act(x, norm2_g[l], mod3, 4, 3, interleave_ffn_up(ffn_up[l].astype(BF16)),
                         ffn_conv_w[l], ffn_conv_b[l].reshape(1, ff), L)
        x = out_proj_residual(x, act, mod3, 5, ffn_down[l].astype(BF16), L, tm=min(1024, L),
                              tn=_pick_tile(d, 512), name="ffn_down_residual")

    y_prompt = final_norm(x, final_g, 0, bp * L).reshape(bp, L, d)
    y_sample = final_norm(x, final_g, bp * L, bs * L).reshape(bs, L, d)
    return (y_prompt, y_sample)
```

```python
import functools
import math

import jax
import jax.numpy as jnp
import numpy as np
from jax import lax
from jax.experimental import pallas as pl
from jax.experimental.pallas import tpu as pltpu

F32 = jnp.float32
BF16 = jnp.bfloat16
HIGHEST = lax.Precision.HIGHEST

NORM_EPS = 1e-6
N_MOD = 6
HYENA_ORDER = 2
N_BANDS = 16
FAST_DECAY_PCT = 0.3
SLOW_DECAY_PCT = 1.5
DECAY_TARGET = 1e-2
ATTN_GROUPS = ((128, 1), (512, 4), (2048, 16))
HEADS_PER_GROUP = 8
HEAD_DIM = 64
REL_BUCKETS = 32
REL_MAX_DIST = 1024
NEG_INF = -1e30

LANES = 128
SUBLANES = 8
VMEM_LIMIT = 56 * 1024 * 1024

FFT_N2 = 128
FFT_GROUP = SUBLANES
FFT_TRIP_GROUPS = 8
HYENA_CT = 256
ATTN_QB = 128
ATTN_ROWS_PER_STEP = 1024


def _cparams(sem):
    return pltpu.CompilerParams(dimension_semantics=sem, vmem_limit_bytes=VMEM_LIMIT)


def _pick_tile(n, cap):
    best = None
    for t in range(LANES, min(n, cap) + 1, LANES):
        if n % t == 0:
            best = t
    assert best is not None, (n, cap)
    return best


def _ada_kernel(c_ref, w_ref, b_ref, o_ref):
    c = c_ref[...]
    cs = c * (1.0 / (1.0 + jnp.exp(-c)))
    o_ref[0] = jnp.dot(cs, w_ref[0], preferred_element_type=F32, precision=HIGHEST) + b_ref[0]


def ada_modulation(c_pad, ada_w, ada_b):
    depth, d, n = ada_w.shape
    nb = c_pad.shape[0]
    tn = _pick_tile(n, 1024)
    return pl.pallas_call(
        _ada_kernel,
        grid=(depth, n // tn),
        in_specs=[
            pl.BlockSpec((nb, d), lambda l, j: (0, 0)),
            pl.BlockSpec((1, d, tn), lambda l, j: (l, 0, j)),
            pl.BlockSpec((1, 1, tn), lambda l, j: (l, 0, j)),
        ],
        out_specs=pl.BlockSpec((1, nb, tn), lambda l, j: (l, 0, j)),
        out_shape=jax.ShapeDtypeStruct((depth, nb, n), F32),
        compiler_params=_cparams(("arbitrary", "arbitrary")),
        name="ada_modulation",
    )(c_pad, ada_w, ada_b.reshape(depth, 1, n))


def _normmod_rows(x, g, sc, sh):
    ms = jnp.mean(x * x, axis=-1, keepdims=True)
    return (x * lax.rsqrt(ms + NORM_EPS)) * (g * (1.0 + sc)) + sh


FFN_HALO = 16


def _ffn_up_act_kernel(x_ref, xp_ref, xn_ref, g_ref, sc_ref, sh_ref, w_ref, cw_ref, cb_ref,
                       o_ref, h_ref, *, tiles_per_seq):
    i = pl.program_id(0)
    tm = x_ref.shape[0]

    @pl.when(pl.program_id(1) == 0)
    def _():
        g, sc, sh = g_ref[...], sc_ref[0], sh_ref[0]
        h_ref[0:FFN_HALO] = _normmod_rows(xp_ref[...], g, sc, sh).astype(BF16)
        h_ref[FFN_HALO:FFN_HALO + tm] = _normmod_rows(x_ref[...], g, sc, sh).astype(BF16)
        h_ref[FFN_HALO + tm:] = _normmod_rows(xn_ref[...], g, sc, sh).astype(BF16)

    first = (i % tiles_per_seq) == 0
    last = (i % tiles_per_seq) == tiles_per_seq - 1
    row = lax.broadcasted_iota(jnp.int32, (tm + 2 * FFN_HALO, 1), 0)
    outside = ((row == FFN_HALO - 1) & first) | ((row == FFN_HALO + tm) & last)
    for c in range(o_ref.shape[1] // LANES):
        res = jnp.dot(h_ref[...], w_ref[:, 2 * c * LANES:2 * (c + 1) * LANES], preferred_element_type=F32)
        a = res[FFN_HALO:FFN_HALO + tm, :LANES]
        gt = jnp.where(outside, 0.0, res[:, LANES:])
        cols = slice(c * LANES, (c + 1) * LANES)
        conv = _dwconv3_rows(gt, cw_ref.at[:, cols], cb_ref.at[:, cols])[FFN_HALO:FFN_HALO + tm]
        half = 0.5 * conv
        o_ref[:, cols] = (half * (1.0 + jnp.tanh(half)) * a).astype(o_ref.dtype)


def interleave_ffn_up(w_up):
    ff = w_up.shape[1] // 2
    chunks = []
    for c in range(ff // LANES):
        chunks += [w_up[:, c * LANES:(c + 1) * LANES], w_up[:, ff + c * LANES:ff + (c + 1) * LANES]]
    return jnp.concatenate(chunks, axis=1)


def ffn_up_act(x, g, mod3, sc_idx, sh_idx, w_il, conv_w, conv_b, seq_len, tm=512, tn_cap=1536):
    t, d = x.shape
    ff = w_il.shape[1] // 2
    tn = _pick_tile(ff, tn_cap)
    nj = ff // tn
    tps = seq_len // tm
    rb = tm // FFN_HALO
    last_rb = t // FFN_HALO - 1
    return pl.pallas_call(
        functools.partial(_ffn_up_act_kernel, tiles_per_seq=tps),
        grid=(t // tm, nj),
        in_specs=[
            pl.BlockSpec((tm, d), lambda i, j: (i, 0)),
            pl.BlockSpec((FFN_HALO, d), lambda i, j: (jnp.maximum(i * rb - 1, 0), 0)),
            pl.BlockSpec((FFN_HALO, d), lambda i, j: (jnp.minimum((i + 1) * rb, last_rb), 0)),
            pl.BlockSpec((1, d), lambda i, j: (0, 0)),
            pl.BlockSpec((1, 1, d), lambda i, j: (i // tps, 0, sc_idx)),
            pl.BlockSpec((1, 1, d), lambda i, j: (i // tps, 0, sh_idx)),
            pl.BlockSpec((d, 2 * tn), lambda i, j: (0, j)),
            pl.BlockSpec((3, tn), lambda i, j: (0, j)),
            pl.BlockSpec((1, tn), lambda i, j: (0, j)),
        ],
        out_specs=pl.BlockSpec((tm, tn), lambda i, j: (i, j)),
        out_shape=jax.ShapeDtypeStruct((t, ff), BF16),
        scratch_shapes=[pltpu.VMEM((tm + 2 * FFN_HALO, d), BF16)],
        compiler_params=_cparams(("arbitrary", "arbitrary")),
        name="ffn_up_act",
    )(x, x, x, g.reshape(1, d), mod3, mod3, w_il, conv_w, conv_b)


def _in_proj_kernel(x_ref, g_ref, sc_ref, sh_ref, wm_ref, wg_ref, om_ref, *rest, n_main, dils, chunk):
    og_refs = rest[:len(dils)]
    h_ref, perm_ref = rest[len(dils):]
    j = pl.program_id(1)
    tm = h_ref.shape[0]

    @pl.when(j == 0)
    def _():
        h_ref[...] = _normmod_rows(x_ref[...], g_ref[...], sc_ref[0], sh_ref[0]).astype(BF16)

    @pl.when(j < n_main)
    def _():
        om_ref[...] = jnp.dot(h_ref[...], wm_ref[...], preferred_element_type=F32).astype(om_ref.dtype)

    for gi, (o_ref, d) in enumerate(zip(og_refs, dils)):
        @pl.when(j == n_main + gi)
        def _():
            rows = tm // d
            for c0 in range(0, wg_ref.shape[1], chunk):
                res = jnp.dot(h_ref[...], wg_ref[:, c0:c0 + chunk], preferred_element_type=F32)
                if d == 1:
                    o_ref[0, 0, :, c0:c0 + chunk] = res.astype(o_ref.dtype)
                    continue
                for s in range(chunk // LANES):
                    perm_ref[s] = res[:, s * LANES:(s + 1) * LANES]
                for r in range(d):
                    for s in range(chunk // LANES):
                        cols = slice(c0 + s * LANES, c0 + (s + 1) * LANES)
                        o_ref[0, r, :, cols] = perm_ref[s, pl.ds(r, rows, stride=d), :].astype(o_ref.dtype)


def in_projection(x, g, mod3, sc_idx, sh_idx, w_main, w_groups, dils, nb, seq_len, tm=512, tn_cap=1792):
    t, d_model = x.shape
    n_main = w_main.shape[1]
    gw = w_groups.shape[1] // len(dils)
    tn = _pick_tile(n_main, tn_cap)
    nj_main = n_main // tn
    tps = seq_len // tm
    chunk = gw // 3
    assert chunk % LANES == 0 and all(tm % (dd * 16) == 0 for dd in dils)
    kernel = functools.partial(_in_proj_kernel, n_main=nj_main, dils=tuple(dils), chunk=chunk)
    outs = pl.pallas_call(
        kernel,
        grid=(t // tm, nj_main + len(dils)),
        in_specs=[
            pl.BlockSpec((tm, d_model), lambda i, j: (i, 0)),
            pl.BlockSpec((1, d_model), lambda i, j: (0, 0)),
            pl.BlockSpec((1, 1, d_model), lambda i, j: (i // tps, 0, sc_idx)),
            pl.BlockSpec((1, 1, d_model), lambda i, j: (i // tps, 0, sh_idx)),
            pl.BlockSpec((d_model, tn), lambda i, j: (0, jnp.minimum(j, nj_main - 1))),
            pl.BlockSpec((d_model, gw), lambda i, j: (0, jnp.clip(j - nj_main, 0, len(dils) - 1))),
        ],
        out_specs=[pl.BlockSpec((tm, tn), lambda i, j: (i, jnp.minimum(j, nj_main - 1)))] + [
            pl.BlockSpec((1, dd, tm // dd, gw), lambda i, j: (i // tps, 0, i % tps, 0)) for dd in dils],
        out_shape=[jax.ShapeDtypeStruct((t, n_main), BF16)] + [
            jax.ShapeDtypeStruct((nb, dd, seq_len // dd, gw), BF16) for dd in dils],
        scratch_shapes=[pltpu.VMEM((tm, d_model), BF16), pltpu.VMEM((chunk // LANES, tm, LANES), F32)],
        compiler_params=_cparams(("arbitrary", "arbitrary")),
        name="in_projection",
    )(x, g.reshape(1, d_model), mod3, mod3, w_main, w_groups)
    return outs[0], list(outs[1:])


def _fft_dims(seq_len):
    n_fft = 2 * seq_len
    n1 = n_fft // FFT_N2
    assert n1 * FFT_N2 == n_fft and n1 % 2 == 0
    nk1 = n1 // 2 + 1
    pad = -(-nk1 // SUBLANES) * SUBLANES
    return n_fft, n1, nk1, pad


@functools.lru_cache(maxsize=None)
def _fft_constants(seq_len):
    n_fft, n1, nk1, pad = _fft_dims(seq_len)
    n2 = FFT_N2
    k1 = np.arange(nk1)[None, :, None]
    i1 = np.arange(n1)[None, None, :]
    i2 = np.arange(n2)[:, None, None]
    ph = 2.0 * np.pi * ((k1 * (i1 * n2 + i2)) % n_fft) / n_fft
    f1 = np.zeros((n2, 2 * pad, n1), np.float64)
    f1[:, :nk1] = np.cos(ph)
    f1[:, pad:pad + nk1] = -np.sin(ph)
    ck = np.full((nk1,), 2.0)
    ck[0] = 1.0
    ck[-1] = 1.0
    g1 = np.zeros((n2, n1 // 2, 2 * pad), np.float64)
    pht = np.transpose(ph[:, :, :n1 // 2], (0, 2, 1))
    g1[:, :, :nk1] = np.cos(pht) * ck / n_fft
    g1[:, :, pad:pad + nk1] = -np.sin(pht) * ck / n_fft
    a = np.arange(n2)
    ph2 = 2.0 * np.pi * ((a[:, None] * a[None, :]) % n2) / n2
    c2, s2 = np.cos(ph2), np.sin(ph2)
    f2 = np.block([[c2, s2], [-s2, c2]])
    g2 = np.block([[c2, -s2], [s2, c2]])
    return (f1.astype(np.float32), f2.astype(np.float32), g2.astype(np.float32), g1.astype(np.float32))


def _interleaved_block_diag(mats):
    n, r, c = mats.shape
    out = np.zeros((n // FFT_GROUP, r * FFT_GROUP, c * FFT_GROUP), mats.dtype)
    for j in range(FFT_GROUP):
        out[:, j::FFT_GROUP, j::FFT_GROUP] = mats[j::FFT_GROUP]
    return out


def _group_rows(g):
    return pl.ds(pl.multiple_of(g * FFT_GROUP, FFT_GROUP), FFT_GROUP)


def _ld_group(ref, g):
    n_outer = ref.shape[1]
    return jnp.concatenate([ref[h, :, _group_rows(g), :].reshape(n_outer * FFT_GROUP, LANES)
                            for h in range(ref.shape[0])], axis=1)


def _st_group(ref, g, val):
    n_outer = ref.shape[1]
    for h in range(ref.shape[0]):
        ref[h, :, _group_rows(g), :] = val[:, h * LANES:(h + 1) * LANES].reshape(n_outer, FFT_GROUP, LANES)


def _ld_outer(ref, idx):
    return jnp.concatenate([ref[h, idx] for h in range(ref.shape[0])], axis=1)


def _st_outer(ref, idx, val):
    for h in range(ref.shape[0]):
        ref[h, idx] = val[:, h * LANES:(h + 1) * LANES]


def _st_rows(ref, row0, val):
    n = val.shape[0]
    for h in range(ref.shape[0]):
        ref[h, row0 // FFT_N2:(row0 + n) // FFT_N2] = val[:, h * LANES:(h + 1) * LANES].reshape(
            n // FFT_N2, FFT_N2, LANES)


def _fft_stage1(src_ref, f1b_ref, a_ref):
    def body(g, carry):
        a = jnp.dot(f1b_ref[g], _ld_group(src_ref, g).astype(BF16), preferred_element_type=F32)
        _st_group(a_ref, g, a)
        return carry
    lax.fori_loop(0, FFT_N2 // FFT_GROUP, body, 0, unroll=FFT_TRIP_GROUPS)


def _fft_stage2_rhs(a_ref, k1, pad):
    return jnp.concatenate([_ld_outer(a_ref, k1), _ld_outer(a_ref, pad + k1)], axis=0).astype(BF16)


def _filter_kernel(feat_ref, featr_ref, w1_ref, b1_ref, fr1_ref, w2_ref, b2_ref, fr2_ref,
                   w3f_ref, w3b_ref, delta_ref, f1_ref, f2_ref,
                   o_ref, hid_ref, k2_ref, a_ref, *, seq_len):
    n_fft, n1, nk1, pad = _fft_dims(seq_len)
    L = seq_len

    @pl.when((pl.program_id(1) == 0) & (pl.program_id(2) == 0))
    def _():
        for d, fref in enumerate((feat_ref, featr_ref)):
            h = jnp.dot(fref[...], w1_ref[0], preferred_element_type=F32, precision=HIGHEST) + b1_ref[0]
            h = jnp.sin(fr1_ref[0] * h)
            h = jnp.dot(h, w2_ref[0], preferred_element_type=F32, precision=HIGHEST) + b2_ref[0]
            hid_ref[d] = jnp.sin(fr2_ref[0] * h)

    delta = delta_ref[...]
    chunk = min(L, 512)
    row = lax.broadcasted_iota(jnp.int32, (chunk, 1), 0)
    l1 = jnp.zeros(delta.shape, F32)
    for d, (w3_ref, fref) in enumerate(((w3f_ref, feat_ref), (w3b_ref, featr_ref))):
        for c in range(L // chunk):
            rows = pl.ds(c * chunk, chunk)
            h = jnp.dot(hid_ref[d, rows, :], w3_ref[0], preferred_element_type=F32, precision=HIGHEST)
            h = h * jnp.exp(-fref[rows, 0:1] * delta)
            if d == 1 and c == 0:
                h = jnp.where(row == 0, 0.0, h)
            l1 = l1 + jnp.sum(jnp.abs(h), axis=0, keepdims=True)
            _st_rows(k2_ref, d * L + c * chunk, h)
    inv = 1.0 / l1
    _fft_stage1(k2_ref, f1_ref, a_ref)

    def body(k1, carry):
        spec = jnp.dot(f2_ref[...], _fft_stage2_rhs(a_ref, k1, pad), preferred_element_type=F32) * inv
        o_ref[0, 0, k1] = spec.astype(o_ref.dtype)
        return carry
    lax.fori_loop(0, nk1, body, 0, unroll=3 if nk1 % 3 == 0 else 1)


def hyena_filter_spectra(seq_len, filt_w1, filt_b1, filt_freq1, filt_w2, filt_b2, filt_freq2, filt_w3, ct):
    depth, n_feat, hid = filt_w1.shape
    hw = filt_w3.shape[2] // (2 * HYENA_ORDER)
    n_fft, n1, nk1, pad = _fft_dims(seq_len)
    L = seq_len
    t = np.linspace(0.0, 1.0, L, dtype=np.float32).astype(np.float64)[:, None]
    pos = np.arange(L, dtype=np.float64)[:, None]
    bands = np.linspace(1e-4, N_BANDS - 1, N_BANDS, dtype=np.float32).astype(np.float64)[None, :]
    ang = (2.0 * math.pi / L) * pos * bands
    feat = np.concatenate([t, np.cos(ang), -np.sin(ang)], axis=-1)
    assert feat.shape[1] == n_feat
    feat_rev = np.concatenate([np.zeros((1, n_feat)), feat[:0:-1]], axis=0)
    n_feat_pad = -(-n_feat // LANES) * LANES
    feat = np.pad(feat, ((0, 0), (0, n_feat_pad - n_feat)))
    feat_rev = np.pad(feat_rev, ((0, 0), (0, n_feat_pad - n_feat)))
    filt_w1 = jnp.pad(filt_w1, ((0, 0), (0, n_feat_pad - n_feat), (0, 0)))
    n_feat = n_feat_pad
    max_decay = math.log(DECAY_TARGET) / FAST_DECAY_PCT
    min_decay = math.log(DECAY_TARGET) / SLOW_DECAY_PCT
    deltas = np.abs(np.linspace(min_decay, max_decay, hw, dtype=np.float32))[None, :]
    f1, f2, _, _ = _fft_constants(seq_len)
    f1 = _interleaved_block_diag(f1)
    nct = hw // ct
    nslab = ct // LANES
    assert L % FFT_N2 == 0 and (L < 512 or L % 512 == 0)

    def w3_spec(direction):
        return pl.BlockSpec((1, hid, ct), lambda l, c, o: (l, 0, (direction * HYENA_ORDER + o) * nct + c))

    def const(shape):
        return pl.BlockSpec(shape, lambda l, c, o: (0,) * len(shape), pipeline_mode=pl.Buffered(1))

    vec = lambda a: a.reshape(depth, 1, hid)
    vspec = pl.BlockSpec((1, 1, hid), lambda l, c, o: (l, 0, 0))
    return pl.pallas_call(
        functools.partial(_filter_kernel, seq_len=seq_len),
        grid=(depth, nct, HYENA_ORDER),
        in_specs=[
            const((L, n_feat)), const((L, n_feat)),
            pl.BlockSpec((1, n_feat, hid), lambda l, c, o: (l, 0, 0)), vspec, vspec,
            pl.BlockSpec((1, hid, hid), lambda l, c, o: (l, 0, 0)), vspec, vspec,
            w3_spec(0), w3_spec(1),
            pl.BlockSpec((1, ct), lambda l, c, o: (0, c)),
            const(f1.shape), const((2 * FFT_N2, 2 * FFT_N2)),
        ],
        out_specs=pl.BlockSpec((1, 1, nk1, 2 * FFT_N2, ct), lambda l, c, o: (l, o, 0, 0, c)),
        out_shape=jax.ShapeDtypeStruct((depth, HYENA_ORDER, nk1, 2 * FFT_N2, hw), BF16),
        scratch_shapes=[
            pltpu.VMEM((2, L, hid), F32),
            pltpu.VMEM((nslab, n1, FFT_N2, LANES), F32),
            pltpu.VMEM((nslab, 2 * pad, FFT_N2, LANES), F32),
        ],
        compiler_params=_cparams(("arbitrary", "arbitrary", "arbitrary")),
        name="hyena_filter_spectra",
    )(jnp.asarray(feat, F32), jnp.asarray(feat_rev, F32), filt_w1, vec(filt_b1), vec(filt_freq1),
      filt_w2, vec(filt_b2), vec(filt_freq2), filt_w3, filt_w3,
      jnp.asarray(deltas, F32), jnp.asarray(f1, BF16), jnp.asarray(f2, BF16))


def _dwconv3_rows(x, w_ref, b_ref, prev_row=None, next_row=None):
    n = x.shape[0]
    row = lax.broadcasted_iota(jnp.int32, (SUBLANES, 1), 0)
    xm = pltpu.roll(x, 1, axis=0)
    xp = pltpu.roll(x, n - 1, axis=0)
    head = jnp.where(row == 0, 0.0 if prev_row is None else prev_row, xm[:SUBLANES])
    tail = jnp.where(row == SUBLANES - 1, 0.0 if next_row is None else next_row, xp[n - SUBLANES:])
    xm = jnp.concatenate([head, xm[SUBLANES:]], axis=0)
    xp = jnp.concatenate([xp[:n - SUBLANES], tail], axis=0)
    return xm * w_ref[0:1, :] + x * w_ref[1:2, :] + xp * w_ref[2:3, :] + b_ref[...]


def _dwconv3_to_slabs(src_ref, w_ref, b_ref, dst_ref, seq_len, chunk):
    halo = 16
    n_chunks = seq_len // chunk
    for c in range(n_chunks):
        r0 = c * chunk
        x = src_ref[0, r0:r0 + chunk, :].astype(F32)
        prev_row = src_ref[0, r0 - halo:r0, :].astype(F32)[halo - 1:halo] if c > 0 else None
        next_row = src_ref[0, r0 + chunk:r0 + chunk + halo, :].astype(F32)[0:1] if c < n_chunks - 1 else None
        _st_rows(dst_ref, r0, _dwconv3_rows(x, w_ref, b_ref, prev_row, next_row))


def _hyena_conv_kernel(zin_ref, ux_ref, wz_ref, bz_ref, wx_ref, bx_ref, hb_ref, kh_ref,
                       f1_ref, f2_ref, g2_ref, g1_ref, o_ref, z_ref, xg_ref, a_ref, y_ref, *, seq_len, first):
    n_fft, n1, nk1, pad = _fft_dims(seq_len)
    L = seq_len
    half = n1 // 2
    nslab = z_ref.shape[0]
    chunk = min(L, 512)

    if first:
        _dwconv3_to_slabs(zin_ref, wz_ref, bz_ref, z_ref, L, chunk)
    else:
        for c in range(L // chunk):
            _st_rows(z_ref, c * chunk, zin_ref[0, c * chunk:(c + 1) * chunk, :])
    _dwconv3_to_slabs(ux_ref, wx_ref, bx_ref, xg_ref, L, chunk)

    _fft_stage1(z_ref, f1_ref, a_ref)

    u2 = 3 if nk1 % 3 == 0 else 1
    n_trips = nk1 // u2

    def spectrum_product(t):
        ys = []
        for j in range(u2):
            k1 = t * u2 + j
            x = jnp.dot(f2_ref[...], _fft_stage2_rhs(a_ref, k1, pad), preferred_element_type=F32)
            kk = kh_ref[0, 0, k1].astype(F32)
            xr, xi = x[:FFT_N2], x[FFT_N2:]
            kr, ki = kk[:FFT_N2], kk[FFT_N2:]
            ys.append(jnp.concatenate([xr * kr - xi * ki, xr * ki + xi * kr], axis=0).astype(BF16))
        return ys

    def inverse_dots(slot):
        return [jnp.dot(g2_ref[...], y_ref[slot, j], preferred_element_type=F32) for j in range(u2)]

    def store_inverse(t, bks):
        for j, bk in enumerate(bks):
            _st_outer(a_ref, t * u2 + j, bk[:FFT_N2])
            _st_outer(a_ref, pad + t * u2 + j, bk[FFT_N2:])

    def store_products(slot, ys):
        for j, y in enumerate(ys):
            y_ref[slot, j] = y

    store_products(0, spectrum_product(0))

    def body2(t, carry):
        ys = spectrum_product(t)
        bks = inverse_dots((t + 1) % 2)
        store_inverse(t - 1, bks)
        store_products(t % 2, ys)
        return carry
    lax.fori_loop(1, n_trips, body2, 0)
    store_inverse(n_trips - 1, inverse_dots((n_trips - 1) % 2))

    bias = hb_ref[...]
    u3 = FFT_TRIP_GROUPS

    def body3(t, carry):
        gs = [t * u3 + j for j in range(u3)]
        ins = [(_ld_group(a_ref, g).astype(BF16), _ld_group(xg_ref, g), _ld_group(z_ref, g)) for g in gs]
        outs = [xg * (jnp.dot(g1_ref[g], bb, preferred_element_type=F32) + bias * z_old)
                for g, (bb, xg, z_old) in zip(gs, ins)]
        for g, z_new in zip(gs, outs):
            _st_group(z_ref, g, z_new)
        return carry
    lax.fori_loop(0, FFT_N2 // FFT_GROUP // u3, body3, 0)

    for c in range(L // chunk):
        blk = jnp.concatenate([z_ref[h, c * chunk // FFT_N2:(c + 1) * chunk // FFT_N2].reshape(chunk, LANES)
                               for h in range(nslab)], axis=1)
        o_ref[0, c * chunk:(c + 1) * chunk, :] = blk.astype(o_ref.dtype)


def hyena_step(zin, zin_col0, u3, x_col0, w_cols, conv_w, conv_b, khat, layer, order, hy_bias_row, first, ct):
    nb, L, _ = u3.shape
    hw = hy_bias_row.shape[1]
    n_fft, n1, nk1, pad = _fft_dims(L)
    half = n1 // 2
    f1, f2, g2, g1 = _fft_constants(L)
    f1 = _interleaved_block_diag(f1[:, :, :half])
    g1 = _interleaved_block_diag(g1)
    nct = hw // ct
    nslab = ct // LANES
    assert zin_col0 % ct == 0 and x_col0 % ct == 0 and w_cols[0] % ct == 0 and w_cols[1] % ct == 0
    last = order == HYENA_ORDER - 1

    def col_spec(rows, col0):
        return pl.BlockSpec((rows, ct), lambda c, b: (0, col0 // ct + c))

    const = lambda shape: pl.BlockSpec(shape, lambda c, b: (0,) * len(shape), pipeline_mode=pl.Buffered(1))
    return pl.pallas_call(
        functools.partial(_hyena_conv_kernel, seq_len=L, first=first),
        grid=(nct, nb),
        in_specs=[
            pl.BlockSpec((1, L, ct), lambda c, b: (b, 0, zin_col0 // ct + c)),
            pl.BlockSpec((1, L, ct), lambda c, b: (b, 0, x_col0 // ct + c)),
            col_spec(3, w_cols[0]), col_spec(1, w_cols[0]), col_spec(3, w_cols[1]), col_spec(1, w_cols[1]),
            pl.BlockSpec((1, ct), lambda c, b: (0, c)),
            pl.BlockSpec((1, 1, nk1, 2 * FFT_N2, ct), lambda c, b: (layer, order, 0, 0, c),
                         pipeline_mode=pl.Buffered(1)),
            const(f1.shape), const((2 * FFT_N2, 2 * FFT_N2)),
            const((2 * FFT_N2, 2 * FFT_N2)), const(g1.shape),
        ],
        out_specs=pl.BlockSpec((1, L, ct), lambda c, b: (b, 0, c)),
        out_shape=jax.ShapeDtypeStruct((nb, L, hw), BF16 if last else F32),
        scratch_shapes=[
            pltpu.VMEM((nslab, half, FFT_N2, LANES), F32),
            pltpu.VMEM((nslab, half, FFT_N2, LANES), F32),
            pltpu.VMEM((nslab, 2 * pad, FFT_N2, LANES), F32),
            pltpu.VMEM((2, 3 if nk1 % 3 == 0 else 1, 2 * FFT_N2, ct), BF16),
        ],
        compiler_params=_cparams(("arbitrary", "arbitrary")),
        name=f"hyena_step{order}",
    )(zin, u3, conv_w, conv_b, conv_w, conv_b, hy_bias_row, khat,
      jnp.asarray(f1, BF16), jnp.asarray(f2, BF16), jnp.asarray(g2, BF16), jnp.asarray(g1, BF16))


def hyena_mix(u3, col0, conv_w, conv_b, khat, layer, hy_bias, ct):
    hw = hy_bias.shape[1]
    z = hyena_step(u3, col0, u3, col0 + hw, (0, hw), conv_w, conv_b, khat, layer, 0, hy_bias[0:1], True, ct)
    return hyena_step(z, 0, u3, col0 + 2 * hw, (0, 2 * hw), conv_w, conv_b, khat, layer, 1, hy_bias[1:2], False, ct)


def _t5_bucket(rel):
    nb = REL_BUCKETS // 2
    ret = (rel > 0).astype(np.int32) * nb
    n = np.abs(rel)
    max_exact = nb // 2
    large = max_exact + (np.log(np.maximum(n, 1) / max_exact) / np.log(REL_MAX_DIST / max_exact)
                         * (nb - max_exact)).astype(np.int32)
    large = np.minimum(large, nb - 1)
    return ret + np.where(n < max_exact, n, large)


ATTN_EDGE_VARIANTS = 4


def _bias_kernel(rb_ref, bkt_ref, o_ref, *, n_sides):
    n_groups = bkt_ref.shape[0]
    qq = lax.broadcasted_iota(jnp.int32, (ATTN_QB, 2 * ATTN_QB), 0)
    kk = lax.broadcasted_iota(jnp.int32, (ATTN_QB, 2 * ATTN_QB), 1)
    for g in range(n_groups):
        n_side = n_sides[g]
        rel = kk - n_side - qq
        band = (rel >= -n_side) & (rel <= n_side)
        before = kk < n_side
        after = kk >= ATTN_QB + n_side
        bkt = bkt_ref[g]
        for h in range(HEADS_PER_GROUP):
            tile = jnp.zeros(bkt.shape, F32)
            for b in range(REL_BUCKETS):
                tile = jnp.where(bkt == b, rb_ref[b, g * HEADS_PER_GROUP + h], tile)
            tile = jnp.where(band, tile, NEG_INF)
            o_ref[g, 0, h] = tile
            o_ref[g, 1, h] = jnp.where(before, NEG_INF, tile)
            o_ref[g, 2, h] = jnp.where(after, NEG_INF, tile)
            o_ref[g, 3, h] = jnp.where(before | after, NEG_INF, tile)


def attention_bias_tiles(rel_bias, n_side_list):
    n_groups = len(ATTN_GROUPS)
    qq = np.arange(ATTN_QB)[:, None]
    kk = np.arange(2 * ATTN_QB)[None, :]
    bkts = []
    for (window, dil), n_side in zip(ATTN_GROUPS, n_side_list):
        j = kk - n_side - qq
        bkts.append(_t5_bucket(j * dil))
    bkt = np.stack(bkts).astype(np.int32)
    return pl.pallas_call(
        functools.partial(_bias_kernel, n_sides=tuple(n_side_list)),
        in_specs=[pl.BlockSpec(memory_space=pltpu.SMEM), pl.BlockSpec(memory_space=pltpu.VMEM)],
        out_specs=pl.BlockSpec(memory_space=pltpu.VMEM),
        out_shape=jax.ShapeDtypeStruct((n_groups, ATTN_EDGE_VARIANTS, HEADS_PER_GROUP, ATTN_QB, 2 * ATTN_QB), F32),
        name="attention_bias_tiles",
    )(rel_bias, jnp.asarray(bkt))


def _attn_kernel(q_ref, k_ref, v_ref, bias_ref, o_ref, l_ref, ks_ref, vs_ref, *, m_len, n_side, tq):
    qt = pl.program_id(2)
    width = HEADS_PER_GROUP * HEAD_DIM
    halo = n_side

    n_cls = q_ref.shape[0]
    n_blk = tq // ATTN_QB

    @pl.when(qt == 0)
    def _():
        zeros = jnp.zeros((halo, width), BF16)
        for c in range(n_cls):
            for ref, src in ((ks_ref, k_ref), (vs_ref, v_ref)):
                ref[c, pl.ds(0, halo), :] = zeros
                ref[c, pl.ds(halo + m_len, 2 * ATTN_QB - halo), :] = jnp.zeros((2 * ATTN_QB - halo, width), BF16)
                ref[c, pl.ds(halo, m_len), :] = src[c]

    lane = lax.broadcasted_iota(jnp.int32, (ATTN_QB, 2 * HEAD_DIM), 1)
    low = lane < HEAD_DIM
    scale = 1.0 / math.sqrt(HEAD_DIM)

    def block(idx, carry):
        c = idx // n_blk
        r0 = pl.multiple_of((idx % n_blk) * ATTN_QB, ATTN_QB)
        q0 = pl.multiple_of(qt * tq + r0, ATTN_QB)
        variant = (q0 == 0).astype(jnp.int32) + 2 * (q0 == m_len - ATTN_QB).astype(jnp.int32)
        for hp in range(HEADS_PER_GROUP // 2):
            cols = pl.ds(hp * 2 * HEAD_DIM, 2 * HEAD_DIM)
            qp = q_ref[c, pl.ds(r0, ATTN_QB), cols] * scale
            kp = ks_ref[c, pl.ds(q0, 2 * ATTN_QB), cols]
            vp = vs_ref[c, pl.ds(q0, 2 * ATTN_QB), cols]
            outs, lses = [], []
            for hh in range(2):
                sel = low if hh == 0 else jnp.logical_not(low)
                qm = jnp.where(sel, qp, jnp.zeros_like(qp))
                s = lax.dot_general(qm, kp, (((1,), (1,)), ((), ())), preferred_element_type=F32)
                s = s + bias_ref[variant, 2 * hp + hh]
                mx = jnp.max(s, axis=-1, keepdims=True)
                p = jnp.exp(s - mx)
                den = jnp.sum(p, axis=-1, keepdims=True)
                pv = jnp.dot(p.astype(BF16), vp, preferred_element_type=F32)
                outs.append(pv / den)
                lses.append(mx + jnp.log(den))
            o_ref[c, pl.ds(r0, ATTN_QB), cols] = jnp.where(low, outs[0], outs[1]).astype(o_ref.dtype)
            l_ref[c, pl.ds(r0, ATTN_QB), cols] = jnp.where(low, lses[0], lses[1])
        return carry
    n_trip_blocks = n_cls * n_blk
    lax.fori_loop(0, n_trip_blocks, block, 0, unroll=next(u for u in (4, 2, 1) if n_trip_blocks % u == 0))


def dilated_group_attention(qkv, g, dil, n_side, bias_tiles):
    nb, d, m_len, _ = qkv.shape
    width = HEADS_PER_GROUP * HEAD_DIM
    assert d == dil and m_len % ATTN_QB == 0 and n_side <= ATTN_QB and n_side % 16 == 0
    tq = min(ATTN_ROWS_PER_STEP, m_len)
    cg = math.gcd(dil, ATTN_ROWS_PER_STEP // tq)
    kernel = functools.partial(_attn_kernel, m_len=m_len, n_side=n_side, tq=tq)
    return pl.pallas_call(
        kernel,
        grid=(nb, dil // cg, m_len // tq),
        in_specs=[
            pl.BlockSpec((None, cg, tq, width), lambda b, r, t: (b, r, t, 0)),
            pl.BlockSpec((None, cg, m_len, width), lambda b, r, t: (b, r, 0, 1)),
            pl.BlockSpec((None, cg, m_len, width), lambda b, r, t: (b, r, 0, 2)),
            pl.BlockSpec((None, ATTN_EDGE_VARIANTS, HEADS_PER_GROUP, ATTN_QB, 2 * ATTN_QB),
                         lambda b, r, t: (g, 0, 0, 0, 0)),
        ],
        out_specs=[
            pl.BlockSpec((None, cg, tq, width), lambda b, r, t: (b, r, t, 0)),
            pl.BlockSpec((None, cg, tq, width), lambda b, r, t: (b, r, t, 0)),
        ],
        out_shape=[
            jax.ShapeDtypeStruct((nb, dil, m_len, width), BF16),
            jax.ShapeDtypeStruct((nb, dil, m_len, width), F32),
        ],
        scratch_shapes=[
            pltpu.VMEM((cg, m_len + 2 * ATTN_QB, width), BF16),
            pltpu.VMEM((cg, m_len + 2 * ATTN_QB, width), BF16),
        ],
        compiler_params=_cparams(("arbitrary", "arbitrary", "arbitrary")),
        name=f"dilated_attention_g{g}",
    )(qkv, qkv, qkv, bias_tiles)


def _merge_kernel(yh_ref, *rest, dils):
    ng = len(dils)
    o_refs, l_refs = rest[:ng], rest[ng:2 * ng]
    gh_ref, ga_ref, bh_ref, ba_ref, wh_ref, wa_ref, out_ref, ya_ref, tok_ref = rest[2 * ng:]
    tm, aw = ya_ref.shape
    nslab = aw // LANES

    @pl.when(pl.program_id(1) == 0)
    def _():
        for gi, d in enumerate(dils):
            if d == 1:
                continue
            rows = tm // d
            for r in range(d):
                for s in range(nslab):
                    cols = slice(s * LANES, (s + 1) * LANES)
                    tok_ref[2 * gi, s, pl.ds(r, rows, stride=d), :] = o_refs[gi][r, :, cols].astype(F32)
                    tok_ref[2 * gi + 1, s, pl.ds(r, rows, stride=d), :] = l_refs[gi][r, :, cols]
        for s in range(nslab):
            cols = slice(s * LANES, (s + 1) * LANES)
            os_, ls_ = [], []
            for gi, d in enumerate(dils):
                if d == 1:
                    os_.append(o_refs[gi][0, :, cols].astype(F32))
                    ls_.append(l_refs[gi][0, :, cols])
                else:
                    os_.append(tok_ref[2 * gi, s])
                    ls_.append(tok_ref[2 * gi + 1, s])
            mx = functools.reduce(jnp.maximum, ls_)
            es = [jnp.exp(l - mx) for l in ls_]
            inv = 1.0 / functools.reduce(lambda a, b: a + b, es)
            ya = functools.reduce(lambda a, b: a + b, [(e * inv) * o for e, o in zip(es, os_)])
            ya_ref[:, cols] = ya.astype(BF16)

    acc_h = jnp.dot(yh_ref[...], wh_ref[...], preferred_element_type=F32)
    acc_a = jnp.dot(ya_ref[...], wa_ref[...], preferred_element_type=F32)
    t_h = jnp.tanh(0.5 * (gh_ref[...].astype(F32) + bh_ref[...]))
    t_a = jnp.tanh(0.5 * (ga_ref[...].astype(F32) + ba_ref[...]))
    out_ref[...] = (0.5 * ((1.0 + t_h) * acc_h + (1.0 + t_a) * acc_a)).astype(out_ref.dtype)


def branch_merge(y_hy, outs, lses, dils, u2, gate_col0, b_gate, w_hy, w_at, seq_len, tm=512):
    t, hw = y_hy.shape
    d = w_hy.shape[1]
    aw = w_at.shape[0]
    tn = _pick_tile(math.gcd(gate_col0, d) if gate_col0 else d, d)
    gb = gate_col0 // tn
    nj = d // tn
    tps = seq_len // tm
    assert all(tm % (dd * 2 * SUBLANES) == 0 for dd in dils)
    grp = [pl.BlockSpec((None, dd, tm // dd, aw), lambda i, j: (i // tps, 0, i % tps, 0)) for dd in dils]
    w_mode = dict(pipeline_mode=pl.Buffered(1)) if nj == 1 else {}
    return pl.pallas_call(
        functools.partial(_merge_kernel, dils=tuple(dils)),
        grid=(t // tm, nj),
        in_specs=[pl.BlockSpec((tm, hw), lambda i, j: (i, 0))] + grp + grp + [
            pl.BlockSpec((tm, tn), lambda i, j: (i, gb + j)),
            pl.BlockSpec((tm, tn), lambda i, j: (i, gb + nj + j)),
            pl.BlockSpec((1, tn), lambda i, j: (0, j)),
            pl.BlockSpec((1, tn), lambda i, j: (0, nj + j)),
            pl.BlockSpec((hw, tn), lambda i, j: (0, j), **w_mode),
            pl.BlockSpec((aw, tn), lambda i, j: (0, j), **w_mode),
        ],
        out_specs=pl.BlockSpec((tm, tn), lambda i, j: (i, j)),
        out_shape=jax.ShapeDtypeStruct((t, d), BF16),
        scratch_shapes=[pltpu.VMEM((tm, aw), BF16), pltpu.VMEM((2 * len(dils), aw // LANES, tm, LANES), F32)],
        compiler_params=_cparams(("arbitrary", "arbitrary")),
        name="branch_merge",
    )(y_hy, *outs, *lses, u2, u2, b_gate, b_gate, w_hy, w_at)


def _out_proj_kernel(x_ref, m_ref, gate_ref, w_ref, o_ref):
    acc = jnp.dot(m_ref[...], w_ref[...], preferred_element_type=F32)
    o_ref[...] = x_ref[...] + gate_ref[0] * acc


def out_proj_residual(x, m, mod3, gate_idx, w, seq_len, tm=512, tn=None, name="out_proj_residual"):
    t, d = x.shape
    k = m.shape[1]
    tn = d if tn is None else tn
    nj = d // tn
    tiles_per_seq = seq_len // tm
    return pl.pallas_call(
        _out_proj_kernel,
        grid=(t // tm, nj),
        in_specs=[
            pl.BlockSpec((tm, tn), lambda i, j: (i, j)),
            pl.BlockSpec((tm, k), lambda i, j: (i, 0)),
            pl.BlockSpec((1, 1, tn), lambda i, j: (i // tiles_per_seq, 0, gate_idx * nj + j)),
            pl.BlockSpec((k, tn), lambda i, j: (0, j)),
        ],
        out_specs=pl.BlockSpec((tm, tn), lambda i, j: (i, j)),
        out_shape=jax.ShapeDtypeStruct((t, d), F32),
        input_output_aliases={0: 0},
        compiler_params=_cparams(("arbitrary", "arbitrary")),
        name=name,
    )(x, m, mod3, w)


def _final_norm_kernel(x_ref, g_ref, o_ref):
    x = x_ref[...]
    ms = jnp.mean(x * x, axis=-1, keepdims=True)
    o_ref[...] = x * lax.rsqrt(ms + NORM_EPS) * g_ref[...]


def final_norm(x, g, row0, n_rows, tm=512):
    d = x.shape[1]
    off = row0 // tm
    return pl.pallas_call(
        _final_norm_kernel,
        grid=(n_rows // tm,),
        in_specs=[pl.BlockSpec((tm, d), lambda i: (off + i, 0)), pl.BlockSpec((1, d), lambda i: (0, 0))],
        out_specs=pl.BlockSpec((tm, d), lambda i: (i, 0)),
        out_shape=jax.ShapeDtypeStruct((n_rows, d), F32),
        compiler_params=_cparams(("arbitrary",)),
        name="final_norm",
    )(x, g.reshape(1, d))


def kernel(x_prompt, x_sample, c_prompt, c_sample, ada_w, ada_b, norm1_g, w_in, b_gate, hy_conv_w, hy_conv_b,
           filt_w1, filt_b1, filt_freq1, filt_w2, filt_b2, filt_freq2, filt_w3, hy_bias, rel_bias, w_br_hy,
           w_br_attn, w_out, norm2_g, ffn_up, ffn_conv_w, ffn_conv_b, ffn_down, final_g):
    bp, L, d = x_prompt.shape
    bs = x_sample.shape[0]
    assert x_sample.shape[1] == L
    nb = bp + bs
    depth = ada_w.shape[0]
    hw = hy_bias.shape[2]
    n_groups = len(ATTN_GROUPS)
    attn_w = n_groups * HEADS_PER_GROUP * HEAD_DIM
    hy_cols = 3 * hw
    gate_col0 = hy_cols + 3 * attn_w
    assert w_in.shape[2] == gate_col0 + 2 * d

    x = jnp.concatenate([x_prompt, x_sample], axis=0).reshape(nb * L, d)
    nb_pad = -(-nb // SUBLANES) * SUBLANES
    c_pad = jnp.zeros((nb_pad, d), F32).at[:nb].set(jnp.concatenate([c_prompt, c_sample], axis=0))
    mod = ada_modulation(c_pad, ada_w, ada_b)

    hy_ct = min(HYENA_CT, hw)
    khat = hyena_filter_spectra(L, filt_w1, filt_b1, filt_freq1, filt_w2, filt_b2, filt_freq2, filt_w3, hy_ct)
    n_sides = [(window // 2) // dil for window, dil in ATTN_GROUPS]
    bias_tiles = attention_bias_tiles(rel_bias, n_sides)

    dils = [dil for _, dil in ATTN_GROUPS]
    gw = HEADS_PER_GROUP * HEAD_DIM

    def group_cols(w, g):
        return [w[:, hy_cols + part * attn_w + g * gw: hy_cols + part * attn_w + (g + 1) * gw] for part in range(3)]

    for l in range(depth):
        mod3 = mod[l].reshape(nb_pad, 1, N_MOD * d)
        w_l = w_in[l]
        w_main = jnp.concatenate([w_l[:, gate_col0:], w_l[:, :hy_cols]], axis=1).astype(BF16)
        w_groups = jnp.concatenate([c for g in range(n_groups) for c in group_cols(w_l, g)], axis=1).astype(BF16)
        u, qkv = in_projection(x, norm1_g[l], mod3, 1, 0, w_main, w_groups, dils, nb, L)
        u3 = u.reshape(nb, L, u.shape[1])
        y_hy = hyena_mix(u3, 2 * d, hy_conv_w[l], hy_conv_b[l].reshape(1, hy_cols), khat, l, hy_bias[l], hy_ct)
        outs, lses = [], []
        for g, dil in enumerate(dils):
            o_g, l_g = dilated_group_attention(qkv[g], g, dil, n_sides[g], bias_tiles)
            outs.append(o_g)
            lses.append(l_g)
        merged = branch_merge(y_hy.reshape(nb * L, hw), outs, lses, dils, u, 0, b_gate[l].reshape(1, 2 * d),
                              w_br_hy[l].astype(BF16), w_br_attn[l].astype(BF16), L)
        x = out_proj_residual(x, merged, mod3, 2, w_out[l].astype(BF16), L)
        ff = ffn_down.shape[1]
        act = ffn_up_act(x, norm2_g[l], mod3, 4, 3, interleave_ffn_up(ffn_up[l].astype(BF16)),
                         ffn_conv_w[l], ffn_conv_b[l].reshape(1, ff), L)
        x = out_proj_residual(x, act, mod3, 5, ffn_down[l].astype(BF16), L, tm=min(1024, L),
                              tn=_pick_tile(d, 512), name="ffn_down_residual")

    y_prompt = final_norm(x, final_g, 0, bp * L).reshape(bp, L, d)
    y_sample = final_norm(x, final_g, bp * L, bs * L).reshape(bs, L, d)
    return (y_prompt, y_sample)
```

```python
import functools
import math

import jax
import jax.numpy as jnp
import numpy as np
from jax import lax
from jax.experimental import pallas as pl
from jax.experimental.pallas import tpu as pltpu

F32 = jnp.float32
BF16 = jnp.bfloat16
HIGHEST = lax.Precision.HIGHEST

NORM_EPS = 1e-6
N_MOD = 6
HYENA_ORDER = 2
N_BANDS = 16
FAST_DECAY_PCT = 0.3
SLOW_DECAY_PCT = 1.5
DECAY_TARGET = 1e-2
ATTN_GROUPS = ((128, 1), (512, 4), (2048, 16))
HEADS_PER_GROUP = 8
HEAD_DIM = 64
REL_BUCKETS = 32
REL_MAX_DIST = 1024
NEG_INF = -1e30

LANES = 128
SUBLANES = 8
VMEM_LIMIT = 56 * 1024 * 1024

FFT_N2 = 128
FFT_GROUP = SUBLANES
FFT_TRIP_GROUPS = 8
HYENA_CT = 256
ATTN_QB = 128
ATTN_ROWS_PER_STEP = 2048


def _cparams(sem):
    return pltpu.CompilerParams(dimension_semantics=sem, vmem_limit_bytes=VMEM_LIMIT)


def _pick_tile(n, cap):
    best = None
    for t in range(LANES, min(n, cap) + 1, LANES):
        if n % t == 0:
            best = t
    assert best is not None, (n, cap)
    return best


def _ada_kernel(c_ref, w_ref, b_ref, o_ref):
    c = c_ref[...]
    cs = c * (1.0 / (1.0 + jnp.exp(-c)))
    o_ref[0] = jnp.dot(cs, w_ref[0], preferred_element_type=F32, precision=HIGHEST) + b_ref[0]


def ada_modulation(c_pad, ada_w, ada_b):
    depth, d, n = ada_w.shape
    nb = c_pad.shape[0]
    tn = _pick_tile(n, 1024)
    return pl.pallas_call(
        _ada_kernel,
        grid=(depth, n // tn),
        in_specs=[
            pl.BlockSpec((nb, d), lambda l, j: (0, 0)),
            pl.BlockSpec((1, d, tn), lambda l, j: (l, 0, j)),
            pl.BlockSpec((1, 1, tn), lambda l, j: (l, 0, j)),
        ],
        out_specs=pl.BlockSpec((1, nb, tn), lambda l, j: (l, 0, j)),
        out_shape=jax.ShapeDtypeStruct((depth, nb, n), F32),
        compiler_params=_cparams(("arbitrary", "arbitrary")),
        name="ada_modulation",
    )(c_pad, ada_w, ada_b.reshape(depth, 1, n))


def _normmod_rows(x, g, sc, sh):
    ms = jnp.mean(x * x, axis=-1, keepdims=True)
    return (x * lax.rsqrt(ms + NORM_EPS)) * (g * (1.0 + sc)) + sh


FFN_HALO = 16


def _ffn_up_act_kernel(x_ref, xp_ref, xn_ref, g_ref, sc_ref, sh_ref, w_ref, cw_ref, cb_ref,
                       o_ref, h_ref, *, tiles_per_seq):
    i = pl.program_id(0)
    tm = x_ref.shape[0]

    @pl.when(pl.program_id(1) == 0)
    def _():
        g, sc, sh = g_ref[...], sc_ref[0], sh_ref[0]
        h_ref[0:FFN_HALO] = _normmod_rows(xp_ref[...], g, sc, sh).astype(BF16)
        h_ref[FFN_HALO:FFN_HALO + tm] = _normmod_rows(x_ref[...], g, sc, sh).astype(BF16)
        h_ref[FFN_HALO + tm:] = _normmod_rows(xn_ref[...], g, sc, sh).astype(BF16)

    first = (i % tiles_per_seq) == 0
    last = (i % tiles_per_seq) == tiles_per_seq - 1
    row = lax.broadcasted_iota(jnp.int32, (tm + 2 * FFN_HALO, 1), 0)
    outside = ((row == FFN_HALO - 1) & first) | ((row == FFN_HALO + tm) & last)
    for c in range(o_ref.shape[1] // LANES):
        res = jnp.dot(h_ref[...], w_ref[:, 2 * c * LANES:2 * (c + 1) * LANES], preferred_element_type=F32)
        a = res[FFN_HALO:FFN_HALO + tm, :LANES]
        gt = jnp.where(outside, 0.0, res[:, LANES:])
        cols = slice(c * LANES, (c + 1) * LANES)
        conv = _dwconv3_rows(gt, cw_ref.at[:, cols], cb_ref.at[:, cols])[FFN_HALO:FFN_HALO + tm]
        half = 0.5 * conv
        o_ref[:, cols] = (half * (1.0 + jnp.tanh(half)) * a).astype(o_ref.dtype)


def interleave_ffn_up(w_up):
    ff = w_up.shape[1] // 2
    chunks = []
    for c in range(ff // LANES):
        chunks += [w_up[:, c * LANES:(c + 1) * LANES], w_up[:, ff + c * LANES:ff + (c + 1) * LANES]]
    return jnp.concatenate(chunks, axis=1)


def ffn_up_act(x, g, mod3, sc_idx, sh_idx, w_il, conv_w, conv_b, seq_len, tm=512, tn_cap=1536):
    t, d = x.shape
    ff = w_il.shape[1] // 2
    tn = _pick_tile(ff, tn_cap)
    nj = ff // tn
    tps = seq_len // tm
    rb = tm // FFN_HALO
    last_rb = t // FFN_HALO - 1
    return pl.pallas_call(
        functools.partial(_ffn_up_act_kernel, tiles_per_seq=tps),
        grid=(t // tm, nj),
        in_specs=[
            pl.BlockSpec((tm, d), lambda i, j: (i, 0)),
            pl.BlockSpec((FFN_HALO, d), lambda i, j: (jnp.maximum(i * rb - 1, 0), 0)),
            pl.BlockSpec((FFN_HALO, d), lambda i, j: (jnp.minimum((i + 1) * rb, last_rb), 0)),
            pl.BlockSpec((1, d), lambda i, j: (0, 0)),
            pl.BlockSpec((1, 1, d), lambda i, j: (i // tps, 0, sc_idx)),
            pl.BlockSpec((1, 1, d), lambda i, j: (i // tps, 0, sh_idx)),
            pl.BlockSpec((d, 2 * tn), lambda i, j: (0, j)),
            pl.BlockSpec((3, tn), lambda i, j: (0, j)),
            pl.BlockSpec((1, tn), lambda i, j: (0, j)),
        ],
        out_specs=pl.BlockSpec((tm, tn), lambda i, j: (i, j)),
        out_shape=jax.ShapeDtypeStruct((t, ff), BF16),
        scratch_shapes=[pltpu.VMEM((tm + 2 * FFN_HALO, d), BF16)],
        compiler_params=_cparams(("arbitrary", "arbitrary")),
        name="ffn_up_act",
    )(x, x, x, g.reshape(1, d), mod3, mod3, w_il, conv_w, conv_b)


def _in_proj_kernel(x_ref, g_ref, sc_ref, sh_ref, wm_ref, wg_ref, om_ref, *rest, n_main, dils, chunk):
    og_refs = rest[:len(dils)]
    h_ref, perm_ref = rest[len(dils):]
    j = pl.program_id(1)
    tm = h_ref.shape[0]

    @pl.when(j == 0)
    def _():
        h_ref[...] = _normmod_rows(x_ref[...], g_ref[...], sc_ref[0], sh_ref[0]).astype(BF16)

    @pl.when(j < n_main)
    def _():
        om_ref[...] = jnp.dot(h_ref[...], wm_ref[...], preferred_element_type=F32).astype(om_ref.dtype)

    for gi, (o_ref, d) in enumerate(zip(og_refs, dils)):
        @pl.when(j == n_main + gi)
        def _():
            rows = tm // d
            for c0 in range(0, wg_ref.shape[1], chunk):
                res = jnp.dot(h_ref[...], wg_ref[:, c0:c0 + chunk], preferred_element_type=F32)
                if d == 1:
                    o_ref[0, 0, :, c0:c0 + chunk] = res.astype(o_ref.dtype)
                    continue
                for s in range(chunk // LANES):
                    perm_ref[s] = res[:, s * LANES:(s + 1) * LANES]
                for r in range(d):
                    for s in range(chunk // LANES):
                        cols = slice(c0 + s * LANES, c0 + (s + 1) * LANES)
                        o_ref[0, r, :, cols] = perm_ref[s, pl.ds(r, rows, stride=d), :].astype(o_ref.dtype)


def in_projection(x, g, mod3, sc_idx, sh_idx, w_main, w_groups, dils, nb, seq_len, tm=512, tn_cap=1792):
    t, d_model = x.shape
    n_main = w_main.shape[1]
    gw = w_groups.shape[1] // len(dils)
    tn = _pick_tile(n_main, tn_cap)
    nj_main = n_main // tn
    tps = seq_len // tm
    chunk = gw // 3
    assert chunk % LANES == 0 and all(tm % (dd * 16) == 0 for dd in dils)
    kernel = functools.partial(_in_proj_kernel, n_main=nj_main, dils=tuple(dils), chunk=chunk)
    outs = pl.pallas_call(
        kernel,
        grid=(t // tm, nj_main + len(dils)),
        in_specs=[
            pl.BlockSpec((tm, d_model), lambda i, j: (i, 0)),
            pl.BlockSpec((1, d_model), lambda i, j: (0, 0)),
            pl.BlockSpec((1, 1, d_model), lambda i, j: (i // tps, 0, sc_idx)),
            pl.BlockSpec((1, 1, d_model), lambda i, j: (i // tps, 0, sh_idx)),
            pl.BlockSpec((d_model, tn), lambda i, j: (0, jnp.minimum(j, nj_main - 1))),
            pl.BlockSpec((d_model, gw), lambda i, j: (0, jnp.clip(j - nj_main, 0, len(dils) - 1))),
        ],
        out_specs=[pl.BlockSpec((tm, tn), lambda i, j: (i, jnp.minimum(j, nj_main - 1)))] + [
            pl.BlockSpec((1, dd, tm // dd, gw), lambda i, j: (i // tps, 0, i % tps, 0)) for dd in dils],
        out_shape=[jax.ShapeDtypeStruct((t, n_main), BF16)] + [
            jax.ShapeDtypeStruct((nb, dd, seq_len // dd, gw), BF16) for dd in dils],
        scratch_shapes=[pltpu.VMEM((tm, d_model), BF16), pltpu.VMEM((chunk // LANES, tm, LANES), F32)],
        compiler_params=_cparams(("arbitrary", "arbitrary")),
        name="in_projection",
    )(x, g.reshape(1, d_model), mod3, mod3, w_main, w_groups)
    return outs[0], list(outs[1:])


def _fft_dims(seq_len):
    n_fft = 2 * seq_len
    n1 = n_fft // FFT_N2
    assert n1 * FFT_N2 == n_fft and n1 % 2 == 0
    nk1 = n1 // 2 + 1
    pad = -(-nk1 // SUBLANES) * SUBLANES
    return n_fft, n1, nk1, pad


@functools.lru_cache(maxsize=None)
def _fft_constants(seq_len):
    n_fft, n1, nk1, pad = _fft_dims(seq_len)
    n2 = FFT_N2
    k1 = np.arange(nk1)[None, :, None]
    i1 = np.arange(n1)[None, None, :]
    i2 = np.arange(n2)[:, None, None]
    ph = 2.0 * np.pi * ((k1 * (i1 * n2 + i2)) % n_fft) / n_fft
    f1 = np.zeros((n2, 2 * pad, n1), np.float64)
    f1[:, :nk1] = np.cos(ph)
    f1[:, pad:pad + nk1] = -np.sin(ph)
    ck = np.full((nk1,), 2.0)
    ck[0] = 1.0
    ck[-1] = 1.0
    g1 = np.zeros((n2, n1 // 2, 2 * pad), np.float64)
    pht = np.transpose(ph[:, :, :n1 // 2], (0, 2, 1))
    g1[:, :, :nk1] = np.cos(pht) * ck / n_fft
    g1[:, :, pad:pad + nk1] = -np.sin(pht) * ck / n_fft
    a = np.arange(n2)
    ph2 = 2.0 * np.pi * ((a[:, None] * a[None, :]) % n2) / n2
    c2, s2 = np.cos(ph2), np.sin(ph2)
    f2 = np.block([[c2, s2], [-s2, c2]])
    g2 = np.block([[c2, -s2], [s2, c2]])
    return (f1.astype(np.float32), f2.astype(np.float32), g2.astype(np.float32), g1.astype(np.float32))


def _interleaved_block_diag(mats):
    n, r, c = mats.shape
    out = np.zeros((n // FFT_GROUP, r * FFT_GROUP, c * FFT_GROUP), mats.dtype)
    for j in range(FFT_GROUP):
        out[:, j::FFT_GROUP, j::FFT_GROUP] = mats[j::FFT_GROUP]
    return out


def _group_rows(g):
    return pl.ds(pl.multiple_of(g * FFT_GROUP, FFT_GROUP), FFT_GROUP)


def _ld_group(ref, g):
    n_outer = ref.shape[1]
    return jnp.concatenate([ref[h, :, _group_rows(g), :].reshape(n_outer * FFT_GROUP, LANES)
                            for h in range(ref.shape[0])], axis=1)


def _st_group(ref, g, val):
    n_outer = ref.shape[1]
    for h in range(ref.shape[0]):
        ref[h, :, _group_rows(g), :] = val[:, h * LANES:(h + 1) * LANES].reshape(n_outer, FFT_GROUP, LANES)


def _ld_outer(ref, idx):
    return jnp.concatenate([ref[h, idx] for h in range(ref.shape[0])], axis=1)


def _st_outer(ref, idx, val):
    for h in range(ref.shape[0]):
        ref[h, idx] = val[:, h * LANES:(h + 1) * LANES]


def _st_rows(ref, row0, val):
    n = val.shape[0]
    for h in range(ref.shape[0]):
        ref[h, row0 // FFT_N2:(row0 + n) // FFT_N2] = val[:, h * LANES:(h + 1) * LANES].reshape(
            n // FFT_N2, FFT_N2, LANES)


def _fft_stage1(src_ref, f1b_ref, a_ref):
    def body(g, carry):
        a = jnp.dot(f1b_ref[g], _ld_group(src_ref, g).astype(BF16), preferred_element_type=F32)
        _st_group(a_ref, g, a)
        return carry
    lax.fori_loop(0, FFT_N2 // FFT_GROUP, body, 0, unroll=FFT_TRIP_GROUPS)


def _fft_stage2_rhs(a_ref, k1, pad):
    return jnp.concatenate([_ld_outer(a_ref, k1), _ld_outer(a_ref, pad + k1)], axis=0).astype(BF16)


def _filter_kernel(feat_ref, featr_ref, w1_ref, b1_ref, fr1_ref, w2_ref, b2_ref, fr2_ref,
                   w3f_ref, w3b_ref, delta_ref, f1_ref, f2_ref,
                   o_ref, hid_ref, k2_ref, a_ref, *, seq_len):
    n_fft, n1, nk1, pad = _fft_dims(seq_len)
    L = seq_len

    @pl.when((pl.program_id(1) == 0) & (pl.program_id(2) == 0))
    def _():
        for d, fref in enumerate((feat_ref, featr_ref)):
            h = jnp.dot(fref[...], w1_ref[0], preferred_element_type=F32, precision=HIGHEST) + b1_ref[0]
            h = jnp.sin(fr1_ref[0] * h)
            h = jnp.dot(h, w2_ref[0], preferred_element_type=F32, precision=HIGHEST) + b2_ref[0]
            hid_ref[d] = jnp.sin(fr2_ref[0] * h)

    delta = delta_ref[...]
    chunk = min(L, 512)
    row = lax.broadcasted_iota(jnp.int32, (chunk, 1), 0)
    l1 = jnp.zeros(delta.shape, F32)
    for d, (w3_ref, fref) in enumerate(((w3f_ref, feat_ref), (w3b_ref, featr_ref))):
        for c in range(L // chunk):
            rows = pl.ds(c * chunk, chunk)
            h = jnp.dot(hid_ref[d, rows, :], w3_ref[0], preferred_element_type=F32, precision=HIGHEST)
            h = h * jnp.exp(-fref[rows, 0:1] * delta)
            if d == 1 and c == 0:
                h = jnp.where(row == 0, 0.0, h)
            l1 = l1 + jnp.sum(jnp.abs(h), axis=0, keepdims=True)
            _st_rows(k2_ref, d * L + c * chunk, h)
    inv = 1.0 / l1
    _fft_stage1(k2_ref, f1_ref, a_ref)

    def body(k1, carry):
        spec = jnp.dot(f2_ref[...], _fft_stage2_rhs(a_ref, k1, pad), preferred_element_type=F32) * inv
        o_ref[0, 0, k1] = spec.astype(o_ref.dtype)
        return carry
    lax.fori_loop(0, nk1, body, 0, unroll=3 if nk1 % 3 == 0 else 1)


def hyena_filter_spectra(seq_len, filt_w1, filt_b1, filt_freq1, filt_w2, filt_b2, filt_freq2, filt_w3, ct):
    depth, n_feat, hid = filt_w1.shape
    hw = filt_w3.shape[2] // (2 * HYENA_ORDER)
    n_fft, n1, nk1, pad = _fft_dims(seq_len)
    L = seq_len
    t = np.linspace(0.0, 1.0, L, dtype=np.float32).astype(np.float64)[:, None]
    pos = np.arange(L, dtype=np.float64)[:, None]
    bands = np.linspace(1e-4, N_BANDS - 1, N_BANDS, dtype=np.float32).astype(np.float64)[None, :]
    ang = (2.0 * math.pi / L) * pos * bands
    feat = np.concatenate([t, np.cos(ang), -np.sin(ang)], axis=-1)
    assert feat.shape[1] == n_feat
    feat_rev = np.concatenate([np.zeros((1, n_feat)), feat[:0:-1]], axis=0)
    n_feat_pad = -(-n_feat // LANES) * LANES
    feat = np.pad(feat, ((0, 0), (0, n_feat_pad - n_feat)))
    feat_rev = np.pad(feat_rev, ((0, 0), (0, n_feat_pad - n_feat)))
    filt_w1 = jnp.pad(filt_w1, ((0, 0), (0, n_feat_pad - n_feat), (0, 0)))
    n_feat = n_feat_pad
    max_decay = math.log(DECAY_TARGET) / FAST_DECAY_PCT
    min_decay = math.log(DECAY_TARGET) / SLOW_DECAY_PCT
    deltas = np.abs(np.linspace(min_decay, max_decay, hw, dtype=np.float32))[None, :]
    f1, f2, _, _ = _fft_constants(seq_len)
    f1 = _interleaved_block_diag(f1)
    nct = hw // ct
    nslab = ct // LANES
    assert L % FFT_N2 == 0 and (L < 512 or L % 512 == 0)

    def w3_spec(direction):
        return pl.BlockSpec((1, hid, ct), lambda l, c, o: (l, 0, (direction * HYENA_ORDER + o) * nct + c))

    def const(shape):
        return pl.BlockSpec(shape, lambda l, c, o: (0,) * len(shape), pipeline_mode=pl.Buffered(1))

    vec = lambda a: a.reshape(depth, 1, hid)
    vspec = pl.BlockSpec((1, 1, hid), lambda l, c, o: (l, 0, 0))
    return pl.pallas_call(
        functools.partial(_filter_kernel, seq_len=seq_len),
        grid=(depth, nct, HYENA_ORDER),
        in_specs=[
            const((L, n_feat)), const((L, n_feat)),
            pl.BlockSpec((1, n_feat, hid), lambda l, c, o: (l, 0, 0)), vspec, vspec,
            pl.BlockSpec((1, hid, hid), lambda l, c, o: (l, 0, 0)), vspec, vspec,
            w3_spec(0), w3_spec(1),
            pl.BlockSpec((1, ct), lambda l, c, o: (0, c)),
            const(f1.shape), const((2 * FFT_N2, 2 * FFT_N2)),
        ],
        out_specs=pl.BlockSpec((1, 1, nk1, 2 * FFT_N2, ct), lambda l, c, o: (l, o, 0, 0, c)),
        out_shape=jax.ShapeDtypeStruct((depth, HYENA_ORDER, nk1, 2 * FFT_N2, hw), BF16),
        scratch_shapes=[
            pltpu.VMEM((2, L, hid), F32),
            pltpu.VMEM((nslab, n1, FFT_N2, LANES), F32),
            pltpu.VMEM((nslab, 2 * pad, FFT_N2, LANES), F32),
        ],
        compiler_params=_cparams(("arbitrary", "arbitrary", "arbitrary")),
        name="hyena_filter_spectra",
    )(jnp.asarray(feat, F32), jnp.asarray(feat_rev, F32), filt_w1, vec(filt_b1), vec(filt_freq1),
      filt_w2, vec(filt_b2), vec(filt_freq2), filt_w3, filt_w3,
      jnp.asarray(deltas, F32), jnp.asarray(f1, BF16), jnp.asarray(f2, BF16))


def _dwconv3_rows(x, w_ref, b_ref, prev_row=None, next_row=None):
    n = x.shape[0]
    row = lax.broadcasted_iota(jnp.int32, (SUBLANES, 1), 0)
    xm = pltpu.roll(x, 1, axis=0)
    xp = pltpu.roll(x, n - 1, axis=0)
    head = jnp.where(row == 0, 0.0 if prev_row is None else prev_row, xm[:SUBLANES])
    tail = jnp.where(row == SUBLANES - 1, 0.0 if next_row is None else next_row, xp[n - SUBLANES:])
    xm = jnp.concatenate([head, xm[SUBLANES:]], axis=0)
    xp = jnp.concatenate([xp[:n - SUBLANES], tail], axis=0)
    return xm * w_ref[0:1, :] + x * w_ref[1:2, :] + xp * w_ref[2:3, :] + b_ref[...]


def _dwconv3_to_slabs(src_ref, w_ref, b_ref, dst_ref, seq_len, chunk):
    halo = 16
    n_chunks = seq_len // chunk
    for c in range(n_chunks):
        r0 = c * chunk
        x = src_ref[0, r0:r0 + chunk, :].astype(F32)
        prev_row = src_ref[0, r0 - halo:r0, :].astype(F32)[halo - 1:halo] if c > 0 else None
        next_row = src_ref[0, r0 + chunk:r0 + chunk + halo, :].astype(F32)[0:1] if c < n_chunks - 1 else None
        _st_rows(dst_ref, r0, _dwconv3_rows(x, w_ref, b_ref, prev_row, next_row))


def _hyena_conv_kernel(zin_ref, ux_ref, wz_ref, bz_ref, wx_ref, bx_ref, hb_ref, kh_ref,
                       f1_ref, f2_ref, g2_ref, g1_ref, o_ref, z_ref, xg_ref, a_ref, y_ref, *, seq_len, first):
    n_fft, n1, nk1, pad = _fft_dims(seq_len)
    L = seq_len
    half = n1 // 2
    nslab = z_ref.shape[0]
    chunk = min(L, 512)

    if first:
        _dwconv3_to_slabs(zin_ref, wz_ref, bz_ref, z_ref, L, chunk)
    else:
        for c in range(L // chunk):
            _st_rows(z_ref, c * chunk, zin_ref[0, c * chunk:(c + 1) * chunk, :])
    _dwconv3_to_slabs(ux_ref, wx_ref, bx_ref, xg_ref, L, chunk)

    _fft_stage1(z_ref, f1_ref, a_ref)

    u2 = 3 if nk1 % 3 == 0 else 1
    n_trips = nk1 // u2

    def spectrum_product(t):
        ys = []
        for j in range(u2):
            k1 = t * u2 + j
            x = jnp.dot(f2_ref[...], _fft_stage2_rhs(a_ref, k1, pad), preferred_element_type=F32)
            kk = kh_ref[0, 0, k1].astype(F32)
            xr, xi = x[:FFT_N2], x[FFT_N2:]
            kr, ki = kk[:FFT_N2], kk[FFT_N2:]
            ys.append(jnp.concatenate([xr * kr - xi * ki, xr * ki + xi * kr], axis=0).astype(BF16))
        return ys

    def inverse_dots(slot):
        return [jnp.dot(g2_ref[...], y_ref[slot, j], preferred_element_type=F32) for j in range(u2)]

    def store_inverse(t, bks):
        for j, bk in enumerate(bks):
            _st_outer(a_ref, t * u2 + j, bk[:FFT_N2])
            _st_outer(a_ref, pad + t * u2 + j, bk[FFT_N2:])

    def store_products(slot, ys):
        for j, y in enumerate(ys):
            y_ref[slot, j] = y

    store_products(0, spectrum_product(0))

    def body2(t, carry):
        ys = spectrum_product(t)
        bks = inverse_dots((t + 1) % 2)
        store_inverse(t - 1, bks)
        store_products(t % 2, ys)
        return carry
    lax.fori_loop(1, n_trips, body2, 0)
    store_inverse(n_trips - 1, inverse_dots((n_trips - 1) % 2))

    bias = hb_ref[...]
    u3 = FFT_TRIP_GROUPS

    def body3(t, carry):
        gs = [t * u3 + j for j in range(u3)]
        ins = [(_ld_group(a_ref, g).astype(BF16), _ld_group(xg_ref, g), _ld_group(z_ref, g)) for g in gs]
        outs = [xg * (jnp.dot(g1_ref[g], bb, preferred_element_type=F32) + bias * z_old)
                for g, (bb, xg, z_old) in zip(gs, ins)]
        for g, z_new in zip(gs, outs):
            _st_group(z_ref, g, z_new)
        return carry
    lax.fori_loop(0, FFT_N2 // FFT_GROUP // u3, body3, 0)

    for c in range(L // chunk):
        blk = jnp.concatenate([z_ref[h, c * chunk // FFT_N2:(c + 1) * chunk // FFT_N2].reshape(chunk, LANES)
                               for h in range(nslab)], axis=1)
        o_ref[0, c * chunk:(c + 1) * chunk, :] = blk.astype(o_ref.dtype)


def hyena_step(zin, zin_col0, u3, x_col0, w_cols, conv_w, conv_b, khat, layer, order, hy_bias_row, first, ct):
    nb, L, _ = u3.shape
    hw = hy_bias_row.shape[1]
    n_fft, n1, nk1, pad = _fft_dims(L)
    half = n1 // 2
    f1, f2, g2, g1 = _fft_constants(L)
    f1 = _interleaved_block_diag(f1[:, :, :half])
    g1 = _interleaved_block_diag(g1)
    nct = hw // ct
    nslab = ct // LANES
    assert zin_col0 % ct == 0 and x_col0 % ct == 0 and w_cols[0] % ct == 0 and w_cols[1] % ct == 0
    last = order == HYENA_ORDER - 1

    def col_spec(rows, col0):
        return pl.BlockSpec((rows, ct), lambda c, b: (0, col0 // ct + c))

    const = lambda shape: pl.BlockSpec(shape, lambda c, b: (0,) * len(shape), pipeline_mode=pl.Buffered(1))
    return pl.pallas_call(
        functools.partial(_hyena_conv_kernel, seq_len=L, first=first),
        grid=(nct, nb),
        in_specs=[
            pl.BlockSpec((1, L, ct), lambda c, b: (b, 0, zin_col0 // ct + c)),
            pl.BlockSpec((1, L, ct), lambda c, b: (b, 0, x_col0 // ct + c)),
            col_spec(3, w_cols[0]), col_spec(1, w_cols[0]), col_spec(3, w_cols[1]), col_spec(1, w_cols[1]),
            pl.BlockSpec((1, ct), lambda c, b: (0, c)),
            pl.BlockSpec((1, 1, nk1, 2 * FFT_N2, ct), lambda c, b: (layer, order, 0, 0, c),
                         pipeline_mode=pl.Buffered(1)),
            const(f1.shape), const((2 * FFT_N2, 2 * FFT_N2)),
            const((2 * FFT_N2, 2 * FFT_N2)), const(g1.shape),
        ],
        out_specs=pl.BlockSpec((1, L, ct), lambda c, b: (b, 0, c)),
        out_shape=jax.ShapeDtypeStruct((nb, L, hw), BF16 if last else F32),
        scratch_shapes=[
            pltpu.VMEM((nslab, half, FFT_N2, LANES), F32),
            pltpu.VMEM((nslab, half, FFT_N2, LANES), F32),
            pltpu.VMEM((nslab, 2 * pad, FFT_N2, LANES), F32),
            pltpu.VMEM((2, 3 if nk1 % 3 == 0 else 1, 2 * FFT_N2, ct), BF16),
        ],
        compiler_params=_cparams(("arbitrary", "arbitrary")),
        name=f"hyena_step{order}",
    )(zin, u3, conv_w, conv_b, conv_w, conv_b, hy_bias_row, khat,
      jnp.asarray(f1, BF16), jnp.asarray(f2, BF16), jnp.asarray(g2, BF16), jnp.asarray(g1, BF16))


def hyena_mix(u3, col0, conv_w, conv_b, khat, layer, hy_bias, ct):
    hw = hy_bias.shape[1]
    z = hyena_step(u3, col0, u3, col0 + hw, (0, hw), conv_w, conv_b, khat, layer, 0, hy_bias[0:1], True, ct)
    return hyena_step(z, 0, u3, col0 + 2 * hw, (0, 2 * hw), conv_w, conv_b, khat, layer, 1, hy_bias[1:2], False, ct)


def _t5_bucket(rel):
    nb = REL_BUCKETS // 2
    ret = (rel > 0).astype(np.int32) * nb
    n = np.abs(rel)
    max_exact = nb // 2
    large = max_exact + (np.log(np.maximum(n, 1) / max_exact) / np.log(REL_MAX_DIST / max_exact)
                         * (nb - max_exact)).astype(np.int32)
    large = np.minimum(large, nb - 1)
    return ret + np.where(n < max_exact, n, large)


ATTN_EDGE_VARIANTS = 3


def _attn_window_shift(variant, n_side):
    return (n_side, 0, ATTN_QB)[variant]


def _bias_kernel(rb_ref, bkt_ref, o_ref, *, n_sides):
    n_groups = bkt_ref.shape[0]
    qq = lax.broadcasted_iota(jnp.int32, (ATTN_QB, 2 * ATTN_QB), 0)
    kk = lax.broadcasted_iota(jnp.int32, (ATTN_QB, 2 * ATTN_QB), 1)
    for g in range(n_groups):
        n_side = n_sides[g]
        for v in range(ATTN_EDGE_VARIANTS):
            rel = kk - _attn_window_shift(v, n_side) - qq
            band = (rel >= -n_side) & (rel <= n_side)
            bkt = bkt_ref[g, v]
            for h in range(HEADS_PER_GROUP):
                tile = jnp.zeros(bkt.shape, F32)
                for b in range(REL_BUCKETS):
                    tile = jnp.where(bkt == b, rb_ref[b, g * HEADS_PER_GROUP + h], tile)
                o_ref[g, v, h] = jnp.where(band, tile, NEG_INF)


def attention_bias_tiles(rel_bias, n_side_list):
    n_groups = len(ATTN_GROUPS)
    qq = np.arange(ATTN_QB)[:, None]
    kk = np.arange(2 * ATTN_QB)[None, :]
    bkts = []
    for (window, dil), n_side in zip(ATTN_GROUPS, n_side_list):
        assert 2 * n_side <= ATTN_QB
        bkts.append([_t5_bucket((kk - _attn_window_shift(v, n_side) - qq) * dil)
                     for v in range(ATTN_EDGE_VARIANTS)])
    bkt = np.asarray(bkts).astype(np.int32)
    return pl.pallas_call(
        functools.partial(_bias_kernel, n_sides=tuple(n_side_list)),
        in_specs=[pl.BlockSpec(memory_space=pltpu.SMEM), pl.BlockSpec(memory_space=pltpu.VMEM)],
        out_specs=pl.BlockSpec(memory_space=pltpu.VMEM),
        out_shape=jax.ShapeDtypeStruct((n_groups, ATTN_EDGE_VARIANTS, HEADS_PER_GROUP, ATTN_QB, 2 * ATTN_QB), F32),
        name="attention_bias_tiles",
    )(rel_bias, jnp.asarray(bkt))


def _attn_kernel(q_ref, k_ref, v_ref, bias_ref, o_ref, l_ref, *, m_len, n_side, tq):
    qt = pl.program_id(2)
    n_cls = q_ref.shape[0]
    n_blk = tq // ATTN_QB

    lane = lax.broadcasted_iota(jnp.int32, (ATTN_QB, 2 * HEAD_DIM), 1)
    low = lane < HEAD_DIM
    scale = 1.0 / math.sqrt(HEAD_DIM)

    def block(idx, carry):
        c = idx // n_blk
        r0 = pl.multiple_of((idx % n_blk) * ATTN_QB, ATTN_QB)
        q0 = qt * tq + r0
        variant = jnp.where(q0 == 0, 1, jnp.where(q0 == m_len - ATTN_QB, 2, 0))
        w0 = pl.multiple_of(jnp.clip(q0 - n_side, 0, m_len - 2 * ATTN_QB), 16)
        for hp in range(HEADS_PER_GROUP // 2):
            cols = pl.ds(hp * 2 * HEAD_DIM, 2 * HEAD_DIM)
            qp = q_ref[c, pl.ds(r0, ATTN_QB), cols] * scale
            kp = k_ref[c, pl.ds(w0, 2 * ATTN_QB), cols]
            vp = v_ref[c, pl.ds(w0, 2 * ATTN_QB), cols]
            outs, lses = [], []
            for hh in range(2):
                sel = low if hh == 0 else jnp.logical_not(low)
                qm = jnp.where(sel, qp, jnp.zeros_like(qp))
                s = lax.dot_general(qm, kp, (((1,), (1,)), ((), ())), preferred_element_type=F32)
                s = s + bias_ref[variant, 2 * hp + hh]
                mx = jnp.max(s, axis=-1, keepdims=True)
                p = jnp.exp(s - mx)
                den = jnp.sum(p, axis=-1, keepdims=True)
                pv = jnp.dot(p.astype(BF16), vp, preferred_element_type=F32)
                outs.append(pv / den)
                lses.append(mx + jnp.log(den))
            o_ref[c, pl.ds(r0, ATTN_QB), cols] = jnp.where(low, outs[0], outs[1]).astype(o_ref.dtype)
            l_ref[c, pl.ds(r0, ATTN_QB), cols] = jnp.where(low, lses[0], lses[1])
        return carry
    n_trip_blocks = n_cls * n_blk
    lax.fori_loop(0, n_trip_blocks, block, 0, unroll=next(u for u in (4, 2, 1) if n_trip_blocks % u == 0))


def dilated_group_attention(qkv, g, dil, n_side, bias_tiles):
    nb, d, m_len, _ = qkv.shape
    width = HEADS_PER_GROUP * HEAD_DIM
    assert d == dil and m_len % ATTN_QB == 0 and m_len >= 2 * ATTN_QB and 2 * n_side <= ATTN_QB and n_side % 16 == 0
    tq = min(ATTN_ROWS_PER_STEP, m_len)
    cg = math.gcd(dil, ATTN_ROWS_PER_STEP // tq)
    kernel = functools.partial(_attn_kernel, m_len=m_len, n_side=n_side, tq=tq)
    return pl.pallas_call(
        kernel,
        grid=(nb, dil // cg, m_len // tq),
        in_specs=[
            pl.BlockSpec((None, cg, tq, width), lambda b, r, t: (b, r, t, 0)),
            pl.BlockSpec((None, cg, m_len, width), lambda b, r, t: (b, r, 0, 1)),
            pl.BlockSpec((None, cg, m_len, width), lambda b, r, t: (b, r, 0, 2)),
            pl.BlockSpec((None, ATTN_EDGE_VARIANTS, HEADS_PER_GROUP, ATTN_QB, 2 * ATTN_QB),
                         lambda b, r, t: (g, 0, 0, 0, 0)),
        ],
        out_specs=[
            pl.BlockSpec((None, cg, tq, width), lambda b, r, t: (b, r, t, 0)),
            pl.BlockSpec((None, cg, tq, width), lambda b, r, t: (b, r, t, 0)),
        ],
        out_shape=[
            jax.ShapeDtypeStruct((nb, dil, m_len, width), BF16),
            jax.ShapeDtypeStruct((nb, dil, m_len, width), F32),
        ],
        compiler_params=_cparams(("arbitrary", "arbitrary", "arbitrary")),
        name=f"dilated_attention_g{g}",
    )(qkv, qkv, qkv, bias_tiles)


def _merge_kernel(yh_ref, *rest, dils):
    ng = len(dils)
    o_refs, l_refs = rest[:ng], rest[ng:2 * ng]
    gh_ref, ga_ref, bh_ref, ba_ref, wh_ref, wa_ref, out_ref, ya_ref, tok_ref = rest[2 * ng:]
    tm, aw = ya_ref.shape
    nslab = aw // LANES

    @pl.when(pl.program_id(1) == 0)
    def _():
        for gi, d in enumerate(dils):
            if d == 1:
                continue
            rows = tm // d
            for r in range(d):
                for s in range(nslab):
                    cols = slice(s * LANES, (s + 1) * LANES)
                    tok_ref[2 * gi, s, pl.ds(r, rows, stride=d), :] = o_refs[gi][r, :, cols].astype(F32)
                    tok_ref[2 * gi + 1, s, pl.ds(r, rows, stride=d), :] = l_refs[gi][r, :, cols]
        for s in range(nslab):
            cols = slice(s * LANES, (s + 1) * LANES)
            os_, ls_ = [], []
            for gi, d in enumerate(dils):
                if d == 1:
                    os_.append(o_refs[gi][0, :, cols].astype(F32))
                    ls_.append(l_refs[gi][0, :, cols])
                else:
                    os_.append(tok_ref[2 * gi, s])
                    ls_.append(tok_ref[2 * gi + 1, s])
            mx = functools.reduce(jnp.maximum, ls_)
            es = [jnp.exp(l - mx) for l in ls_]
            inv = 1.0 / functools.reduce(lambda a, b: a + b, es)
            ya = functools.reduce(lambda a, b: a + b, [(e * inv) * o for e, o in zip(es, os_)])
            ya_ref[:, cols] = ya.astype(BF16)

    acc_h = jnp.dot(yh_ref[...], wh_ref[...], preferred_element_type=F32)
    acc_a = jnp.dot(ya_ref[...], wa_ref[...], preferred_element_type=F32)
    t_h = jnp.tanh(0.5 * (gh_ref[...].astype(F32) + bh_ref[...]))
    t_a = jnp.tanh(0.5 * (ga_ref[...].astype(F32) + ba_ref[...]))
    out_ref[...] = (0.5 * ((1.0 + t_h) * acc_h + (1.0 + t_a) * acc_a)).astype(out_ref.dtype)


def branch_merge(y_hy, outs, lses, dils, u2, gate_col0, b_gate, w_hy, w_at, seq_len, tm=512):
    t, hw = y_hy.shape
    d = w_hy.shape[1]
    aw = w_at.shape[0]
    tn = _pick_tile(math.gcd(gate_col0, d) if gate_col0 else d, d)
    gb = gate_col0 // tn
    nj = d // tn
    tps = seq_len // tm
    assert all(tm % (dd * 2 * SUBLANES) == 0 for dd in dils)
    grp = [pl.BlockSpec((None, dd, tm // dd, aw), lambda i, j: (i // tps, 0, i % tps, 0)) for dd in dils]
    w_mode = dict(pipeline_mode=pl.Buffered(1)) if nj == 1 else {}
    return pl.pallas_call(
        functools.partial(_merge_kernel, dils=tuple(dils)),
        grid=(t // tm, nj),
        in_specs=[pl.BlockSpec((tm, hw), lambda i, j: (i, 0))] + grp + grp + [
            pl.BlockSpec((tm, tn), lambda i, j: (i, gb + j)),
            pl.BlockSpec((tm, tn), lambda i, j: (i, gb + nj + j)),
            pl.BlockSpec((1, tn), lambda i, j: (0, j)),
            pl.BlockSpec((1, tn), lambda i, j: (0, nj + j)),
            pl.BlockSpec((hw, tn), lambda i, j: (0, j), **w_mode),
            pl.BlockSpec((aw, tn), lambda i, j: (0, j), **w_mode),
        ],
        out_specs=pl.BlockSpec((tm, tn), lambda i, j: (i, j)),
        out_shape=jax.ShapeDtypeStruct((t, d), BF16),
        scratch_shapes=[pltpu.VMEM((tm, aw), BF16), pltpu.VMEM((2 * len(dils), aw // LANES, tm, LANES), F32)],
        compiler_params=_cparams(("arbitrary", "arbitrary")),
        name="branch_merge",
    )(y_hy, *outs, *lses, u2, u2, b_gate, b_gate, w_hy, w_at)


def _out_proj_kernel(x_ref, m_ref, gate_ref, w_ref, o_ref):
    acc = jnp.dot(m_ref[...], w_ref[...], preferred_element_type=F32)
    o_ref[...] = x_ref[...] + gate_ref[0] * acc


def out_proj_residual(x, m, mod3, gate_idx, w, seq_len, tm=512, tn=None, name="out_proj_residual"):
    t, d = x.shape
    k = m.shape[1]
    tn = d if tn is None else tn
    nj = d // tn
    tiles_per_seq = seq_len // tm
    return pl.pallas_call(
        _out_proj_kernel,
        grid=(t // tm, nj),
        in_specs=[
            pl.BlockSpec((tm, tn), lambda i, j: (i, j)),
            pl.BlockSpec((tm, k), lambda i, j: (i, 0)),
            pl.BlockSpec((1, 1, tn), lambda i, j: (i // tiles_per_seq, 0, gate_idx * nj + j)),
            pl.BlockSpec((k, tn), lambda i, j: (0, j)),
        ],
        out_specs=pl.BlockSpec((tm, tn), lambda i, j: (i, j)),
        out_shape=jax.ShapeDtypeStruct((t, d), F32),
        input_output_aliases={0: 0},
        compiler_params=_cparams(("arbitrary", "arbitrary")),
        name=name,
    )(x, m, mod3, w)


def _final_norm_kernel(x_ref, g_ref, o_ref):
    x = x_ref[...]
    ms = jnp.mean(x * x, axis=-1, keepdims=True)
    o_ref[...] = x * lax.rsqrt(ms + NORM_EPS) * g_ref[...]


def final_norm(x, g, row0, n_rows, tm=512):
    d = x.shape[1]
    off = row0 // tm
    return pl.pallas_call(
        _final_norm_kernel,
        grid=(n_rows // tm,),
        in_specs=[pl.BlockSpec((tm, d), lambda i: (off + i, 0)), pl.BlockSpec((1, d), lambda i: (0, 0))],
        out_specs=pl.BlockSpec((tm, d), lambda i: (i, 0)),
        out_shape=jax.ShapeDtypeStruct((n_rows, d), F32),
        compiler_params=_cparams(("arbitrary",)),
        name="final_norm",
    )(x, g.reshape(1, d))


def kernel(x_prompt, x_sample, c_prompt, c_sample, ada_w, ada_b, norm1_g, w_in, b_gate, hy_conv_w, hy_conv_b,
           filt_w1, filt_b1, filt_freq1, filt_w2, filt_b2, filt_freq2, filt_w3, hy_bias, rel_bias, w_br_hy,
           w_br_attn, w_out, norm2_g, ffn_up, ffn_conv_w, ffn_conv_b, ffn_down, final_g):
    bp, L, d = x_prompt.shape
    bs = x_sample.shape[0]
    assert x_sample.shape[1] == L
    nb = bp + bs
    depth = ada_w.shape[0]
    hw = hy_bias.shape[2]
    n_groups = len(ATTN_GROUPS)
    attn_w = n_groups * HEADS_PER_GROUP * HEAD_DIM
    hy_cols = 3 * hw
    gate_col0 = hy_cols + 3 * attn_w
    assert w_in.shape[2] == gate_col0 + 2 * d

    x = jnp.concatenate([x_prompt, x_sample], axis=0).reshape(nb * L, d)
    nb_pad = -(-nb // SUBLANES) * SUBLANES
    c_pad = jnp.zeros((nb_pad, d), F32).at[:nb].set(jnp.concatenate([c_prompt, c_sample], axis=0))
    mod = ada_modulation(c_pad, ada_w, ada_b)

    hy_ct = min(HYENA_CT, hw)
    khat = hyena_filter_spectra(L, filt_w1, filt_b1, filt_freq1, filt_w2, filt_b2, filt_freq2, filt_w3, hy_ct)
    n_sides = [(window // 2) // dil for window, dil in ATTN_GROUPS]
    bias_tiles = attention_bias_tiles(rel_bias, n_sides)

    dils = [dil for _, dil in ATTN_GROUPS]
    gw = HEADS_PER_GROUP * HEAD_DIM

    def group_cols(w, g):
        return [w[:, hy_cols + part * attn_w + g * gw: hy_cols + part * attn_w + (g + 1) * gw] for part in range(3)]

    for l in range(depth):
        mod3 = mod[l].reshape(nb_pad, 1, N_MOD * d)
        w_l = w_in[l]
        w_main = jnp.concatenate([w_l[:, gate_col0:], w_l[:, :hy_cols]], axis=1).astype(BF16)
        w_groups = jnp.concatenate([c for g in range(n_groups) for c in group_cols(w_l, g)], axis=1).astype(BF16)
        u, qkv = in_projection(x, norm1_g[l], mod3, 1, 0, w_main, w_groups, dils, nb, L)
        u3 = u.reshape(nb, L, u.shape[1])
        y_hy = hyena_mix(u3, 2 * d, hy_conv_w[l], hy_conv_b[l].reshape(1, hy_cols), khat, l, hy_bias[l], hy_ct)
        outs, lses = [], []
        for g, dil in enumerate(dils):
            o_g, l_g = dilated_group_attention(qkv[g], g, dil, n_sides[g], bias_tiles)
            outs.append(o_g)
            lses.append(l_g)
        merged = branch_merge(y_hy.reshape(nb * L, hw), outs, lses, dils, u, 0, b_gate[l].reshape(1, 2 * d),
                              w_br_hy[l].astype(BF16), w_br_attn[l].astype(BF16), L)
        x = out_proj_residual(x, merged, mod3, 2, w_out[l].astype(BF16), L)
        ff = ffn_down.shape[1]
        act = ffn_up_act(x, norm2_g[l], mod3, 4, 3, interleave_ffn_up(ffn_up[l].astype(BF16)),
                         ffn_conv_w[l], ffn_conv_b[l].reshape(1, ff), L)
        x = out_proj_residual(x, act, mod3, 5, ffn_down[l].astype(BF16), L, tm=min(1024, L),
                              tn=_pick_tile(d, 512), name="ffn_down_residual")

    y_prompt = final_norm(x, final_g, 0, bp * L).reshape(bp, L, d)
    y_sample = final_norm(x, final_g, bp * L, bs * L).reshape(bs, L, d)
    return (y_prompt, y_sample)
```
